```python
import jax
import jax.numpy as jnp
from jax import lax
import numpy as np

D_MODEL = 1024
BATCH = 32
SEQ = 2048
DEPTH = 1

ATTN_HEAD_DIM = 64
ATTN_WIDTH = D_MODEL // 2
ATTN_HEADS = ATTN_WIDTH // ATTN_HEAD_DIM
HGRN_EXPAND = 128
HGRN_WIDTH = D_MODEL - ATTN_WIDTH
HGRN_HEADS = HGRN_WIDTH // HGRN_EXPAND
HGRN_HEAD_V = HGRN_WIDTH // HGRN_HEADS
IN_SPLITS = (ATTN_WIDTH, ATTN_WIDTH, ATTN_WIDTH, ATTN_HEADS,
             HGRN_WIDTH, HGRN_WIDTH, HGRN_WIDTH, HGRN_WIDTH)
IN_PROJ_WIDTH = 3 * ATTN_WIDTH + ATTN_HEADS + 4 * HGRN_WIDTH
Q_BLOCK = 128
HGRN_CHUNK = 64
N_GROUPS = 4
EXPERTS_PER_GROUP = 4
N_EXPERTS = N_GROUPS * EXPERTS_PER_GROUP
TOP_K_IN_GROUP = 2
D_EXPERT = D_MODEL // 2
NORM_EPS = 1e-6

kernel_name = "hybrid_fox_hgrn2_hmoe_adaln_layer"


def rms_norm(x, g):
    xf = x.astype(jnp.float32)
    y = xf * lax.rsqrt(jnp.mean(xf * xf, axis=-1, keepdims=True) + NORM_EPS)
    return (y * g.astype(jnp.float32)).astype(x.dtype)


def split_heads(t, n_heads):
    b, s, w = t.shape
    return t.reshape(b, s, n_heads, w // n_heads).transpose(0, 2, 1, 3)


def merge_heads(t):
    b, h, s, d = t.shape
    return t.transpose(0, 2, 1, 3).reshape(b, s, h * d)


def forgetting_attention(q, k, v, log_f):
    s_len, dh = q.shape[2], q.shape[3]
    scale = dh ** -0.5
    cum = jnp.cumsum(log_f, axis=-1)
    outs = []
    for i in range(s_len // Q_BLOCK):
        q0 = i * Q_BLOCK
        k_end = q0 + Q_BLOCK
        scores = jnp.einsum('bhqd,bhkd->bhqk', q[:, :, q0:k_end], k[:, :, :k_end]).astype(jnp.float32) * scale
        scores = scores + cum[:, :, q0:k_end, None] - cum[:, :, None, :k_end]
        causal = (q0 + jnp.arange(Q_BLOCK))[:, None] >= jnp.arange(k_end)[None, :]
        scores = jnp.where(causal, scores, -jnp.inf)
        probs = jax.nn.softmax(scores, axis=-1).astype(v.dtype)
        outs.append(jnp.einsum('bhqk,bhkd->bhqd', probs, v[:, :, :k_end]))
    return jnp.concatenate(outs, axis=2)


def hgrn2_chunkwise(q, k, v, log_f):
    b, h, s_len, dk = q.shape
    dv = v.shape[-1]
    n_chunks = s_len // HGRN_CHUNK

    def to_chunks(t):
        return t.reshape(b, h, n_chunks, HGRN_CHUNK, t.shape[-1]).transpose(2, 0, 1, 3, 4)

    tri = jnp.tril(jnp.ones((HGRN_CHUNK, HGRN_CHUNK), dtype=bool))[None, None, :, :, None]

    def step(state, inp):
        qc, kc, vc, gc = inp
        cum = jnp.cumsum(gc, axis=2)
        diff = jnp.where(tri, cum[:, :, :, None, :] - cum[:, :, None, :, :], -jnp.inf)
        scores = jnp.einsum('bhtk,bhtsk,bhsk->bhts', qc, jnp.exp(diff), kc)
        out = jnp.einsum('bhts,bhsv->bhtv', scores, vc) + jnp.einsum('bhtk,bhkv->bhtv', qc * jnp.exp(cum), state)
        last = cum[:, :, -1:, :]
        state = jnp.exp(last[:, :, 0, :])[..., None] * state + jnp.einsum('bhsk,bhsv->bhkv', kc * jnp.exp(last - cum), vc)
        return state, out

    state0 = jnp.zeros((b, h, dk, dv), jnp.float32)
    _, out = lax.scan(step, state0, (to_chunks(q), to_chunks(k), to_chunks(v), to_chunks(log_f)))
    return out.transpose(1, 2, 0, 3, 4).reshape(b, h, s_len, dv)


def hierarchical_moe(h, w_rg, b_rg, w_re, b_re, w_gate, w_up, w_down):
    f32 = jnp.float32
    hf = h.astype(f32)
    group_probs = jax.nn.softmax(hf @ w_rg.astype(f32) + b_rg.astype(f32), axis=-1)
    group_p, group_idx = lax.top_k(group_probs, 1)
    expert_logits = (hf @ w_re.astype(f32) + b_re.astype(f32)).reshape(-1, N_GROUPS, EXPERTS_PER_GROUP)
    group_onehot = jax.nn.one_hot(group_idx[:, 0], N_GROUPS, dtype=f32)
    in_group = jnp.einsum('tg,tge->te', group_onehot, expert_logits)
    top_logits, top_local = lax.top_k(in_group, TOP_K_IN_GROUP)
    weights = jax.nn.softmax(top_logits, axis=-1) * group_p
    expert_ids = group_idx * EXPERTS_PER_GROUP + top_local
    gates = jnp.einsum('tk,tke->te', weights, jax.nn.one_hot(expert_ids, N_EXPERTS, dtype=f32))
    y = jnp.zeros(h.shape, f32)
    for e in range(N_EXPERTS):
        act = jax.nn.silu(h @ w_gate[e]) * (h @ w_up[e])
        y = y + gates[:, e:e + 1] * (act @ w_down[e]).astype(f32)
    return y.astype(h.dtype)


def setup_inputs(seed: int = 0) -> dict:
    key = jax.random.key(seed)
    ks = jax.random.split(key, 21)
    f32 = jnp.float32

    def nrm(k, shape, s):
        return jax.random.normal(k, shape, f32) * s

    return {
        "x": nrm(ks[0], (BATCH, SEQ, D_MODEL), 1.0),
        "c": nrm(ks[1], (BATCH, D_MODEL), 1.0),
        "w_ada": nrm(ks[2], (DEPTH, D_MODEL, 6 * D_MODEL), D_MODEL ** -0.5),
        "b_ada": nrm(ks[3], (DEPTH, 6 * D_MODEL), 0.02),
        "norm1_g": 1.0 + nrm(ks[4], (DEPTH, D_MODEL), 0.02),
        "w_in": nrm(ks[5], (DEPTH, D_MODEL, IN_PROJ_WIDTH), D_MODEL ** -0.5),
        "b_fox": 1.0 + nrm(ks[6], (DEPTH, ATTN_HEADS), 0.1),
        "q_norm_g": 1.0 + nrm(ks[7], (DEPTH, ATTN_HEAD_DIM), 0.02),
        "k_norm_g": 1.0 + nrm(ks[8], (DEPTH, ATTN_HEAD_DIM), 0.02),
        "attn_out_g": 1.0 + nrm(ks[9], (DEPTH, ATTN_WIDTH), 0.02),
        "hgrn_lb": nrm(ks[10], (DEPTH + 1, HGRN_WIDTH), 1.0),
        "hgrn_out_g": 1.0 + nrm(ks[11], (DEPTH, HGRN_WIDTH), 0.02),
        "w_out": nrm(ks[12], (DEPTH, D_MODEL, D_MODEL), D_MODEL ** -0.5),
        "norm2_g": 1.0 + nrm(ks[13], (DEPTH, D_MODEL), 0.02),
        "w_router_group": nrm(ks[14], (DEPTH, D_MODEL, N_GROUPS), D_MODEL ** -0.5),
        "b_router_group": nrm(ks[15], (DEPTH, N_GROUPS), 0.01),
        "w_router_expert": nrm(ks[16], (DEPTH, D_MODEL, N_EXPERTS), D_MODEL ** -0.5),
        "b_router_expert": nrm(ks[17], (DEPTH, N_EXPERTS), 0.01),
        "w_gate": nrm(ks[18], (DEPTH, N_EXPERTS, D_MODEL, D_EXPERT), D_MODEL ** -0.5),
        "w_up": nrm(ks[19], (DEPTH, N_EXPERTS, D_MODEL, D_EXPERT), D_MODEL ** -0.5),
        "w_down": nrm(ks[20], (DEPTH, N_EXPERTS, D_EXPERT, D_MODEL), D_EXPERT ** -0.5),
    }


def reference(x, c, w_ada, b_ada, norm1_g, w_in, b_fox, q_norm_g, k_norm_g, attn_out_g,
              hgrn_lb, hgrn_out_g, w_out, norm2_g, w_router_group, b_router_group,
              w_router_expert, b_router_expert, w_gate, w_up, w_down):
    b, s, d = x.shape
    f32 = jnp.float32
    split_at = np.cumsum(IN_SPLITS)[:-1].tolist()
    lower_bounds = jnp.cumsum(jax.nn.softmax(hgrn_lb.astype(f32), axis=0), axis=0)
    for l in range(DEPTH):
        mod = jax.nn.silu(c) @ w_ada[l] + b_ada[l]
        shift1, scale1, gate1, shift2, scale2, gate2 = jnp.split(mod[:, None, :], 6, axis=-1)

        h = rms_norm(x, norm1_g[l]) * (1 + scale1) + shift1
        proj = h @ w_in[l]
        aq, ak, av, af, hq, hf, hi, hg = jnp.split(proj, split_at, axis=-1)

        aq = rms_norm(split_heads(aq, ATTN_HEADS), q_norm_g[l])
        ak = rms_norm(split_heads(ak, ATTN_HEADS), k_norm_g[l])
        av = split_heads(av, ATTN_HEADS)
        fox_log_f = jax.nn.log_sigmoid(af.astype(f32) + b_fox[l].astype(f32)).transpose(0, 2, 1)
        ao = forgetting_attention(aq, ak, av, fox_log_f)
        ao = merge_heads(rms_norm(ao, attn_out_g[l].reshape(ATTN_HEADS, 1, ATTN_HEAD_DIM)))

        lb = lower_bounds[l].reshape(HGRN_HEADS, 1, HGRN_EXPAND)
        forget = lb + (1 - lb) * jax.nn.sigmoid(split_heads(hf, HGRN_HEADS).astype(f32))
        ho = hgrn2_chunkwise(jax.nn.silu(split_heads(hq, HGRN_HEADS).astype(f32)), 1 - forget,
                             split_heads(hi, HGRN_HEADS).astype(f32), jnp.log(forget))
        ho = rms_norm(ho, hgrn_out_g[l].reshape(HGRN_HEADS, 1, HGRN_HEAD_V)) * jax.nn.silu(split_heads(hg, HGRN_HEADS).astype(f32))
        ho = merge_heads(ho).astype(x.dtype)

        x = x + gate1 * (jnp.concatenate([ao, ho], axis=-1) @ w_out[l])

        h2 = rms_norm(x, norm2_g[l]) * (1 + scale2) + shift2
        y = hierarchical_moe(h2.reshape(b * s, d), w_router_group[l], b_router_group[l],
                             w_router_expert[l], b_router_expert[l],
                             w_gate[l], w_up[l], w_down[l]).reshape(b, s, d)
        x = x + gate2 * y
    return x
```

```python
import functools

import numpy as np
import jax
import jax.numpy as jnp
from jax import lax
from jax.experimental import pallas as pl
from jax.experimental.pallas import tpu as pltpu

F32 = jnp.float32
BF16 = jnp.bfloat16

D_MODEL = 1024
ATTN_HEAD_DIM = 64
ATTN_WIDTH = 512
ATTN_HEADS = 8
HGRN_WIDTH = 512
HGRN_HEADS = 4
HGRN_DK = 128
N_GROUPS = 4
EXPERTS_PER_GROUP = 4
N_EXPERTS = 16
D_EXPERT = 512
NORM_EPS = 1e-6
LANES = 128
SUBLANES = 8
VMEM_LIMIT = 56 * 1024 * 1024

HGRN_CHUNK = 128
HGRN_LEVELS = (8, 16, 32, 64)
ROUTER_OFF = N_GROUPS


def _sigmoid(x):
    return 1.0 / (1.0 + jnp.exp(-x))


def _silu(x):
    return x * _sigmoid(x)


def _split3(x):
    p1 = x.astype(BF16)
    r1 = x - p1.astype(F32)
    p2 = r1.astype(BF16)
    p3 = (r1 - p2.astype(F32)).astype(BF16)
    return p1, p2, p3


def _dot(a, b):
    return jnp.dot(a, b, preferred_element_type=F32)


def _dot_nt(a, b):
    return lax.dot_general(a, b, (((1,), (1,)), ((), ())), preferred_element_type=F32)


def _ada_kernel(c_ref, w_ref, b_ref, o_ref):
    c = c_ref[...]
    o_ref[...] = jnp.dot(_silu(c), w_ref[...], preferred_element_type=F32,
                         precision=lax.Precision.HIGHEST) + b_ref[...]


def _ada(c, w, b):
    bsz, d = c.shape
    n = w.shape[1]
    tn = 1024
    return pl.pallas_call(
        _ada_kernel,
        grid=(n // tn,),
        in_specs=[pl.BlockSpec((bsz, d), lambda j: (0, 0)),
                  pl.BlockSpec((d, tn), lambda j: (0, j)),
                  pl.BlockSpec((1, tn), lambda j: (0, j))],
        out_specs=pl.BlockSpec((bsz, tn), lambda j: (0, j)),
        out_shape=jax.ShapeDtypeStruct((bsz, n), F32),
        compiler_params=pltpu.CompilerParams(dimension_semantics=("arbitrary",),
                                             vmem_limit_bytes=VMEM_LIMIT),
        name="ada",
    )(c, w, b.reshape(1, n))


def _inproj_kernel(x_ref, mod_ref, g1_ref, wq_ref, wk_ref, wv_ref, wf_ref, wh_ref,
                   gq_ref, gk_ref, bf_ref, gsum_ref, tri_ref,
                   q_out, k_out, v_out, cum_out, hq_out, hf_out, hi_out, hg_out,
                   carry_ref):
    si = pl.program_id(1)

    @pl.when(si == 0)
    def _():
        carry_ref[...] = jnp.zeros_like(carry_ref)

    x = x_ref[0]
    shift = mod_ref[0, 0:1, :]
    scale = mod_ref[0, 1:2, :]
    ms = jnp.mean(x * x, axis=-1, keepdims=True)
    h = (x * lax.rsqrt(ms + NORM_EPS) * g1_ref[...]) * (1.0 + scale) + shift
    hb = h.astype(BF16)

    def qk_norm(w_ref, g_ref, mult):
        t = _dot(hb, w_ref[...])
        ssq = _dot((t * t).astype(BF16), gsum_ref[...])
        return t * lax.rsqrt(ssq * (1.0 / ATTN_HEAD_DIM) + NORM_EPS) * (g_ref[...] * mult)

    q_out[0] = qk_norm(wq_ref, gq_ref, ATTN_HEAD_DIM ** -0.5).astype(BF16)
    k_out[0] = qk_norm(wk_ref, gk_ref, 1.0).astype(BF16)
    v_out[0] = _dot(hb, wv_ref[...]).astype(BF16)

    af = _dot(hb, wf_ref[...]) + bf_ref[...]
    lf = jnp.minimum(af, 0.0) - jnp.log(1.0 + jnp.exp(-jnp.abs(af)))
    tri = tri_ref[...]
    p1, p2, p3 = _split3(lf)
    cum = (_dot(tri, p1) + _dot(tri, p2)) + _dot(tri, p3) + carry_ref[...]
    tm = cum.shape[0]
    carry_ref[...] = cum[tm - 1:tm, :]
    cum_out[0] = cum.T[0:ATTN_HEADS, :]

    hq_out[0] = _dot(hb, wh_ref[:, 0 * HGRN_WIDTH:1 * HGRN_WIDTH]).astype(BF16)
    hf_out[0] = _dot(hb, wh_ref[:, 1 * HGRN_WIDTH:2 * HGRN_WIDTH]).astype(BF16)
    hi_out[0] = _dot(hb, wh_ref[:, 2 * HGRN_WIDTH:3 * HGRN_WIDTH]).astype(BF16)
    hg_out[0] = _dot(hb, wh_ref[:, 3 * HGRN_WIDTH:4 * HGRN_WIDTH]).astype(BF16)


def _inproj(x, mod, g1, wq, wk, wv, wf, wh, gq, gk, bfox, tm):
    b, s, d = x.shape
    gsum = jnp.asarray(np.kron(np.eye(ATTN_HEADS), np.ones((ATTN_HEAD_DIM, ATTN_HEAD_DIM))), BF16)
    tri = jnp.asarray(np.tril(np.ones((tm, tm))), BF16)
    const = lambda shape: pl.BlockSpec(shape, lambda bi, si: (0,) * len(shape))
    tok = lambda w: pl.BlockSpec((1, tm, w), lambda bi, si: (bi, si, 0))
    act = lambda w: jax.ShapeDtypeStruct((b, s, w), BF16)
    return pl.pallas_call(
        _inproj_kernel,
        grid=(b, s // tm),
        in_specs=[tok(d),
                  pl.BlockSpec((1, 6, d), lambda bi, si: (bi, 0, 0)),
                  const((1, d)),
                  const((d, ATTN_WIDTH)), const((d, ATTN_WIDTH)), const((d, ATTN_WIDTH)),
                  const((d, LANES)), const((d, 4 * HGRN_WIDTH)),
                  const((1, ATTN_WIDTH)), const((1, ATTN_WIDTH)), const((1, LANES)),
                  const((ATTN_WIDTH, ATTN_WIDTH)), const((tm, tm))],
        out_specs=[tok(ATTN_WIDTH), tok(ATTN_WIDTH), tok(ATTN_WIDTH),
                   pl.BlockSpec((1, ATTN_HEADS, tm), lambda bi, si: (bi, 0, si)),
                   tok(HGRN_WIDTH), tok(HGRN_WIDTH), tok(HGRN_WIDTH), tok(HGRN_WIDTH)],
        out_shape=[act(ATTN_WIDTH), act(ATTN_WIDTH), act(ATTN_WIDTH),
                   jax.ShapeDtypeStruct((b, ATTN_HEADS, s), F32),
                   act(HGRN_WIDTH), act(HGRN_WIDTH), act(HGRN_WIDTH), act(HGRN_WIDTH)],
        scratch_shapes=[pltpu.VMEM((1, LANES), F32)],
        compiler_params=pltpu.CompilerParams(dimension_semantics=("arbitrary", "arbitrary"),
                                             vmem_limit_bytes=VMEM_LIMIT),
        name="inproj",
    )(x, mod, g1, wq, wk, wv, wf, wh, gq, gk, bfox, gsum, tri)


def _fox_kernel(q_ref, k_ref, v_ref, cum_ref, g_ref, o_ref, *, tq):
    qi = pl.program_id(2)
    q = q_ref[0]
    lane = lax.broadcasted_iota(jnp.int32, (tq, LANES), 1)
    first = lane < ATTN_HEAD_DIM
    zero = jnp.zeros_like(q)
    qh = (jnp.where(first, q, zero), jnp.where(first, zero, q))

    def block(j, carry, diagonal):
        kb = k_ref[0, pl.ds(j * tq, tq), :]
        vb = v_ref[0, pl.ds(j * tq, tq), :]
        out = []
        for hd in range(2):
            m, l, acc = carry[hd]
            ck = cum_ref[0, 0, hd:hd + 1, pl.ds(j * tq, tq)]
            s = _dot_nt(qh[hd], kb) - ck
            if diagonal:
                r = lax.broadcasted_iota(jnp.int32, (tq, tq), 0)
                c = lax.broadcasted_iota(jnp.int32, (tq, tq), 1)
                s = jnp.where(r >= c, s, -jnp.inf)
            m_new = jnp.maximum(m, jnp.max(s, axis=-1, keepdims=True))
            alpha = jnp.exp(m - m_new)
            p = jnp.exp(s - m_new)
            l = alpha * l + jnp.sum(p, axis=-1, keepdims=True)
            acc = alpha * acc + _dot(p.astype(BF16), vb)
            out.append((m_new, l, acc))
        return tuple(out)

    init = tuple((jnp.full((tq, 1), -1e30, F32), jnp.zeros((tq, 1), F32),
                  jnp.zeros((tq, LANES), F32)) for _ in range(2))
    carry = lax.fori_loop(0, qi, lambda j, c: block(j, c, False), init)
    carry = block(qi, carry, True)
    (_, l0, a0), (_, l1, a1) = carry
    o = jnp.where(first, a0 / l0, a1 / l1)
    osq = o * o
    ss0 = jnp.sum(jnp.where(first, osq, 0.0), axis=-1, keepdims=True)
    ss1 = jnp.sum(jnp.where(first, 0.0, osq), axis=-1, keepdims=True)
    ms = jnp.where(first, ss0, ss1) * (1.0 / ATTN_HEAD_DIM)
    o_ref[0] = (o * lax.rsqrt(ms + NORM_EPS) * g_ref[...]).astype(BF16)


def _fox(q, k, v, cum, g_out, tq):
    b, s, _ = q.shape
    pairs = ATTN_HEADS // 2
    cum4 = cum.reshape(b, pairs, 2, s)
    return pl.pallas_call(
        functools.partial(_fox_kernel, tq=tq),
        grid=(b, pairs, s // tq),
        in_specs=[pl.BlockSpec((1, tq, LANES), lambda bi, p, qi: (bi, qi, p)),
                  pl.BlockSpec((1, s, LANES), lambda bi, p, qi: (bi, 0, p)),
                  pl.BlockSpec((1, s, LANES), lambda bi, p, qi: (bi, 0, p)),
                  pl.BlockSpec((1, 1, 2, s), lambda bi, p, qi: (bi, p, 0, 0)),
                  pl.BlockSpec((1, LANES), lambda bi, p, qi: (0, p))],
        out_specs=pl.BlockSpec((1, tq, LANES), lambda bi, p, qi: (bi, qi, p)),
        out_shape=jax.ShapeDtypeStruct((b, s, ATTN_WIDTH), BF16),
        compiler_params=pltpu.CompilerParams(
            dimension_semantics=("arbitrary", "arbitrary", "arbitrary"),
            vmem_limit_bytes=VMEM_LIMIT),
        name="fox",
    )(q, k, v, cum4, g_out)


def _hgrn_decay_matrix(c):
    t = np.arange(c)[:, None]
    j = np.arange(c)[None, :]
    blocks = [(j <= t), (j > t)]
    for m in HGRN_LEVELS:
        mid = (t // (2 * m)) * (2 * m) + m
        right = (t % (2 * m)) >= m
        blocks.append(np.where(right, (j >= mid) & (j <= t), (j > t) & (j < mid)))
    return np.concatenate(blocks, axis=0).astype(np.float32)


def _hgrn_kernel(hq_ref, hf_ref, hi_ref, hg_ref, lb_ref, g_ref, w_ref, o_ref, st_ref):
    ci = pl.program_id(1)
    c = HGRN_CHUNK

    @pl.when(ci == 0)
    def _():
        st_ref[...] = jnp.zeros_like(st_ref)

    r0 = lb_ref[0:1, :]
    r1 = lb_ref[1:2, :]
    rmax = jnp.maximum(r0, r1)
    e0 = jnp.exp(r0 - rmax)
    lb = e0 / (e0 + jnp.exp(r1 - rmax))

    f = lb + (1.0 - lb) * _sigmoid(hf_ref[0].astype(F32))
    g = jnp.log(f)
    g1, g2, g3 = _split3(g)
    w = w_ref[...]
    xall = (_dot(w, g1) + _dot(w, g2)) + _dot(w, g3)
    eall = jnp.exp(xall)
    q_all = _silu(hq_ref[0].astype(F32))
    k_all = 1.0 - f
    v_all = hi_ref[0].astype(F32)

    row = lax.broadcasted_iota(jnp.int32, (c, HGRN_DK), 0)
    rowmod = row % SUBLANES
    rr = lax.broadcasted_iota(jnp.int32, (c, c), 0)
    cc = lax.broadcasted_iota(jnp.int32, (c, c), 1)

    for hd in range(HGRN_HEADS):
        sl = slice(hd * HGRN_DK, (hd + 1) * HGRN_DK)
        q, k, v, fh = q_all[:, sl], k_all[:, sl], v_all[:, sl], f[:, sl]
        e_pre = eall[0:c, sl]
        e_suf = eall[c:2 * c, sl]

        a = jnp.zeros((c, c), F32)
        for li, m in enumerate(HGRN_LEVELS):
            e = eall[(2 + li) * c:(3 + li) * c, sl]
            right = (row % (2 * m)) >= m
            qt = jnp.where(right, q * e, 0.0).astype(BF16)
            kt = jnp.where(right, 0.0, k * e).astype(BF16)
            same = (rr // (2 * m)) == (cc // (2 * m))
            a = a + jnp.where(same, _dot_nt(qt, kt), 0.0)
        vb = v.astype(BF16)
        out = _dot(a.astype(BF16), vb)

        out = out + jnp.sum(q * k, axis=-1, keepdims=True) * v
        prod = fh
        for dd in range(1, SUBLANES):
            kd = pltpu.roll(k, dd, 0)
            vd = pltpu.roll(v, dd, 0)
            coef = jnp.sum(q * prod * kd, axis=-1, keepdims=True)
            out = out + jnp.where(rowmod >= dd, coef, 0.0) * vd
            if dd + 1 < SUBLANES:
                prod = prod * pltpu.roll(fh, dd, 0)

        st = st_ref[hd]
        out = out + _dot_nt((q * e_pre).astype(BF16), st.astype(BF16))
        kdec = (k * e_suf).astype(BF16)
        upd = lax.dot_general(vb, kdec, (((0,), (0,)), ((), ())), preferred_element_type=F32)
        st_ref[hd] = st * e_pre[c - 1:c, :] + upd

        ms = jnp.mean(out * out, axis=-1, keepdims=True)
        gate = _silu(hg_ref[0, :, sl].astype(F32))
        o_ref[0, :, sl] = (out * lax.rsqrt(ms + NORM_EPS) * g_ref[:, sl] * gate).astype(BF16)


def _hgrn(hq, hf, hi, hg, lb_rows, g_out):
    b, s, wd = hq.shape
    c = HGRN_CHUNK
    wmat = jnp.asarray(_hgrn_decay_matrix(c), BF16)
    tok = pl.BlockSpec((1, c, wd), lambda bi, ci: (bi, ci, 0))
    return pl.pallas_call(
        _hgrn_kernel,
        grid=(b, s // c),
        in_specs=[tok, tok, tok, tok,
                  pl.BlockSpec((2, wd), lambda bi, ci: (0, 0)),
                  pl.BlockSpec((1, wd), lambda bi, ci: (0, 0)),
                  pl.BlockSpec(wmat.shape, lambda bi, ci: (0, 0))],
        out_specs=tok,
        out_shape=jax.ShapeDtypeStruct((b, s, wd), BF16),
        scratch_shapes=[pltpu.VMEM((HGRN_HEADS, HGRN_DK, HGRN_DK), F32)],
        compiler_params=pltpu.CompilerParams(dimension_semantics=("arbitrary", "arbitrary"),
                                             vmem_limit_bytes=VMEM_LIMIT),
        name="hgrn",
    )(hq, hf, hi, hg, lb_rows, g_out, wmat)


def _outproj_kernel(x_ref, ao_ref, ho_ref, mod_ref, wo_ref, g2_ref, wr_ref, br_ref,
                    x1_out, h2_out, gates_out):
    gate1 = mod_ref[0, 2:3, :]
    shift2 = mod_ref[0, 3:4, :]
    scale2 = mod_ref[0, 4:5, :]
    mix = _dot(ao_ref[0], wo_ref[0:ATTN_WIDTH, :]) + _dot(ho_ref[0], wo_ref[ATTN_WIDTH:D_MODEL, :])
    x1 = x_ref[0] + gate1 * mix
    x1_out[0] = x1
    ms = jnp.mean(x1 * x1, axis=-1, keepdims=True)
    h2 = (x1 * lax.rsqrt(ms + NORM_EPS) * g2_ref[...]) * (1.0 + scale2) + shift2
    h2_out[0] = h2.astype(BF16)

    logits = jnp.dot(h2, wr_ref[...], preferred_element_type=F32,
                     precision=lax.Precision.HIGHEST) + br_ref[...]
    tm = logits.shape[0]
    lane = lax.broadcasted_iota(jnp.int32, (tm, LANES), 1)
    neg = -jnp.inf
    is_group = lane < N_GROUPS
    gl = jnp.where(is_group, logits, neg)
    gmax = jnp.max(gl, axis=-1, keepdims=True)
    gsum = jnp.sum(jnp.exp(gl - gmax), axis=-1, keepdims=True)
    group_p = 1.0 / gsum
    gidx = jnp.min(jnp.where(is_group & (gl == gmax), lane, LANES), axis=-1, keepdims=True)
    in_group = (lane >= ROUTER_OFF) & (lane < ROUTER_OFF + N_EXPERTS) & \
        (((lane - ROUTER_OFF) // EXPERTS_PER_GROUP) == gidx)
    el = jnp.where(in_group, logits, neg)
    top1 = jnp.max(el, axis=-1, keepdims=True)
    idx1 = jnp.min(jnp.where(in_group & (el == top1), lane, LANES), axis=-1, keepdims=True)
    el2 = jnp.where(lane == idx1, neg, el)
    top2 = jnp.max(el2, axis=-1, keepdims=True)
    idx2 = jnp.min(jnp.where(in_group & (lane != idx1) & (el2 == top2), lane, LANES),
                   axis=-1, keepdims=True)
    e2 = jnp.exp(top2 - top1)
    w1 = group_p / (1.0 + e2)
    w2 = group_p * e2 / (1.0 + e2)
    gates_out[0] = jnp.where(lane == idx1, w1, jnp.where(lane == idx2, w2, 0.0))


def _outproj(x, ao, ho, mod, wo, g2, wr, br, tm):
    b, s, d = x.shape
    const = lambda shape: pl.BlockSpec(shape, lambda bi, si: (0,) * len(shape))
    tok = lambda w: pl.BlockSpec((1, tm, w), lambda bi, si: (bi, si, 0))
    return pl.pallas_call(
        _outproj_kernel,
        grid=(b, s // tm),
        in_specs=[tok(d), tok(ATTN_WIDTH), tok(HGRN_WIDTH),
                  pl.BlockSpec((1, 6, d), lambda bi, si: (bi, 0, 0)),
                  const((d, d)), const((1, d)), const((d, LANES)), const((1, LANES))],
        out_specs=[tok(d), tok(d), tok(LANES)],
        out_shape=[jax.ShapeDtypeStruct((b, s, d), F32),
                   jax.ShapeDtypeStruct((b, s, d), BF16),
                   jax.ShapeDtypeStruct((b, s, LANES), F32)],
        compiler_params=pltpu.CompilerParams(dimension_semantics=("arbitrary", "arbitrary"),
                                             vmem_limit_bytes=VMEM_LIMIT),
        name="outproj",
    )(x, ao, ho, mod, wo, g2, wr, br)


def _moe_kernel(x1_ref, h2_ref, gates_ref, mod_ref, wg_ref, wu_ref, wd_ref, o_ref, acc_ref):
    e = pl.program_id(2)

    @pl.when(e == 0)
    def _():
        acc_ref[...] = jnp.zeros_like(acc_ref)

    h2 = h2_ref[0]
    gates = gates_ref[0]
    lane = lax.broadcasted_iota(jnp.int32, gates.shape, 1)
    gcol = jnp.sum(jnp.where(lane == e + ROUTER_OFF, gates, 0.0), axis=-1, keepdims=True)
    act = _silu(_dot(h2, wg_ref[0])) * _dot(h2, wu_ref[0])
    acc_ref[...] += gcol * _dot(act.astype(BF16), wd_ref[0])

    @pl.when(e == N_EXPERTS - 1)
    def _():
        o_ref[0] = x1_ref[0] + mod_ref[0, 5:6, :] * acc_ref[...]


def _moe(x1, h2, gates, mod, wg, wu, wd, tm):
    b, s, d = x1.shape
    tok = lambda w: pl.BlockSpec((1, tm, w), lambda bi, si, e: (bi, si, 0))
    return pl.pallas_call(
        _moe_kernel,
        grid=(b, s // tm, N_EXPERTS),
        in_specs=[tok(d), tok(d), tok(LANES),
                  pl.BlockSpec((1, 6, d), lambda bi, si, e: (bi, 0, 0)),
                  pl.BlockSpec((1, d, D_EXPERT), lambda bi, si, e: (e, 0, 0)),
                  pl.BlockSpec((1, d, D_EXPERT), lambda bi, si, e: (e, 0, 0)),
                  pl.BlockSpec((1, D_EXPERT, d), lambda bi, si, e: (e, 0, 0))],
        out_specs=tok(d),
        out_shape=jax.ShapeDtypeStruct((b, s, d), F32),
        scratch_shapes=[pltpu.VMEM((tm, d), F32)],
        compiler_params=pltpu.CompilerParams(
            dimension_semantics=("arbitrary", "arbitrary", "arbitrary"),
            vmem_limit_bytes=VMEM_LIMIT),
        name="moe",
    )(x1, h2, gates, mod, wg, wu, wd)


def kernel(x, c, w_ada, b_ada, norm1_g, w_in, b_fox, q_norm_g, k_norm_g, attn_out_g, hgrn_lb,
           hgrn_out_g, w_out, norm2_g, w_router_group, b_router_group, w_router_expert,
           b_router_expert, w_gate, w_up, w_down):
    b, s, d = x.shape
    l = 0
    aw = ATTN_WIDTH
    mod = _ada(c, w_ada[l], b_ada[l]).reshape(b, 6, d)

    w = w_in[l]
    wq = w[:, 0:aw].astype(BF16)
    wk = w[:, aw:2 * aw].astype(BF16)
    wv = w[:, 2 * aw:3 * aw].astype(BF16)
    f0 = 3 * aw
    wf = jnp.pad(w[:, f0:f0 + ATTN_HEADS], ((0, 0), (0, LANES - ATTN_HEADS))).astype(BF16)
    wh = w[:, f0 + ATTN_HEADS:].astype(BF16)
    bfox = jnp.pad(b_fox[l], (0, LANES - ATTN_HEADS)).reshape(1, LANES)
    gq = jnp.tile(q_norm_g[l], ATTN_HEADS).reshape(1, aw)
    gk = jnp.tile(k_norm_g[l], ATTN_HEADS).reshape(1, aw)

    tm = min(512, s)
    q, k, v, cum, hq, hf, hi, hg = _inproj(x, mod, norm1_g[l].reshape(1, d), wq, wk, wv, wf, wh,
                                           gq, gk, bfox, tm)
    ao = _fox(q, k, v, cum, attn_out_g[l].reshape(1, aw), min(256, s))
    ho = _hgrn(hq, hf, hi, hg, hgrn_lb[0:2], hgrn_out_g[l].reshape(1, HGRN_WIDTH))

    wr = jnp.pad(jnp.concatenate([w_router_group[l], w_router_expert[l]], axis=1),
                 ((0, 0), (0, LANES - N_GROUPS - N_EXPERTS)))
    br = jnp.pad(jnp.concatenate([b_router_group[l], b_router_expert[l]]),
                 (0, LANES - N_GROUPS - N_EXPERTS)).reshape(1, LANES)
    x1, h2, gates = _outproj(x, ao, ho, mod, w_out[l].astype(BF16), norm2_g[l].reshape(1, d),
                             wr, br, tm)
    return _moe(x1, h2, gates, mod, w_gate[l].astype(BF16), w_up[l].astype(BF16),
                w_down[l].astype(BF16), min(1024, s))
```

```python
import functools

import numpy as np
import jax
import jax.numpy as jnp
from jax import lax
from jax.experimental import pallas as pl
from jax.experimental.pallas import tpu as pltpu

F32 = jnp.float32
BF16 = jnp.bfloat16

D_MODEL = 1024
ATTN_HEAD_DIM = 64
ATTN_WIDTH = 512
ATTN_HEADS = 8
HGRN_WIDTH = 512
HGRN_HEADS = 4
HGRN_DK = 128
N_GROUPS = 4
EXPERTS_PER_GROUP = 4
N_EXPERTS = 16
D_EXPERT = 512
NORM_EPS = 1e-6
LANES = 128
SUBLANES = 8
VMEM_LIMIT = 56 * 1024 * 1024

HGRN_CHUNK = 128
HGRN_LEVELS = (8, 16, 32, 64)
ROUTER_OFF = N_GROUPS
LOG2E = 1.4426950408889634
FOX_GROUP = 4
FOX_PIECES = 3


def _sigmoid(x):
    return 1.0 / (1.0 + jnp.exp(-x))


def _silu(x):
    return x * _sigmoid(x)


def _split3(x):
    p1 = x.astype(BF16)
    r1 = x - p1.astype(F32)
    p2 = r1.astype(BF16)
    p3 = (r1 - p2.astype(F32)).astype(BF16)
    return p1, p2, p3


def _dot(a, b):
    return jnp.dot(a, b, preferred_element_type=F32)


def _dot_nt(a, b):
    return lax.dot_general(a, b, (((1,), (1,)), ((), ())), preferred_element_type=F32)


def _ada_kernel(c_ref, w_ref, b_ref, o_ref):
    c = c_ref[...]
    o_ref[...] = jnp.dot(_silu(c), w_ref[...], preferred_element_type=F32,
                         precision=lax.Precision.HIGHEST) + b_ref[...]


def _ada(c, w, b):
    bsz, d = c.shape
    n = w.shape[1]
    tn = 1024
    return pl.pallas_call(
        _ada_kernel,
        grid=(n // tn,),
        in_specs=[pl.BlockSpec((bsz, d), lambda j: (0, 0)),
                  pl.BlockSpec((d, tn), lambda j: (0, j)),
                  pl.BlockSpec((1, tn), lambda j: (0, j))],
        out_specs=pl.BlockSpec((bsz, tn), lambda j: (0, j)),
        out_shape=jax.ShapeDtypeStruct((bsz, n), F32),
        compiler_params=pltpu.CompilerParams(dimension_semantics=("arbitrary",),
                                             vmem_limit_bytes=VMEM_LIMIT),
        name="ada",
    )(c, w, b.reshape(1, n))


def _inproj_kernel(x_ref, mod_ref, g1_ref, wq_ref, wk_ref, wv_ref, wf_ref, wh_ref,
                   gq_ref, gk_ref, bf_ref, gsum_ref, tri_ref, place_ref,
                   q_out, k_out, v_out, ck_out, hq_out, hf_out, hi_out, hg_out,
                   carry_ref):
    si = pl.program_id(1)

    @pl.when(si == 0)
    def _():
        carry_ref[...] = jnp.zeros_like(carry_ref)

    x = x_ref[0]
    shift = mod_ref[0, 0:1, :]
    scale = mod_ref[0, 1:2, :]
    ms = jnp.mean(x * x, axis=-1, keepdims=True)
    h = (x * lax.rsqrt(ms + NORM_EPS) * g1_ref[...]) * (1.0 + scale) + shift
    hb = h.astype(BF16)

    def qk_norm(w_ref, g_ref, mult):
        t = _dot(hb, w_ref[...])
        ssq = _dot((t * t).astype(BF16), gsum_ref[...])
        return t * lax.rsqrt(ssq * (1.0 / ATTN_HEAD_DIM) + NORM_EPS) * (g_ref[...] * mult)

    q_out[0] = qk_norm(wq_ref, gq_ref, ATTN_HEAD_DIM ** -0.5 * LOG2E).T.astype(BF16)
    k_out[0] = qk_norm(wk_ref, gk_ref, 1.0).astype(BF16)
    v_out[0] = _dot(hb, wv_ref[...]).T.astype(BF16)

    af = _dot(hb, wf_ref[...]) + bf_ref[...]
    lf = jnp.minimum(af, 0.0) - jnp.log(1.0 + jnp.exp(-jnp.abs(af)))
    tri = tri_ref[...]
    p1, p2, p3 = _split3(lf)
    cum = (_dot(tri, p1) + _dot(tri, p2)) + _dot(tri, p3) + carry_ref[...]
    tm = cum.shape[0]
    carry_ref[...] = cum[tm - 1:tm, :]
    c1, c2, c3 = _split3(cum * (-LOG2E))
    lane = lax.broadcasted_iota(jnp.int32, cum.shape, 1)
    zero = jnp.zeros_like(c1)
    pieces = jnp.where(lane < ATTN_HEADS, c1,
                       jnp.where(lane < 2 * ATTN_HEADS, c2,
                                 jnp.where(lane < 3 * ATTN_HEADS, c3, zero)))
    ck_out[0] = _dot(pieces, place_ref[...]).astype(BF16)

    hq_out[0] = _dot(hb, wh_ref[:, 0 * HGRN_WIDTH:1 * HGRN_WIDTH]).astype(BF16)
    hf_out[0] = _dot(hb, wh_ref[:, 1 * HGRN_WIDTH:2 * HGRN_WIDTH]).astype(BF16)
    hi_out[0] = _dot(hb, wh_ref[:, 2 * HGRN_WIDTH:3 * HGRN_WIDTH]).astype(BF16)
    hg_out[0] = _dot(hb, wh_ref[:, 3 * HGRN_WIDTH:4 * HGRN_WIDTH]).astype(BF16)


def _inproj(x, mod, g1, wq, wk, wv, wf, wh, gq, gk, bfox, tm):
    b, s, d = x.shape
    gsum = jnp.asarray(np.kron(np.eye(ATTN_HEADS), np.ones((ATTN_HEAD_DIM, ATTN_HEAD_DIM))), BF16)
    tri = jnp.asarray(np.tril(np.ones((tm, tm))), BF16)
    place_np = np.zeros((LANES, ATTN_HEADS * LANES), np.float32)
    for piece in range(FOX_PIECES):
        for hd in range(ATTN_HEADS):
            place_np[piece * ATTN_HEADS + hd, hd * LANES + piece] = 1.0
    place = jnp.asarray(place_np, BF16)
    const = lambda shape: pl.BlockSpec(shape, lambda bi, si: (0,) * len(shape))
    tok = lambda w: pl.BlockSpec((1, tm, w), lambda bi, si: (bi, si, 0))
    tok_t = lambda w: pl.BlockSpec((1, w, tm), lambda bi, si: (bi, 0, si))
    act = lambda w: jax.ShapeDtypeStruct((b, s, w), BF16)
    act_t = lambda w: jax.ShapeDtypeStruct((b, w, s), BF16)
    return pl.pallas_call(
        _inproj_kernel,
        grid=(b, s // tm),
        in_specs=[tok(d),
                  pl.BlockSpec((1, 6, d), lambda bi, si: (bi, 0, 0)),
                  const((1, d)),
                  const((d, ATTN_WIDTH)), const((d, ATTN_WIDTH)), const((d, ATTN_WIDTH)),
                  const((d, LANES)), const((d, 4 * HGRN_WIDTH)),
                  const((1, ATTN_WIDTH)), const((1, ATTN_WIDTH)), const((1, LANES)),
                  const((ATTN_WIDTH, ATTN_WIDTH)), const((tm, tm)),
                  const((LANES, ATTN_HEADS * LANES))],
        out_specs=[tok_t(ATTN_WIDTH), tok(ATTN_WIDTH), tok_t(ATTN_WIDTH),
                   tok(ATTN_HEADS * LANES),
                   tok(HGRN_WIDTH), tok(HGRN_WIDTH), tok(HGRN_WIDTH), tok(HGRN_WIDTH)],
        out_shape=[act_t(ATTN_WIDTH), act(ATTN_WIDTH), act_t(ATTN_WIDTH),
                   act(ATTN_HEADS * LANES),
                   act(HGRN_WIDTH), act(HGRN_WIDTH), act(HGRN_WIDTH), act(HGRN_WIDTH)],
        scratch_shapes=[pltpu.VMEM((1, LANES), F32)],
        compiler_params=pltpu.CompilerParams(dimension_semantics=("arbitrary", "arbitrary"),
                                             vmem_limit_bytes=VMEM_LIMIT),
        name="inproj",
    )(x, mod, g1, wq, wk, wv, wf, wh, gq, gk, bfox, gsum, tri, place)


def _fox_kernel(qt_ref, k_ref, vt_ref, ck_ref, g_ref, o_ref, *, tq, tk):
    qi = pl.program_id(2)
    qt = qt_ref[0].astype(F32)
    row = lax.broadcasted_iota(jnp.int32, (LANES, tq), 0)
    first = row < ATTN_HEAD_DIM
    ones = jnp.where(row < FOX_PIECES, 1.0, 0.0)
    rhs = []
    for hd in range(FOX_GROUP):
        qp = qt[(hd // 2) * LANES:(hd // 2 + 1) * LANES, :]
        qh = jnp.where(first, qp, 0.0) if hd % 2 == 0 else jnp.where(first, 0.0, qp)
        rhs.append(jnp.concatenate([qh, ones], axis=0).astype(BF16))

    def block(j, carry, masked):
        k0 = pl.multiple_of(j * tk, tk)
        kb = k_ref[0, pl.ds(k0, tk), :]
        ckb = ck_ref[0, pl.ds(k0, tk), :]
        vtb = vt_ref[0, :, pl.ds(k0, tk)]
        out = []
        for hd in range(FOX_GROUP):
            m, l, acc = carry[hd]
            pr = slice((hd // 2) * LANES, (hd // 2 + 1) * LANES)
            lhs = jnp.concatenate([kb[:, pr], ckb[:, hd * LANES:(hd + 1) * LANES]], axis=1)
            st = _dot(lhs, rhs[hd])
            if masked:
                key = k0 + lax.broadcasted_iota(jnp.int32, (tk, tq), 0)
                qry = qi * tq + lax.broadcasted_iota(jnp.int32, (tk, tq), 1)
                st = jnp.where(key <= qry, st, -jnp.inf)
            m_new = jnp.maximum(m, jnp.max(st, axis=0, keepdims=True))
            alpha = jnp.exp2(m - m_new)
            pt = jnp.exp2(st - m_new)
            l = alpha * l + jnp.sum(pt, axis=0, keepdims=True)
            acc = alpha * acc + _dot(vtb[pr, :], pt.astype(BF16))
            out.append((m_new, l, acc))
        return tuple(out)

    per_q = tq // tk

    def full_blocks(i, carry):
        for u in range(per_q):
            carry = block(i * per_q + u, carry, False)
        return carry

    init = tuple((jnp.full((1, tq), -1e30, F32), jnp.zeros((1, tq), F32),
                  jnp.zeros((LANES, tq), F32)) for _ in range(FOX_GROUP))
    carry = lax.fori_loop(0, qi, full_blocks, init)
    for u in range(per_q):
        carry = block(qi * per_q + u, carry, True)

    head0 = lax.broadcasted_iota(jnp.int32, (tq, LANES), 1) < ATTN_HEAD_DIM
    for pr in range(FOX_GROUP // 2):
        (_, l0, a0), (_, l1, a1) = carry[2 * pr], carry[2 * pr + 1]
        ot = jnp.where(first, a0 * (1.0 / l0), a1 * (1.0 / l1))
        o = ot.T
        osq = o * o
        ss0 = jnp.sum(jnp.where(head0, osq, 0.0), axis=-1, keepdims=True)
        ss1 = jnp.sum(jnp.where(head0, 0.0, osq), axis=-1, keepdims=True)
        ms = jnp.where(head0, ss0, ss1) * (1.0 / ATTN_HEAD_DIM)
        sl = slice(pr * LANES, (pr + 1) * LANES)
        o_ref[0, :, sl] = (o * lax.rsqrt(ms + NORM_EPS) * g_ref[:, sl]).astype(BF16)


def _fox(qt, k, vt, ck, g_out, tq, tk):
    b, s, _ = k.shape
    groups = ATTN_HEADS // FOX_GROUP
    gw = FOX_GROUP * ATTN_HEAD_DIM
    return pl.pallas_call(
        functools.partial(_fox_kernel, tq=tq, tk=tk),
        grid=(b, groups, s // tq),
        in_specs=[pl.BlockSpec((1, gw, tq), lambda bi, g, qi: (bi, g, qi)),
                  pl.BlockSpec((1, s, gw), lambda bi, g, qi: (bi, 0, g)),
                  pl.BlockSpec((1, gw, s), lambda bi, g, qi: (bi, g, 0)),
                  pl.BlockSpec((1, s, FOX_GROUP * LANES), lambda bi, g, qi: (bi, 0, g)),
                  pl.BlockSpec((1, gw), lambda bi, g, qi: (0, g))],
        out_specs=pl.BlockSpec((1, tq, gw), lambda bi, g, qi: (bi, qi, g)),
        out_shape=jax.ShapeDtypeStruct((b, s, ATTN_WIDTH), BF16),
        compiler_params=pltpu.CompilerParams(
            dimension_semantics=("arbitrary", "arbitrary", "arbitrary"),
            vmem_limit_bytes=VMEM_LIMIT),
        name="fox",
    )(qt, k, vt, ck, g_out)


def _hgrn_decay_matrix(c):
    t = np.arange(c)[:, None]
    j = np.arange(c)[None, :]
    blocks = [(j <= t), (j > t)]
    for m in HGRN_LEVELS:
        mid = (t // (2 * m)) * (2 * m) + m
        right = (t % (2 * m)) >= m
        blocks.append(np.where(right, (j >= mid) & (j <= t), (j > t) & (j < mid)))
    return np.concatenate(blocks, axis=0).astype(np.float32)


def _hgrn_kernel(hq_ref, hf_ref, hi_ref, hg_ref, lb_ref, g_ref, w_ref, o_ref, st_ref):
    ci = pl.program_id(1)
    c = HGRN_CHUNK

    @pl.when(ci == 0)
    def _():
        st_ref[...] = jnp.zeros_like(st_ref)

    r0 = lb_ref[0:1, :]
    r1 = lb_ref[1:2, :]
    rmax = jnp.maximum(r0, r1)
    e0 = jnp.exp(r0 - rmax)
    lb = e0 / (e0 + jnp.exp(r1 - rmax))

    f = lb + (1.0 - lb) * _sigmoid(hf_ref[0].astype(F32))
    g = jnp.log(f)
    g1, g2, g3 = _split3(g)
    w = w_ref[...]
    xall = (_dot(w, g1) + _dot(w, g2)) + _dot(w, g3)
    eall = jnp.exp(xall)
    q_all = _silu(hq_ref[0].astype(F32))
    k_all = 1.0 - f
    v_all = hi_ref[0].astype(F32)

    row = lax.broadcasted_iota(jnp.int32, (c, HGRN_DK), 0)
    rowmod = row % SUBLANES
    rr = lax.broadcasted_iota(jnp.int32, (c, c), 0)
    cc = lax.broadcasted_iota(jnp.int32, (c, c), 1)

    for hd in range(HGRN_HEADS):
        sl = slice(hd * HGRN_DK, (hd + 1) * HGRN_DK)
        q, k, v, fh = q_all[:, sl], k_all[:, sl], v_all[:, sl], f[:, sl]
        e_pre = eall[0:c, sl]
        e_suf = eall[c:2 * c, sl]

        a = jnp.zeros((c, c), F32)
        for li, m in enumerate(HGRN_LEVELS):
            e = eall[(2 + li) * c:(3 + li) * c, sl]
            right = (row % (2 * m)) >= m
            qt = jnp.where(right, q * e, 0.0).astype(BF16)
            kt = jnp.where(right, 0.0, k * e).astype(BF16)
            same = (rr // (2 * m)) == (cc // (2 * m))
            a = a + jnp.where(same, _dot_nt(qt, kt), 0.0)
        vb = v.astype(BF16)
        out = _dot(a.astype(BF16), vb)

        out = out + jnp.sum(q * k, axis=-1, keepdims=True) * v
        prod = fh
        for dd in range(1, SUBLANES):
            kd = pltpu.roll(k, dd, 0)
            vd = pltpu.roll(v, dd, 0)
            coef = jnp.sum(q * prod * kd, axis=-1, keepdims=True)
            out = out + jnp.where(rowmod >= dd, coef, 0.0) * vd
            if dd + 1 < SUBLANES:
                prod = prod * pltpu.roll(fh, dd, 0)

        st = st_ref[hd]
        out = out + _dot_nt((q * e_pre).astype(BF16), st.astype(BF16))
        kdec = (k * e_suf).astype(BF16)
        upd = lax.dot_general(vb, kdec, (((0,), (0,)), ((), ())), preferred_element_type=F32)
        st_ref[hd] = st * e_pre[c - 1:c, :] + upd

        ms = jnp.mean(out * out, axis=-1, keepdims=True)
        gate = _silu(hg_ref[0, :, sl].astype(F32))
        o_ref[0, :, sl] = (out * lax.rsqrt(ms + NORM_EPS) * g_ref[:, sl] * gate).astype(BF16)


def _hgrn(hq, hf, hi, hg, lb_rows, g_out):
    b, s, wd = hq.shape
    c = HGRN_CHUNK
    wmat = jnp.asarray(_hgrn_decay_matrix(c), BF16)
    tok = pl.BlockSpec((1, c, wd), lambda bi, ci: (bi, ci, 0))
    return pl.pallas_call(
        _hgrn_kernel,
        grid=(b, s // c),
        in_specs=[tok, tok, tok, tok,
                  pl.BlockSpec((2, wd), lambda bi, ci: (0, 0)),
                  pl.BlockSpec((1, wd), lambda bi, ci: (0, 0)),
                  pl.BlockSpec(wmat.shape, lambda bi, ci: (0, 0))],
        out_specs=tok,
        out_shape=jax.ShapeDtypeStruct((b, s, wd), BF16),
        scratch_shapes=[pltpu.VMEM((HGRN_HEADS, HGRN_DK, HGRN_DK), F32)],
        compiler_params=pltpu.CompilerParams(dimension_semantics=("arbitrary", "arbitrary"),
                                             vmem_limit_bytes=VMEM_LIMIT),
        name="hgrn",
    )(hq, hf, hi, hg, lb_rows, g_out, wmat)


def _outproj_kernel(x_ref, ao_ref, ho_ref, mod_ref, wo_ref, g2_ref, wr_ref, br_ref,
                    x1_out, h2_out, gates_out):
    gate1 = mod_ref[0, 2:3, :]
    shift2 = mod_ref[0, 3:4, :]
    scale2 = mod_ref[0, 4:5, :]
    mix = _dot(ao_ref[0], wo_ref[0:ATTN_WIDTH, :]) + _dot(ho_ref[0], wo_ref[ATTN_WIDTH:D_MODEL, :])
    x1 = x_ref[0] + gate1 * mix
    x1_out[0] = x1
    ms = jnp.mean(x1 * x1, axis=-1, keepdims=True)
    h2 = (x1 * lax.rsqrt(ms + NORM_EPS) * g2_ref[...]) * (1.0 + scale2) + shift2
    h2_out[0] = h2.astype(BF16)

    logits = jnp.dot(h2, wr_ref[...], preferred_element_type=F32,
                     precision=lax.Precision.HIGHEST) + br_ref[...]
    tm = logits.shape[0]
    lane = lax.broadcasted_iota(jnp.int32, (tm, LANES), 1)
    neg = -jnp.inf
    is_group = lane < N_GROUPS
    gl = jnp.where(is_group, logits, neg)
    gmax = jnp.max(gl, axis=-1, keepdims=True)
    gsum = jnp.sum(jnp.exp(gl - gmax), axis=-1, keepdims=True)
    group_p = 1.0 / gsum
    gidx = jnp.min(jnp.where(is_group & (gl == gmax), lane, LANES), axis=-1, keepdims=True)
    in_group = (lane >= ROUTER_OFF) & (lane < ROUTER_OFF + N_EXPERTS) & \
        (((lane - ROUTER_OFF) // EXPERTS_PER_GROUP) == gidx)
    el = jnp.where(in_group, logits, neg)
    top1 = jnp.max(el, axis=-1, keepdims=True)
    idx1 = jnp.min(jnp.where(in_group & (el == top1), lane, LANES), axis=-1, keepdims=True)
    el2 = jnp.where(lane == idx1, neg, el)
    top2 = jnp.max(el2, axis=-1, keepdims=True)
    idx2 = jnp.min(jnp.where(in_group & (lane != idx1) & (el2 == top2), lane, LANES),
                   axis=-1, keepdims=True)
    e2 = jnp.exp(top2 - top1)
    w1 = group_p / (1.0 + e2)
    w2 = group_p * e2 / (1.0 + e2)
    gates_out[0] = jnp.where(lane == idx1, w1, jnp.where(lane == idx2, w2, 0.0))


def _outproj(x, ao, ho, mod, wo, g2, wr, br, tm):
    b, s, d = x.shape
    const = lambda shape: pl.BlockSpec(shape, lambda bi, si: (0,) * len(shape))
    tok = lambda w: pl.BlockSpec((1, tm, w), lambda bi, si: (bi, si, 0))
    return pl.pallas_call(
        _outproj_kernel,
        grid=(b, s // tm),
        in_specs=[tok(d), tok(ATTN_WIDTH), tok(HGRN_WIDTH),
                  pl.BlockSpec((1, 6, d), lambda bi, si: (bi, 0, 0)),
                  const((d, d)), const((1, d)), const((d, LANES)), const((1, LANES))],
        out_specs=[tok(d), tok(d), tok(LANES)],
        out_shape=[jax.ShapeDtypeStruct((b, s, d), F32),
                   jax.ShapeDtypeStruct((b, s, d), BF16),
                   jax.ShapeDtypeStruct((b, s, LANES), F32)],
        compiler_params=pltpu.CompilerParams(dimension_semantics=("arbitrary", "arbitrary"),
                                             vmem_limit_bytes=VMEM_LIMIT),
        name="outproj",
    )(x, ao, ho, mod, wo, g2, wr, br)


def _moe_kernel(x1_ref, h2_ref, gates_ref, mod_ref, wg_ref, wu_ref, wd_ref, o_ref, acc_ref):
    e = pl.program_id(2)

    @pl.when(e == 0)
    def _():
        acc_ref[...] = jnp.zeros_like(acc_ref)

    h2 = h2_ref[0]
    gates = gates_ref[0]
    lane = lax.broadcasted_iota(jnp.int32, gates.shape, 1)
    gcol = jnp.sum(jnp.where(lane == e + ROUTER_OFF, gates, 0.0), axis=-1, keepdims=True)
    act = _silu(_dot(h2, wg_ref[0])) * _dot(h2, wu_ref[0])
    acc_ref[...] += gcol * _dot(act.astype(BF16), wd_ref[0])

    @pl.when(e == N_EXPERTS - 1)
    def _():
        o_ref[0] = x1_ref[0] + mod_ref[0, 5:6, :] * acc_ref[...]


def _moe(x1, h2, gates, mod, wg, wu, wd, tm):
    b, s, d = x1.shape
    tok = lambda w: pl.BlockSpec((1, tm, w), lambda bi, si, e: (bi, si, 0))
    return pl.pallas_call(
        _moe_kernel,
        grid=(b, s // tm, N_EXPERTS),
        in_specs=[tok(d), tok(d), tok(LANES),
                  pl.BlockSpec((1, 6, d), lambda bi, si, e: (bi, 0, 0)),
                  pl.BlockSpec((1, d, D_EXPERT), lambda bi, si, e: (e, 0, 0)),
                  pl.BlockSpec((1, d, D_EXPERT), lambda bi, si, e: (e, 0, 0)),
                  pl.BlockSpec((1, D_EXPERT, d), lambda bi, si, e: (e, 0, 0))],
        out_specs=tok(d),
        out_shape=jax.ShapeDtypeStruct((b, s, d), F32),
        scratch_shapes=[pltpu.VMEM((tm, d), F32)],
        compiler_params=pltpu.CompilerParams(
            dimension_semantics=("arbitrary", "arbitrary", "arbitrary"),
            vmem_limit_bytes=VMEM_LIMIT),
        name="moe",
    )(x1, h2, gates, mod, wg, wu, wd)


def kernel(x, c, w_ada, b_ada, norm1_g, w_in, b_fox, q_norm_g, k_norm_g, attn_out_g, hgrn_lb,
           hgrn_out_g, w_out, norm2_g, w_router_group, b_router_group, w_router_expert,
           b_router_expert, w_gate, w_up, w_down):
    b, s, d = x.shape
    l = 0
    aw = ATTN_WIDTH
    mod = _ada(c, w_ada[l], b_ada[l]).reshape(b, 6, d)

    w = w_in[l]
    wq = w[:, 0:aw].astype(BF16)
    wk = w[:, aw:2 * aw].astype(BF16)
    wv = w[:, 2 * aw:3 * aw].astype(BF16)
    f0 = 3 * aw
    pad_f = LANES - FOX_PIECES * ATTN_HEADS
    wf = jnp.pad(jnp.tile(w[:, f0:f0 + ATTN_HEADS], (1, FOX_PIECES)), ((0, 0), (0, pad_f))).astype(BF16)
    wh = w[:, f0 + ATTN_HEADS:].astype(BF16)
    bfox = jnp.pad(jnp.tile(b_fox[l], FOX_PIECES), (0, pad_f)).reshape(1, LANES)
    gq = jnp.tile(q_norm_g[l], ATTN_HEADS).reshape(1, aw)
    gk = jnp.tile(k_norm_g[l], ATTN_HEADS).reshape(1, aw)

    tm = min(512, s)
    q, k, v, ck, hq, hf, hi, hg = _inproj(x, mod, norm1_g[l].reshape(1, d), wq, wk, wv, wf, wh,
                                           gq, gk, bfox, tm)
    ao = _fox(q, k, v, ck, attn_out_g[l].reshape(1, aw), min(256, s), 128)
    ho = _hgrn(hq, hf, hi, hg, hgrn_lb[0:2], hgrn_out_g[l].reshape(1, HGRN_WIDTH))

    wr = jnp.pad(jnp.concatenate([w_router_group[l], w_router_expert[l]], axis=1),
                 ((0, 0), (0, LANES - N_GROUPS - N_EXPERTS)))
    br = jnp.pad(jnp.concatenate([b_router_group[l], b_router_expert[l]]),
                 (0, LANES - N_GROUPS - N_EXPERTS)).reshape(1, LANES)
    x1, h2, gates = _outproj(x, ao, ho, mod, w_out[l].astype(BF16), norm2_g[l].reshape(1, d),
                             wr, br, tm)
    return _moe(x1, h2, gates, mod, w_gate[l].astype(BF16), w_up[l].astype(BF16),
                w_down[l].astype(BF16), min(1024, s))
```

```python
import functools

import numpy as np
import jax
import jax.numpy as jnp
from jax import lax
from jax.experimental import pallas as pl
from jax.experimental.pallas import tpu as pltpu

F32 = jnp.float32
BF16 = jnp.bfloat16

D_MODEL = 1024
ATTN_HEAD_DIM = 64
ATTN_WIDTH = 512
ATTN_HEADS = 8
HGRN_WIDTH = 512
HGRN_HEADS = 4
HGRN_DK = 128
N_GROUPS = 4
EXPERTS_PER_GROUP = 4
N_EXPERTS = 16
D_EXPERT = 512
NORM_EPS = 1e-6
LANES = 128
SUBLANES = 8
VMEM_LIMIT = 56 * 1024 * 1024

HGRN_CHUNK = 128
HGRN_LEVELS = (8, 16, 32, 64)
ROUTE_W_LANE = 2 * N_EXPERTS
MOE_TD = 256
MOE_ALIGN = 16
MOE_TM = 512
MOE_L = 2 * MOE_TD + MOE_ALIGN * N_EXPERTS
LOG2E = 1.4426950408889634
FOX_GROUP = 4
FOX_PIECES = 3


def _sigmoid(x):
    return 1.0 / (1.0 + jnp.exp(-x))


def _silu(x):
    return x * _sigmoid(x)


def _split3(x):
    p1 = x.astype(BF16)
    r1 = x - p1.astype(F32)
    p2 = r1.astype(BF16)
    p3 = (r1 - p2.astype(F32)).astype(BF16)
    return p1, p2, p3


def _dot(a, b):
    return jnp.dot(a, b, preferred_element_type=F32)


def _dot_nt(a, b):
    return lax.dot_general(a, b, (((1,), (1,)), ((), ())), preferred_element_type=F32)


def _ada_kernel(c_ref, w_ref, b_ref, o_ref):
    c = c_ref[...]
    o_ref[...] = jnp.dot(_silu(c), w_ref[...], preferred_element_type=F32,
                         precision=lax.Precision.HIGHEST) + b_ref[...]


def _ada(c, w, b):
    bsz, d = c.shape
    n = w.shape[1]
    tn = 1024
    return pl.pallas_call(
        _ada_kernel,
        grid=(n // tn,),
        in_specs=[pl.BlockSpec((bsz, d), lambda j: (0, 0)),
                  pl.BlockSpec((d, tn), lambda j: (0, j)),
                  pl.BlockSpec((1, tn), lambda j: (0, j))],
        out_specs=pl.BlockSpec((bsz, tn), lambda j: (0, j)),
        out_shape=jax.ShapeDtypeStruct((bsz, n), F32),
        compiler_params=pltpu.CompilerParams(dimension_semantics=("arbitrary",),
                                             vmem_limit_bytes=VMEM_LIMIT),
        name="ada",
    )(c, w, b.reshape(1, n))


def _inproj_kernel(x_ref, mod_ref, g1_ref, wq_ref, wk_ref, wv_ref, wf_ref, wh_ref,
                   gq_ref, gk_ref, bf_ref, gsum_ref, tri_ref, place_ref,
                   q_out, k_out, v_out, ck_out, hq_out, hf_out, hi_out, hg_out,
                   carry_ref):
    si = pl.program_id(1)

    @pl.when(si == 0)
    def _():
        carry_ref[...] = jnp.zeros_like(carry_ref)

    x = x_ref[0]
    shift = mod_ref[0, 0:1, :]
    scale = mod_ref[0, 1:2, :]
    ms = jnp.mean(x * x, axis=-1, keepdims=True)
    h = (x * lax.rsqrt(ms + NORM_EPS) * g1_ref[...]) * (1.0 + scale) + shift
    hb = h.astype(BF16)

    def qk_norm(w_ref, g_ref, mult):
        t = _dot(hb, w_ref[...])
        ssq = _dot((t * t).astype(BF16), gsum_ref[...])
        return t * lax.rsqrt(ssq * (1.0 / ATTN_HEAD_DIM) + NORM_EPS) * (g_ref[...] * mult)

    q_out[0] = qk_norm(wq_ref, gq_ref, ATTN_HEAD_DIM ** -0.5 * LOG2E).T.astype(BF16)
    k_out[0] = qk_norm(wk_ref, gk_ref, 1.0).astype(BF16)
    v_out[0] = _dot(hb, wv_ref[...]).T.astype(BF16)

    af = _dot(hb, wf_ref[...]) + bf_ref[...]
    lf = jnp.minimum(af, 0.0) - jnp.log(1.0 + jnp.exp(-jnp.abs(af)))
    tri = tri_ref[...]
    p1, p2, p3 = _split3(lf)
    cum = (_dot(tri, p1) + _dot(tri, p2)) + _dot(tri, p3) + carry_ref[...]
    tm = cum.shape[0]
    carry_ref[...] = cum[tm - 1:tm, :]
    c1, c2, c3 = _split3(cum * (-LOG2E))
    lane = lax.broadcasted_iota(jnp.int32, cum.shape, 1)
    zero = jnp.zeros_like(c1)
    pieces = jnp.where(lane < ATTN_HEADS, c1,
                       jnp.where(lane < 2 * ATTN_HEADS, c2,
                                 jnp.where(lane < 3 * ATTN_HEADS, c3, zero)))
    ck_out[0] = _dot(pieces, place_ref[...]).astype(BF16)

    hq_out[0] = _dot(hb, wh_ref[:, 0 * HGRN_WIDTH:1 * HGRN_WIDTH]).astype(BF16)
    hf_out[0] = _dot(hb, wh_ref[:, 1 * HGRN_WIDTH:2 * HGRN_WIDTH]).astype(BF16)
    hi_out[0] = _dot(hb, wh_ref[:, 2 * HGRN_WIDTH:3 * HGRN_WIDTH]).astype(BF16)
    hg_out[0] = _dot(hb, wh_ref[:, 3 * HGRN_WIDTH:4 * HGRN_WIDTH]).astype(BF16)


def _inproj(x, mod, g1, wq, wk, wv, wf, wh, gq, gk, bfox, tm):
    b, s, d = x.shape
    gsum = jnp.asarray(np.kron(np.eye(ATTN_HEADS), np.ones((ATTN_HEAD_DIM, ATTN_HEAD_DIM))), BF16)
    tri = jnp.asarray(np.tril(np.ones((tm, tm))), BF16)
    place_np = np.zeros((LANES, ATTN_HEADS * LANES), np.float32)
    for piece in range(FOX_PIECES):
        for hd in range(ATTN_HEADS):
            place_np[piece * ATTN_HEADS + hd, hd * LANES + piece] = 1.0
    place = jnp.asarray(place_np, BF16)
    const = lambda shape: pl.BlockSpec(shape, lambda bi, si: (0,) * len(shape))
    tok = lambda w: pl.BlockSpec((1, tm, w), lambda bi, si: (bi, si, 0))
    tok_t = lambda w: pl.BlockSpec((1, w, tm), lambda bi, si: (bi, 0, si))
    act = lambda w: jax.ShapeDtypeStruct((b, s, w), BF16)
    act_t = lambda w: jax.ShapeDtypeStruct((b, w, s), BF16)
    return pl.pallas_call(
        _inproj_kernel,
        grid=(b, s // tm),
        in_specs=[tok(d),
                  pl.BlockSpec((1, 6, d), lambda bi, si: (bi, 0, 0)),
                  const((1, d)),
                  const((d, ATTN_WIDTH)), const((d, ATTN_WIDTH)), const((d, ATTN_WIDTH)),
                  const((d, LANES)), const((d, 4 * HGRN_WIDTH)),
                  const((1, ATTN_WIDTH)), const((1, ATTN_WIDTH)), const((1, LANES)),
                  const((ATTN_WIDTH, ATTN_WIDTH)), const((tm, tm)),
                  const((LANES, ATTN_HEADS * LANES))],
        out_specs=[tok_t(ATTN_WIDTH), tok(ATTN_WIDTH), tok_t(ATTN_WIDTH),
                   tok(ATTN_HEADS * LANES),
                   tok(HGRN_WIDTH), tok(HGRN_WIDTH), tok(HGRN_WIDTH), tok(HGRN_WIDTH)],
        out_shape=[act_t(ATTN_WIDTH), act(ATTN_WIDTH), act_t(ATTN_WIDTH),
                   act(ATTN_HEADS * LANES),
                   act(HGRN_WIDTH), act(HGRN_WIDTH), act(HGRN_WIDTH), act(HGRN_WIDTH)],
        scratch_shapes=[pltpu.VMEM((1, LANES), F32)],
        compiler_params=pltpu.CompilerParams(dimension_semantics=("arbitrary", "arbitrary"),
                                             vmem_limit_bytes=VMEM_LIMIT),
        name="inproj",
    )(x, mod, g1, wq, wk, wv, wf, wh, gq, gk, bfox, gsum, tri, place)


def _fox_kernel(qt_ref, k_ref, vt_ref, ck_ref, g_ref, o_ref, *, tq, tk):
    qi = pl.program_id(2)
    qt = qt_ref[0].astype(F32)
    row = lax.broadcasted_iota(jnp.int32, (LANES, tq), 0)
    first = row < ATTN_HEAD_DIM
    ones = jnp.where(row < FOX_PIECES, 1.0, 0.0)
    rhs = []
    for hd in range(FOX_GROUP):
        qp = qt[(hd // 2) * LANES:(hd // 2 + 1) * LANES, :]
        qh = jnp.where(first, qp, 0.0) if hd % 2 == 0 else jnp.where(first, 0.0, qp)
        rhs.append(jnp.concatenate([qh, ones], axis=0).astype(BF16))

    def block(j, carry, masked):
        k0 = pl.multiple_of(j * tk, tk)
        kb = k_ref[0, pl.ds(k0, tk), :]
        ckb = ck_ref[0, pl.ds(k0, tk), :]
        vtb = vt_ref[0, :, pl.ds(k0, tk)]
        out = []
        for hd in range(FOX_GROUP):
            m, l, acc = carry[hd]
            pr = slice((hd // 2) * LANES, (hd // 2 + 1) * LANES)
            lhs = jnp.concatenate([kb[:, pr], ckb[:, hd * LANES:(hd + 1) * LANES]], axis=1)
            st = _dot(lhs, rhs[hd])
            if masked:
                key = k0 + lax.broadcasted_iota(jnp.int32, (tk, tq), 0)
                qry = qi * tq + lax.broadcasted_iota(jnp.int32, (tk, tq), 1)
                st = jnp.where(key <= qry, st, -jnp.inf)
            m_new = jnp.maximum(m, jnp.max(st, axis=0, keepdims=True))
            alpha = jnp.exp2(m - m_new)
            pt = jnp.exp2(st - m_new)
            l = alpha * l + jnp.sum(pt, axis=0, keepdims=True)
            acc = alpha * acc + _dot(vtb[pr, :], pt.astype(BF16))
            out.append((m_new, l, acc))
        return tuple(out)

    per_q = tq // tk

    def full_blocks(i, carry):
        for u in range(per_q):
            carry = block(i * per_q + u, carry, False)
        return carry

    init = tuple((jnp.full((1, tq), -1e30, F32), jnp.zeros((1, tq), F32),
                  jnp.zeros((LANES, tq), F32)) for _ in range(FOX_GROUP))
    carry = lax.fori_loop(0, qi, full_blocks, init)
    for u in range(per_q):
        carry = block(qi * per_q + u, carry, True)

    head0 = lax.broadcasted_iota(jnp.int32, (tq, LANES), 1) < ATTN_HEAD_DIM
    for pr in range(FOX_GROUP // 2):
        (_, l0, a0), (_, l1, a1) = carry[2 * pr], carry[2 * pr + 1]
        ot = jnp.where(first, a0 * (1.0 / l0), a1 * (1.0 / l1))
        o = ot.T
        osq = o * o
        ss0 = jnp.sum(jnp.where(head0, osq, 0.0), axis=-1, keepdims=True)
        ss1 = jnp.sum(jnp.where(head0, 0.0, osq), axis=-1, keepdims=True)
        ms = jnp.where(head0, ss0, ss1) * (1.0 / ATTN_HEAD_DIM)
        sl = slice(pr * LANES, (pr + 1) * LANES)
        o_ref[0, :, sl] = (o * lax.rsqrt(ms + NORM_EPS) * g_ref[:, sl]).astype(BF16)


def _fox(qt, k, vt, ck, g_out, tq, tk):
    b, s, _ = k.shape
    groups = ATTN_HEADS // FOX_GROUP
    gw = FOX_GROUP * ATTN_HEAD_DIM
    return pl.pallas_call(
        functools.partial(_fox_kernel, tq=tq, tk=tk),
        grid=(b, groups, s // tq),
        in_specs=[pl.BlockSpec((1, gw, tq), lambda bi, g, qi: (bi, g, qi)),
                  pl.BlockSpec((1, s, gw), lambda bi, g, qi: (bi, 0, g)),
                  pl.BlockSpec((1, gw, s), lambda bi, g, qi: (bi, g, 0)),
                  pl.BlockSpec((1, s, FOX_GROUP * LANES), lambda bi, g, qi: (bi, 0, g)),
                  pl.BlockSpec((1, gw), lambda bi, g, qi: (0, g))],
        out_specs=pl.BlockSpec((1, tq, gw), lambda bi, g, qi: (bi, qi, g)),
        out_shape=jax.ShapeDtypeStruct((b, s, ATTN_WIDTH), BF16),
        compiler_params=pltpu.CompilerParams(
            dimension_semantics=("arbitrary", "arbitrary", "arbitrary"),
            vmem_limit_bytes=VMEM_LIMIT),
        name="fox",
    )(qt, k, vt, ck, g_out)


def _hgrn_decay_matrix(c):
    t = np.arange(c)[:, None]
    j = np.arange(c)[None, :]
    blocks = [(j <= t), (j > t)]
    for m in HGRN_LEVELS:
        mid = (t // (2 * m)) * (2 * m) + m
        right = (t % (2 * m)) >= m
        blocks.append(np.where(right, (j >= mid) & (j <= t), (j > t) & (j < mid)))
    return np.concatenate(blocks, axis=0).astype(np.float32)


def _hgrn_kernel(hq_ref, hf_ref, hi_ref, hg_ref, lb_ref, g_ref, w_ref, o_ref, st_ref):
    ci = pl.program_id(1)
    c = HGRN_CHUNK

    @pl.when(ci == 0)
    def _():
        st_ref[...] = jnp.zeros_like(st_ref)

    r0 = lb_ref[0:1, :]
    r1 = lb_ref[1:2, :]
    rmax = jnp.maximum(r0, r1)
    e0 = jnp.exp(r0 - rmax)
    lb = e0 / (e0 + jnp.exp(r1 - rmax))

    f = lb + (1.0 - lb) * _sigmoid(hf_ref[0].astype(F32))
    g = jnp.log(f)
    g1, g2, g3 = _split3(g)
    w = w_ref[...]
    xall = (_dot(w, g1) + _dot(w, g2)) + _dot(w, g3)
    eall = jnp.exp(xall)
    q_all = _silu(hq_ref[0].astype(F32))
    k_all = 1.0 - f
    v_all = hi_ref[0].astype(F32)

    row = lax.broadcasted_iota(jnp.int32, (c, HGRN_DK), 0)
    rowmod = row % SUBLANES
    rr = lax.broadcasted_iota(jnp.int32, (c, c), 0)
    cc = lax.broadcasted_iota(jnp.int32, (c, c), 1)

    for hd in range(HGRN_HEADS):
        sl = slice(hd * HGRN_DK, (hd + 1) * HGRN_DK)
        q, k, v, fh = q_all[:, sl], k_all[:, sl], v_all[:, sl], f[:, sl]
        e_pre = eall[0:c, sl]
        e_suf = eall[c:2 * c, sl]

        a = jnp.zeros((c, c), F32)
        for li, m in enumerate(HGRN_LEVELS):
            e = eall[(2 + li) * c:(3 + li) * c, sl]
            right = (row % (2 * m)) >= m
            qt = jnp.where(right, q * e, 0.0).astype(BF16)
            kt = jnp.where(right, 0.0, k * e).astype(BF16)
            same = (rr // (2 * m)) == (cc // (2 * m))
            a = a + jnp.where(same, _dot_nt(qt, kt), 0.0)
        vb = v.astype(BF16)
        out = _dot(a.astype(BF16), vb)

        out = out + jnp.sum(q * k, axis=-1, keepdims=True) * v
        prod = fh
        for dd in range(1, SUBLANES):
            kd = pltpu.roll(k, dd, 0)
            vd = pltpu.roll(v, dd, 0)
            coef = jnp.sum(q * prod * kd, axis=-1, keepdims=True)
            out = out + jnp.where(rowmod >= dd, coef, 0.0) * vd
            if dd + 1 < SUBLANES:
                prod = prod * pltpu.roll(fh, dd, 0)

        st = st_ref[hd]
        out = out + _dot_nt((q * e_pre).astype(BF16), st.astype(BF16))
        kdec = (k * e_suf).astype(BF16)
        upd = lax.dot_general(vb, kdec, (((0,), (0,)), ((), ())), preferred_element_type=F32)
        st_ref[hd] = st * e_pre[c - 1:c, :] + upd

        ms = jnp.mean(out * out, axis=-1, keepdims=True)
        gate = _silu(hg_ref[0, :, sl].astype(F32))
        o_ref[0, :, sl] = (out * lax.rsqrt(ms + NORM_EPS) * g_ref[:, sl] * gate).astype(BF16)


def _hgrn(hq, hf, hi, hg, lb_rows, g_out):
    b, s, wd = hq.shape
    c = HGRN_CHUNK
    wmat = jnp.asarray(_hgrn_decay_matrix(c), BF16)
    tok = pl.BlockSpec((1, c, wd), lambda bi, ci: (bi, ci, 0))
    return pl.pallas_call(
        _hgrn_kernel,
        grid=(b, s // c),
        in_specs=[tok, tok, tok, tok,
                  pl.BlockSpec((2, wd), lambda bi, ci: (0, 0)),
                  pl.BlockSpec((1, wd), lambda bi, ci: (0, 0)),
                  pl.BlockSpec(wmat.shape, lambda bi, ci: (0, 0))],
        out_specs=tok,
        out_shape=jax.ShapeDtypeStruct((b, s, wd), BF16),
        scratch_shapes=[pltpu.VMEM((HGRN_HEADS, HGRN_DK, HGRN_DK), F32)],
        compiler_params=pltpu.CompilerParams(dimension_semantics=("arbitrary", "arbitrary"),
                                             vmem_limit_bytes=VMEM_LIMIT),
        name="hgrn",
    )(hq, hf, hi, hg, lb_rows, g_out, wmat)


def _outproj_kernel(x_ref, ao_ref, ho_ref, mod_ref, wo_ref, g2_ref, wr_ref, br_ref,
                    x1_out, h2_out, route_out, cnt_out):
    gate1 = mod_ref[0, 2:3, :]
    shift2 = mod_ref[0, 3:4, :]
    scale2 = mod_ref[0, 4:5, :]
    mix = _dot(ao_ref[0], wo_ref[0:ATTN_WIDTH, :]) + _dot(ho_ref[0], wo_ref[ATTN_WIDTH:D_MODEL, :])
    x1 = x_ref[0] + gate1 * mix
    x1_out[0] = x1
    ms = jnp.mean(x1 * x1, axis=-1, keepdims=True)
    h2 = (x1 * lax.rsqrt(ms + NORM_EPS) * g2_ref[...]) * (1.0 + scale2) + shift2
    h2_out[0] = h2.astype(BF16)

    logits = jnp.dot(h2, wr_ref[...], preferred_element_type=F32,
                     precision=lax.Precision.HIGHEST) + br_ref[...]
    tm = logits.shape[0]
    lane = lax.broadcasted_iota(jnp.int32, (tm, LANES), 1)
    neg = -jnp.inf
    is_group = (lane >= N_EXPERTS) & (lane < N_EXPERTS + N_GROUPS)
    gl = jnp.where(is_group, logits, neg)
    gmax = jnp.max(gl, axis=-1, keepdims=True)
    gsum = jnp.sum(jnp.exp(gl - gmax), axis=-1, keepdims=True)
    group_p = 1.0 / gsum
    gidx = jnp.min(jnp.where(is_group & (gl == gmax), lane, LANES), axis=-1, keepdims=True) - N_EXPERTS
    in_group = (lane < N_EXPERTS) & ((lane // EXPERTS_PER_GROUP) == gidx)
    el = jnp.where(in_group, logits, neg)
    top1 = jnp.max(el, axis=-1, keepdims=True)
    idx1 = jnp.min(jnp.where(in_group & (el == top1), lane, LANES), axis=-1, keepdims=True)
    el2 = jnp.where(lane == idx1, neg, el)
    top2 = jnp.max(el2, axis=-1, keepdims=True)
    idx2 = jnp.min(jnp.where(in_group & (lane != idx1) & (el2 == top2), lane, LANES),
                   axis=-1, keepdims=True)
    e2 = jnp.exp(top2 - top1)
    w1 = group_p / (1.0 + e2)
    w2 = group_p * e2 / (1.0 + e2)
    route = jnp.where(lane == idx1, 1.0,
                      jnp.where(lane == idx2 + N_EXPERTS, 1.0,
                                jnp.where(lane == ROUTE_W_LANE, w1,
                                          jnp.where(lane == ROUTE_W_LANE + 1, w2, 0.0))))
    route_out[0] = route
    for sub in range(tm // MOE_TD):
        cnt_out[0, sub:sub + 1, :] = jnp.sum(route[sub * MOE_TD:(sub + 1) * MOE_TD], axis=0,
                                             keepdims=True).astype(jnp.int32)


def _outproj(x, ao, ho, mod, wo, g2, wr, br, tm):
    b, s, d = x.shape
    const = lambda shape: pl.BlockSpec(shape, lambda bi, si: (0,) * len(shape))
    tok = lambda w: pl.BlockSpec((1, tm, w), lambda bi, si: (bi, si, 0))
    return pl.pallas_call(
        _outproj_kernel,
        grid=(b, s // tm),
        in_specs=[tok(d), tok(ATTN_WIDTH), tok(HGRN_WIDTH),
                  pl.BlockSpec((1, 6, d), lambda bi, si: (bi, 0, 0)),
                  const((d, d)), const((1, d)), const((d, LANES)), const((1, LANES))],
        out_specs=[tok(d), tok(d), tok(LANES),
                   pl.BlockSpec((1, tm // MOE_TD, LANES), lambda bi, si: (bi * (s // tm) + si, 0, 0))],
        out_shape=[jax.ShapeDtypeStruct((b, s, d), F32),
                   jax.ShapeDtypeStruct((b, s, d), BF16),
                   jax.ShapeDtypeStruct((b, s, LANES), F32),
                   jax.ShapeDtypeStruct((b * s // tm, tm // MOE_TD, LANES), jnp.int32)],
        compiler_params=pltpu.CompilerParams(dimension_semantics=("arbitrary", "arbitrary"),
                                             vmem_limit_bytes=VMEM_LIMIT),
        name="outproj",
    )(x, ao, ho, mod, wo, g2, wr, br)


def _ceil_to(v, m):
    return ((v + (m - 1)) // m) * m


def _moe_constants():
    a = np.arange(LANES)
    ne = N_EXPERTS
    td = MOE_TD
    lstrict = np.tril(np.ones((td, td)), -1)
    fold = ((a[:, None] < 2 * ne) & (a[None, :] < 2 * ne) & (a[:, None] % ne == a[None, :] % ne))
    upper = ((a[:, None] < ne) & (a[None, :] < 2 * ne) & (a[:, None] < a[None, :] % ne))
    selrows = np.zeros((SUBLANES, LANES))
    selrows[0, :ne] = 1.0
    selrows[1, ne:2 * ne] = 1.0
    return tuple(jnp.asarray(m, BF16) for m in (lstrict, fold, upper, selrows))


def _local_slots(route, lstrict, fold, upper):
    lane = lax.broadcasted_iota(jnp.int32, route.shape, 1)
    member = jnp.where(lane < 2 * N_EXPERTS, route, 0.0)
    rank = _dot(_dot(lstrict, member.astype(BF16)).astype(BF16), fold)
    cnt = jnp.broadcast_to(jnp.sum(member, axis=0, keepdims=True), (SUBLANES, LANES))
    cnt = _dot(cnt.astype(BF16), fold)
    run = jnp.floor((cnt + (MOE_ALIGN - 1)) * (1.0 / MOE_ALIGN)) * MOE_ALIGN
    start = _dot(run.astype(BF16), upper)[0:1, :]
    return member * (start + rank)


def _plan_kernel(c_ref, base_ref, texp_ref, meta_ref, *, ntiles, n_row_tiles):
    ne = N_EXPERTS
    off = jnp.int32(0)
    for e in range(ne):
        def body(i, run, e=e, off=off):
            c = c_ref[i * 2 * ne + e] + c_ref[i * 2 * ne + ne + e]
            base_ref[i * ne + e] = off + run
            return run + _ceil_to(c, MOE_ALIGN)
        total = lax.fori_loop(0, ntiles, body, jnp.int32(0))
        nt = (total + (MOE_TM - 1)) // MOE_TM
        first = off // MOE_TM

        def fill(j, carry, e=e, first=first):
            texp_ref[first + j] = e
            return carry
        lax.fori_loop(0, nt, fill, 0)
        off = off + nt * MOE_TM
    nvalid = off // MOE_TM

    def fill_rest(j, carry):
        texp_ref[j] = ne - 1
        return carry
    lax.fori_loop(nvalid, n_row_tiles, fill_rest, 0)
    meta_ref[0] = nvalid


def _plan(counts_flat, ntiles, n_row_tiles):
    smem = pl.BlockSpec(memory_space=pltpu.SMEM)
    return pl.pallas_call(
        functools.partial(_plan_kernel, ntiles=ntiles, n_row_tiles=n_row_tiles),
        in_specs=[smem],
        out_specs=[smem, smem, smem],
        out_shape=[jax.ShapeDtypeStruct((ntiles * N_EXPERTS,), jnp.int32),
                   jax.ShapeDtypeStruct((n_row_tiles,), jnp.int32),
                   jax.ShapeDtypeStruct((1,), jnp.int32)],
        name="moe_plan",
    )(counts_flat)


def _run_chunks(c_ref, base_ref, tile, fn):
    ne = N_EXPERTS
    local = jnp.int32(0)
    for e in range(ne):
        c = c_ref[tile * 2 * ne + e] + c_ref[tile * 2 * ne + ne + e]
        nchunk = (c + (MOE_ALIGN - 1)) // MOE_ALIGN
        hbm = base_ref[tile * ne + e]

        def body(j, carry, hbm=hbm, local=local):
            fn(pl.multiple_of(hbm + j * MOE_ALIGN, MOE_ALIGN),
               pl.multiple_of(local + j * MOE_ALIGN, MOE_ALIGN))
            return carry
        lax.fori_loop(0, nchunk, body, 0)
        local = local + nchunk * MOE_ALIGN
    return local // MOE_ALIGN


def _dispatch_kernel(c_ref, base_ref, h2_ref, route_ref, lstrict_ref, fold_ref, upper_ref,
                     selrows_ref, xs_init, ws_init, xs_ref, ws_ref, xs_scr, ws_scr, sem, nch_ref):
    del xs_init, ws_init
    i = pl.program_id(0)
    n = pl.num_programs(0)
    slot = i % 2

    def copies(slot_, hbm, local):
        return (pltpu.make_async_copy(xs_scr.at[slot_, pl.ds(local, MOE_ALIGN), :],
                                      xs_ref.at[pl.ds(hbm, MOE_ALIGN), :], sem.at[0, slot_]),
                pltpu.make_async_copy(ws_scr.at[slot_, pl.ds(local, MOE_ALIGN), :],
                                      ws_ref.at[pl.ds(hbm, MOE_ALIGN), :], sem.at[1, slot_]))

    def wait_all(slot_):
        def body(j, carry):
            for cp in copies(slot_, 0, 0):
                cp.wait()
            return carry
        lax.fori_loop(0, nch_ref[slot_], body, 0)

    @pl.when(i >= 2)
    def _():
        wait_all(slot)

    route = route_ref[...]
    lane = lax.broadcasted_iota(jnp.int32, route.shape, 1)
    v = _local_slots(route, lstrict_ref[...], fold_ref[...], upper_ref[...])
    hi = jnp.floor(v * (1.0 / MOE_ALIGN))
    lo = v - hi * MOE_ALIGN
    sel = selrows_ref[...]
    rows = _dot_nt(sel, hi.astype(BF16)) * MOE_ALIGN + _dot_nt(sel, lo.astype(BF16))
    slot_id = lax.broadcasted_iota(jnp.int32, (MOE_L, MOE_TD), 0)
    p1 = jnp.where(slot_id == rows[0:1, :].astype(jnp.int32), 1.0, 0.0)
    p2 = jnp.where(slot_id == rows[1:2, :].astype(jnp.int32), 1.0, 0.0)
    xs_scr[slot] = _dot((p1 + p2).astype(BF16), h2_ref[...]).astype(BF16)
    w = []
    for k in range(2):
        col = jnp.sum(jnp.where(lane == ROUTE_W_LANE + k, route, 0.0), axis=-1, keepdims=True)
        wrep = jnp.broadcast_to(col, route.shape)
        whi = wrep.astype(BF16)
        w.append((whi, (wrep - whi.astype(F32)).astype(BF16)))
    p1b, p2b = p1.astype(BF16), p2.astype(BF16)
    ws_scr[slot] = (_dot(p1b, w[0][0]) + _dot(p1b, w[0][1])) + (_dot(p2b, w[1][0]) + _dot(p2b, w[1][1]))

    def start(hbm, local):
        for cp in copies(slot, hbm, local):
            cp.start()
    nch_ref[slot] = _run_chunks(c_ref, base_ref, i, start)

    @pl.when(i == n - 1)
    def _():
        wait_all(slot)

        @pl.when(n >= 2)
        def _():
            wait_all(1 - slot)


def _dispatch(counts_flat, base, h2, route, n_rows):
    t, d = h2.shape
    ntiles = t // MOE_TD
    lstrict, fold, upper, selrows = _moe_constants()
    const = lambda shape: pl.BlockSpec(shape, lambda i, c, b: (0,) * len(shape))
    tok = lambda w: pl.BlockSpec((MOE_TD, w), lambda i, c, b: (i, 0))
    any_spec = pl.BlockSpec(memory_space=pl.ANY)
    grid_spec = pltpu.PrefetchScalarGridSpec(
        num_scalar_prefetch=2,
        grid=(ntiles,),
        in_specs=[tok(d), tok(LANES), const((MOE_TD, MOE_TD)), const((LANES, LANES)),
                  const((LANES, LANES)), const((SUBLANES, LANES)), any_spec, any_spec],
        out_specs=[any_spec, any_spec],
        scratch_shapes=[pltpu.VMEM((2, MOE_L, d), BF16), pltpu.VMEM((2, MOE_L, LANES), F32),
                        pltpu.SemaphoreType.DMA((2, 2)), pltpu.SMEM((2,), jnp.int32)])
    return pl.pallas_call(
        _dispatch_kernel,
        grid_spec=grid_spec,
        out_shape=[jax.ShapeDtypeStruct((n_rows, d), BF16),
                   jax.ShapeDtypeStruct((n_rows, LANES), F32)],
        input_output_aliases={8: 0, 9: 1},
        compiler_params=pltpu.CompilerParams(dimension_semantics=("arbitrary",),
                                             vmem_limit_bytes=VMEM_LIMIT),
        name="moe_dispatch",
    )(counts_flat, base, h2, route, lstrict, fold, upper, selrows,
      jnp.zeros((n_rows, d), BF16), jnp.zeros((n_rows, LANES), F32))


def _experts_kernel(texp_ref, meta_ref, xs_ref, ws_ref, wg_ref, wu_ref, wd_ref, ys_ref):
    del texp_ref

    @pl.when(pl.program_id(0) >= meta_ref[0])
    def _():
        ys_ref[...] = jnp.zeros_like(ys_ref)

    @pl.when(pl.program_id(0) < meta_ref[0])
    def _():
        x = xs_ref[...]
        act = _silu(_dot(x, wg_ref[0])) * _dot(x, wu_ref[0])
        y = _dot(act.astype(BF16), wd_ref[0])
        w = ws_ref[...]
        ys_ref[...] = (y * jnp.concatenate([w] * (y.shape[1] // LANES), axis=1)).astype(BF16)


def _experts(texp, meta, xs, ws, wg, wu, wd):
    n_rows, d = xs.shape
    row = lambda w: pl.BlockSpec((MOE_TM, w), lambda i, te, mt: (jnp.minimum(i, mt[0] - 1), 0))
    wspec = lambda shape: pl.BlockSpec((1,) + shape, lambda i, te, mt: (te[i], 0, 0))
    grid_spec = pltpu.PrefetchScalarGridSpec(
        num_scalar_prefetch=2,
        grid=(n_rows // MOE_TM,),
        in_specs=[row(d), row(LANES), wspec((d, D_EXPERT)), wspec((d, D_EXPERT)),
                  wspec((D_EXPERT, d))],
        out_specs=pl.BlockSpec((MOE_TM, d), lambda i, te, mt: (i, 0)))
    return pl.pallas_call(
        _experts_kernel,
        grid_spec=grid_spec,
        out_shape=jax.ShapeDtypeStruct((n_rows, d), BF16),
        compiler_params=pltpu.CompilerParams(dimension_semantics=("arbitrary",),
                                             vmem_limit_bytes=VMEM_LIMIT),
        name="moe_experts",
    )(texp, meta, xs, ws, wg, wu, wd)


def _combine_kernel(c_ref, base_ref, x1_ref, route_ref, mod_ref, lstrict_ref, fold_ref, upper_ref,
                    ys_ref, o_ref, ys_scr, sem, nch_ref):
    i = pl.program_id(0)
    n = pl.num_programs(0)
    slot = i % 2

    def copy(slot_, hbm, local):
        return pltpu.make_async_copy(ys_ref.at[pl.ds(hbm, MOE_ALIGN), :],
                                     ys_scr.at[slot_, pl.ds(local, MOE_ALIGN), :], sem.at[slot_])

    def fetch(tile, slot_):
        nch_ref[slot_] = _run_chunks(c_ref, base_ref, tile,
                                     lambda hbm, local: copy(slot_, hbm, local).start())

    @pl.when(i == 0)
    def _():
        ys_scr[...] = jnp.zeros_like(ys_scr)
        fetch(0, 0)

    @pl.when(i + 1 < n)
    def _():
        fetch(i + 1, 1 - slot)

    route = route_ref[...]
    lane = lax.broadcasted_iota(jnp.int32, route.shape, 1)
    v = _local_slots(route, lstrict_ref[...], fold_ref[...], upper_ref[...])
    r1 = jnp.sum(jnp.where(lane < N_EXPERTS, v, 0.0), axis=-1, keepdims=True).astype(jnp.int32)
    r2 = jnp.sum(jnp.where(lane < N_EXPERTS, 0.0, v), axis=-1, keepdims=True).astype(jnp.int32)
    col = lax.broadcasted_iota(jnp.int32, (MOE_TD, MOE_L), 1)
    pick = (jnp.where(col == r1, 1.0, 0.0) + jnp.where(col == r2, 1.0, 0.0)).astype(BF16)

    def wait(j, carry):
        copy(slot, 0, 0).wait()
        return carry
    lax.fori_loop(0, nch_ref[slot], wait, 0)
    y = _dot(pick, ys_scr[slot])
    o_ref[...] = x1_ref[...] + mod_ref[0, 5:6, :] * y


def _combine(counts_flat, base, x1, route, mod, ys, tiles_per_batch):
    t, d = x1.shape
    lstrict, fold, upper, _ = _moe_constants()
    const = lambda shape: pl.BlockSpec(shape, lambda i, c, b: (0,) * len(shape))
    tok = lambda w: pl.BlockSpec((MOE_TD, w), lambda i, c, b: (i, 0))
    grid_spec = pltpu.PrefetchScalarGridSpec(
        num_scalar_prefetch=2,
        grid=(t // MOE_TD,),
        in_specs=[tok(d), tok(LANES),
                  pl.BlockSpec((1, 6, d), lambda i, c, b: (i // tiles_per_batch, 0, 0)),
                  const((MOE_TD, MOE_TD)), const((LANES, LANES)), const((LANES, LANES)),
                  pl.BlockSpec(memory_space=pl.ANY)],
        out_specs=tok(d),
        scratch_shapes=[pltpu.VMEM((2, MOE_L, d), BF16), pltpu.SemaphoreType.DMA((2,)),
                        pltpu.SMEM((2,), jnp.int32)])
    return pl.pallas_call(
        _combine_kernel,
        grid_spec=grid_spec,
        out_shape=jax.ShapeDtypeStruct((t, d), F32),
        compiler_params=pltpu.CompilerParams(dimension_semantics=("arbitrary",),
                                             vmem_limit_bytes=VMEM_LIMIT),
        name="moe_combine",
    )(counts_flat, base, x1, route, mod, lstrict, fold, upper, ys)


def kernel(x, c, w_ada, b_ada, norm1_g, w_in, b_fox, q_norm_g, k_norm_g, attn_out_g, hgrn_lb,
           hgrn_out_g, w_out, norm2_g, w_router_group, b_router_group, w_router_expert,
           b_router_expert, w_gate, w_up, w_down):
    b, s, d = x.shape
    l = 0
    aw = ATTN_WIDTH
    mod = _ada(c, w_ada[l], b_ada[l]).reshape(b, 6, d)

    w = w_in[l]
    wq = w[:, 0:aw].astype(BF16)
    wk = w[:, aw:2 * aw].astype(BF16)
    wv = w[:, 2 * aw:3 * aw].astype(BF16)
    f0 = 3 * aw
    pad_f = LANES - FOX_PIECES * ATTN_HEADS
    wf = jnp.pad(jnp.tile(w[:, f0:f0 + ATTN_HEADS], (1, FOX_PIECES)), ((0, 0), (0, pad_f))).astype(BF16)
    wh = w[:, f0 + ATTN_HEADS:].astype(BF16)
    bfox = jnp.pad(jnp.tile(b_fox[l], FOX_PIECES), (0, pad_f)).reshape(1, LANES)
    gq = jnp.tile(q_norm_g[l], ATTN_HEADS).reshape(1, aw)
    gk = jnp.tile(k_norm_g[l], ATTN_HEADS).reshape(1, aw)

    tm = min(512, s)
    q, k, v, ck, hq, hf, hi, hg = _inproj(x, mod, norm1_g[l].reshape(1, d), wq, wk, wv, wf, wh,
                                           gq, gk, bfox, tm)
    ao = _fox(q, k, v, ck, attn_out_g[l].reshape(1, aw), min(256, s), 128)
    ho = _hgrn(hq, hf, hi, hg, hgrn_lb[0:2], hgrn_out_g[l].reshape(1, HGRN_WIDTH))

    wr = jnp.pad(jnp.concatenate([w_router_expert[l], w_router_group[l]], axis=1),
                 ((0, 0), (0, LANES - N_GROUPS - N_EXPERTS)))
    br = jnp.pad(jnp.concatenate([b_router_expert[l], b_router_group[l]]),
                 (0, LANES - N_GROUPS - N_EXPERTS)).reshape(1, LANES)
    x1, h2, route, counts = _outproj(x, ao, ho, mod, w_out[l].astype(BF16),
                                     norm2_g[l].reshape(1, d), wr, br, tm)

    t = b * s
    ntiles = t // MOE_TD
    counts_flat = counts.reshape(ntiles, LANES)[:, :2 * N_EXPERTS].reshape(-1)
    n_rows = _ceil_to(2 * t + ntiles * N_EXPERTS * (MOE_ALIGN - 1), MOE_TM) + N_EXPERTS * MOE_TM
    base, texp, meta = _plan(counts_flat, ntiles, n_rows // MOE_TM)
    route2 = route.reshape(t, LANES)
    xs, ws = _dispatch(counts_flat, base, h2.reshape(t, d), route2, n_rows)
    ys = _experts(texp, meta, xs, ws, w_gate[l].astype(BF16), w_up[l].astype(BF16),
                  w_down[l].astype(BF16))
    out = _combine(counts_flat, base, x1.reshape(t, d), route2, mod, ys, s // MOE_TD)
    return out.reshape(b, s, d)
```

```python
import functools

import numpy as np
import jax
import jax.numpy as jnp
from jax import lax
from jax.experimental import pallas as pl
from jax.experimental.pallas import tpu as pltpu

F32 = jnp.float32
BF16 = jnp.bfloat16

D_MODEL = 1024
ATTN_HEAD_DIM = 64
ATTN_WIDTH = 512
ATTN_HEADS = 8
HGRN_WIDTH = 512
HGRN_HEADS = 4
HGRN_DK = 128
N_GROUPS = 4
EXPERTS_PER_GROUP = 4
N_EXPERTS = 16
D_EXPERT = 512
NORM_EPS = 1e-6
LANES = 128
SUBLANES = 8
VMEM_LIMIT = 56 * 1024 * 1024

HGRN_CHUNK = 128
HGRN_LEVELS = (8, 16, 32, 64)
ROUTE_W_LANE = 2 * N_EXPERTS
MOE_TD = 256
MOE_ALIGN = 16
MOE_TM = 512
MOE_L = 2 * MOE_TD + MOE_ALIGN * N_EXPERTS
LOG2E = 1.4426950408889634
FOX_GROUP = 4
FOX_PIECES = 3


def _sigmoid(x):
    return 1.0 / (1.0 + jnp.exp(-x))


def _silu(x):
    return x * _sigmoid(x)


def _split3(x):
    p1 = x.astype(BF16)
    r1 = x - p1.astype(F32)
    p2 = r1.astype(BF16)
    p3 = (r1 - p2.astype(F32)).astype(BF16)
    return p1, p2, p3


def _dot(a, b):
    return jnp.dot(a, b, preferred_element_type=F32)


def _dot_nt(a, b):
    return lax.dot_general(a, b, (((1,), (1,)), ((), ())), preferred_element_type=F32)


def _ada_kernel(c_ref, w_ref, b_ref, o_ref):
    c = c_ref[...]
    o_ref[...] = jnp.dot(_silu(c), w_ref[...], preferred_element_type=F32,
                         precision=lax.Precision.HIGHEST) + b_ref[...]


def _ada(c, w, b):
    bsz, d = c.shape
    n = w.shape[1]
    tn = 1024
    return pl.pallas_call(
        _ada_kernel,
        grid=(n // tn,),
        in_specs=[pl.BlockSpec((bsz, d), lambda j: (0, 0)),
                  pl.BlockSpec((d, tn), lambda j: (0, j)),
                  pl.BlockSpec((1, tn), lambda j: (0, j))],
        out_specs=pl.BlockSpec((bsz, tn), lambda j: (0, j)),
        out_shape=jax.ShapeDtypeStruct((bsz, n), F32),
        compiler_params=pltpu.CompilerParams(dimension_semantics=("arbitrary",),
                                             vmem_limit_bytes=VMEM_LIMIT),
        name="ada",
    )(c, w, b.reshape(1, n))


def _inproj_kernel(x_ref, mod_ref, g1_ref, wq_ref, wk_ref, wv_ref, wf_ref, wh_ref,
                   gq_ref, gk_ref, bf_ref, gsum_ref, tri_ref, place_ref,
                   q_out, k_out, v_out, ck_out, hq_out, hf_out, hi_out, hg_out,
                   carry_ref):
    si = pl.program_id(1)

    @pl.when(si == 0)
    def _():
        carry_ref[...] = jnp.zeros_like(carry_ref)

    x = x_ref[0]
    shift = mod_ref[0, 0:1, :]
    scale = mod_ref[0, 1:2, :]
    ms = jnp.mean(x * x, axis=-1, keepdims=True)
    h = (x * lax.rsqrt(ms + NORM_EPS) * g1_ref[...]) * (1.0 + scale) + shift
    hb = h.astype(BF16)

    def qk_norm(w_ref, g_ref, mult):
        t = _dot(hb, w_ref[...])
        ssq = _dot((t * t).astype(BF16), gsum_ref[...])
        return t * lax.rsqrt(ssq * (1.0 / ATTN_HEAD_DIM) + NORM_EPS) * (g_ref[...] * mult)

    q_out[0] = qk_norm(wq_ref, gq_ref, ATTN_HEAD_DIM ** -0.5 * LOG2E).T.astype(BF16)
    k_out[0] = qk_norm(wk_ref, gk_ref, 1.0).astype(BF16)
    v_out[0] = _dot(hb, wv_ref[...]).T.astype(BF16)

    af = _dot(hb, wf_ref[...]) + bf_ref[...]
    lf = jnp.minimum(af, 0.0) - jnp.log(1.0 + jnp.exp(-jnp.abs(af)))
    tri = tri_ref[...]
    p1, p2, p3 = _split3(lf)
    cum = (_dot(tri, p1) + _dot(tri, p2)) + _dot(tri, p3) + carry_ref[...]
    tm = cum.shape[0]
    carry_ref[...] = cum[tm - 1:tm, :]
    c1, c2, c3 = _split3(cum * (-LOG2E))
    lane = lax.broadcasted_iota(jnp.int32, cum.shape, 1)
    zero = jnp.zeros_like(c1)
    pieces = jnp.where(lane < ATTN_HEADS, c1,
                       jnp.where(lane < 2 * ATTN_HEADS, c2,
                                 jnp.where(lane < 3 * ATTN_HEADS, c3, zero)))
    ck_out[0] = _dot(pieces, place_ref[...]).astype(BF16)

    hq_out[0] = _dot(hb, wh_ref[:, 0 * HGRN_WIDTH:1 * HGRN_WIDTH]).astype(BF16)
    hf_out[0] = _dot(hb, wh_ref[:, 1 * HGRN_WIDTH:2 * HGRN_WIDTH]).astype(BF16)
    hi_out[0] = _dot(hb, wh_ref[:, 2 * HGRN_WIDTH:3 * HGRN_WIDTH]).astype(BF16)
    hg_out[0] = _dot(hb, wh_ref[:, 3 * HGRN_WIDTH:4 * HGRN_WIDTH]).astype(BF16)


def _inproj(x, mod, g1, wq, wk, wv, wf, wh, gq, gk, bfox, tm):
    b, s, d = x.shape
    gsum = jnp.asarray(np.kron(np.eye(ATTN_HEADS), np.ones((ATTN_HEAD_DIM, ATTN_HEAD_DIM))), BF16)
    tri = jnp.asarray(np.tril(np.ones((tm, tm))), BF16)
    place_np = np.zeros((LANES, ATTN_HEADS * LANES), np.float32)
    for piece in range(FOX_PIECES):
        for hd in range(ATTN_HEADS):
            place_np[piece * ATTN_HEADS + hd, hd * LANES + piece] = 1.0
    place = jnp.asarray(place_np, BF16)
    const = lambda shape: pl.BlockSpec(shape, lambda bi, si: (0,) * len(shape))
    tok = lambda w: pl.BlockSpec((1, tm, w), lambda bi, si: (bi, si, 0))
    tok_t = lambda w: pl.BlockSpec((1, w, tm), lambda bi, si: (bi, 0, si))
    act = lambda w: jax.ShapeDtypeStruct((b, s, w), BF16)
    act_t = lambda w: jax.ShapeDtypeStruct((b, w, s), BF16)
    return pl.pallas_call(
        _inproj_kernel,
        grid=(b, s // tm),
        in_specs=[tok(d),
                  pl.BlockSpec((1, 6, d), lambda bi, si: (bi, 0, 0)),
                  const((1, d)),
                  const((d, ATTN_WIDTH)), const((d, ATTN_WIDTH)), const((d, ATTN_WIDTH)),
                  const((d, LANES)), const((d, 4 * HGRN_WIDTH)),
                  const((1, ATTN_WIDTH)), const((1, ATTN_WIDTH)), const((1, LANES)),
                  const((ATTN_WIDTH, ATTN_WIDTH)), const((tm, tm)),
                  const((LANES, ATTN_HEADS * LANES))],
        out_specs=[tok_t(ATTN_WIDTH), tok(ATTN_WIDTH), tok_t(ATTN_WIDTH),
                   tok(ATTN_HEADS * LANES),
                   tok(HGRN_WIDTH), tok(HGRN_WIDTH), tok(HGRN_WIDTH), tok(HGRN_WIDTH)],
        out_shape=[act_t(ATTN_WIDTH), act(ATTN_WIDTH), act_t(ATTN_WIDTH),
                   act(ATTN_HEADS * LANES),
                   act(HGRN_WIDTH), act(HGRN_WIDTH), act(HGRN_WIDTH), act(HGRN_WIDTH)],
        scratch_shapes=[pltpu.VMEM((1, LANES), F32)],
        compiler_params=pltpu.CompilerParams(dimension_semantics=("arbitrary", "arbitrary"),
                                             vmem_limit_bytes=VMEM_LIMIT),
        name="inproj",
    )(x, mod, g1, wq, wk, wv, wf, wh, gq, gk, bfox, gsum, tri, place)


def _fox_kernel(qt_ref, k_ref, vt_ref, ck_ref, g_ref, o_ref, *, tq, tk):
    qi = pl.program_id(2)
    qt = qt_ref[0].astype(F32)
    row = lax.broadcasted_iota(jnp.int32, (LANES, tq), 0)
    first = row < ATTN_HEAD_DIM
    ones = jnp.where(row < FOX_PIECES, 1.0, 0.0)
    rhs = []
    for hd in range(FOX_GROUP):
        qp = qt[(hd // 2) * LANES:(hd // 2 + 1) * LANES, :]
        qh = jnp.where(first, qp, 0.0) if hd % 2 == 0 else jnp.where(first, 0.0, qp)
        rhs.append(jnp.concatenate([qh, ones], axis=0).astype(BF16))

    def block(j, carry, masked):
        k0 = pl.multiple_of(j * tk, tk)
        kb = k_ref[0, pl.ds(k0, tk), :]
        ckb = ck_ref[0, pl.ds(k0, tk), :]
        vtb = vt_ref[0, :, pl.ds(k0, tk)]
        out = []
        for hd in range(FOX_GROUP):
            m, l, acc = carry[hd]
            pr = slice((hd // 2) * LANES, (hd // 2 + 1) * LANES)
            lhs = jnp.concatenate([kb[:, pr], ckb[:, hd * LANES:(hd + 1) * LANES]], axis=1)
            st = _dot(lhs, rhs[hd])
            if masked:
                key = k0 + lax.broadcasted_iota(jnp.int32, (tk, tq), 0)
                qry = qi * tq + lax.broadcasted_iota(jnp.int32, (tk, tq), 1)
                st = jnp.where(key <= qry, st, -jnp.inf)
            m_new = jnp.maximum(m, jnp.max(st, axis=0, keepdims=True))
            alpha = jnp.exp2(m - m_new)
            pt = jnp.exp2(st - m_new)
            l = alpha * l + jnp.sum(pt, axis=0, keepdims=True)
            acc = alpha * acc + _dot(vtb[pr, :], pt.astype(BF16))
            out.append((m_new, l, acc))
        return tuple(out)

    per_q = tq // tk

    def full_blocks(i, carry):
        for u in range(per_q):
            carry = block(i * per_q + u, carry, False)
        return carry

    init = tuple((jnp.full((1, tq), -1e30, F32), jnp.zeros((1, tq), F32),
                  jnp.zeros((LANES, tq), F32)) for _ in range(FOX_GROUP))
    carry = lax.fori_loop(0, qi, full_blocks, init)
    for u in range(per_q):
        carry = block(qi * per_q + u, carry, True)

    head0 = lax.broadcasted_iota(jnp.int32, (tq, LANES), 1) < ATTN_HEAD_DIM
    for pr in range(FOX_GROUP // 2):
        (_, l0, a0), (_, l1, a1) = carry[2 * pr], carry[2 * pr + 1]
        ot = jnp.where(first, a0 * (1.0 / l0), a1 * (1.0 / l1))
        o = ot.T
        osq = o * o
        ss0 = jnp.sum(jnp.where(head0, osq, 0.0), axis=-1, keepdims=True)
        ss1 = jnp.sum(jnp.where(head0, 0.0, osq), axis=-1, keepdims=True)
        ms = jnp.where(head0, ss0, ss1) * (1.0 / ATTN_HEAD_DIM)
        sl = slice(pr * LANES, (pr + 1) * LANES)
        o_ref[0, :, sl] = (o * lax.rsqrt(ms + NORM_EPS) * g_ref[:, sl]).astype(BF16)


def _fox(qt, k, vt, ck, g_out, tq, tk):
    b, s, _ = k.shape
    groups = ATTN_HEADS // FOX_GROUP
    gw = FOX_GROUP * ATTN_HEAD_DIM
    return pl.pallas_call(
        functools.partial(_fox_kernel, tq=tq, tk=tk),
        grid=(b, groups, s // tq),
        in_specs=[pl.BlockSpec((1, gw, tq), lambda bi, g, qi: (bi, g, qi)),
                  pl.BlockSpec((1, s, gw), lambda bi, g, qi: (bi, 0, g)),
                  pl.BlockSpec((1, gw, s), lambda bi, g, qi: (bi, g, 0)),
                  pl.BlockSpec((1, s, FOX_GROUP * LANES), lambda bi, g, qi: (bi, 0, g)),
                  pl.BlockSpec((1, gw), lambda bi, g, qi: (0, g))],
        out_specs=pl.BlockSpec((1, tq, gw), lambda bi, g, qi: (bi, qi, g)),
        out_shape=jax.ShapeDtypeStruct((b, s, ATTN_WIDTH), BF16),
        compiler_params=pltpu.CompilerParams(
            dimension_semantics=("arbitrary", "arbitrary", "arbitrary"),
            vmem_limit_bytes=VMEM_LIMIT),
        name="fox",
    )(qt, k, vt, ck, g_out)


def _hgrn_decay_matrix(c):
    t = np.arange(c)[:, None]
    j = np.arange(c)[None, :]
    blocks = [(j <= t), (j > t)]
    for m in HGRN_LEVELS:
        mid = (t // (2 * m)) * (2 * m) + m
        right = (t % (2 * m)) >= m
        blocks.append(np.where(right, (j >= mid) & (j <= t), (j > t) & (j < mid)))
    return np.concatenate(blocks, axis=0).astype(np.float32)


def _hgrn_kernel(hq_ref, hf_ref, hi_ref, hg_ref, lb_ref, g_ref, w_ref, lvl_ref, o_ref, st_ref):
    ci = pl.program_id(1)
    c = HGRN_CHUNK

    @pl.when(ci == 0)
    def _():
        st_ref[...] = jnp.zeros_like(st_ref)

    r0 = lb_ref[0:1, :]
    r1 = lb_ref[1:2, :]
    rmax = jnp.maximum(r0, r1)
    e0 = jnp.exp(r0 - rmax)
    lb = e0 / (e0 + jnp.exp(r1 - rmax))

    f = lb + (1.0 - lb) * _sigmoid(hf_ref[0].astype(F32))
    g = jnp.log(f)
    g1, g2, g3 = _split3(g)
    w = w_ref[...]
    xall = (_dot(w, g1) + _dot(w, g2)) + _dot(w, g3)
    eall = jnp.exp(xall)
    q_all = _silu(hq_ref[0].astype(F32))
    k_all = 1.0 - f
    v_all = hi_ref[0].astype(F32)

    row = lax.broadcasted_iota(jnp.int32, (c, HGRN_DK), 0)
    rowmod = row % SUBLANES
    def roll8(x, shift):
        x3 = x.reshape(c // SUBLANES, SUBLANES, HGRN_DK)
        return pltpu.roll(x3, shift, 1).reshape(c, HGRN_DK)

    lvl = lvl_ref[...]
    rights = [(row % (2 * m)) >= m for m in HGRN_LEVELS]

    for hd in range(HGRN_HEADS):
        sl = slice(hd * HGRN_DK, (hd + 1) * HGRN_DK)
        q, k, v, fh = q_all[:, sl], k_all[:, sl], v_all[:, sl], f[:, sl]
        e_pre = eall[0:c, sl]
        e_suf = eall[c:2 * c, sl]

        a = jnp.zeros((c, c), F32)
        for li, m in enumerate(HGRN_LEVELS):
            e = eall[(2 + li) * c:(3 + li) * c, sl]
            right = rights[li]
            qt =jnp.where(right, q * e, 0.0).astype(BF16)
            kt = jnp.where(right, 0.0, k * e).astype(BF16)
            a = a + jnp.where(lvl == float(li), _dot_nt(qt, kt), 0.0)
        vb = v.astype(BF16)
        out = _dot(a.astype(BF16), vb)

        out = out + jnp.sum(q * k, axis=-1, keepdims=True) * v
        prod = fh
        for dd in range(1, SUBLANES):
            kd = roll8(k, dd)
            vd = roll8(v, dd)
            coef = jnp.sum(q * prod * kd, axis=-1, keepdims=True)
            out = out + jnp.where(rowmod >= dd, coef, 0.0) * vd
            if dd + 1 < SUBLANES:
                prod = prod * roll8(fh, dd)

        st = st_ref[hd]
        out = out + _dot_nt((q * e_pre).astype(BF16), st.astype(BF16))
        kdec = (k * e_suf).astype(BF16)
        upd = lax.dot_general(vb, kdec, (((0,), (0,)), ((), ())), preferred_element_type=F32)
        st_ref[hd] = st * e_pre[c - 1:c, :] + upd

        ms = jnp.mean(out * out, axis=-1, keepdims=True)
        gate = _silu(hg_ref[0, :, sl].astype(F32))
        o_ref[0, :, sl] = (out * lax.rsqrt(ms + NORM_EPS) * g_ref[:, sl] * gate).astype(BF16)


def _hgrn(hq, hf, hi, hg, lb_rows, g_out):
    b, s, wd = hq.shape
    c = HGRN_CHUNK
    wmat = jnp.asarray(_hgrn_decay_matrix(c), BF16)
    tt = np.arange(c)[:, None]
    ss = np.arange(c)[None, :]
    lvl_np = np.full((c, c), -1.0, np.float32)
    for li, m in reversed(list(enumerate(HGRN_LEVELS))):
        lvl_np[(ss < tt) & (tt // (2 * m) == ss // (2 * m)) & (tt // m != ss // m)] = li
    lvl = jnp.asarray(lvl_np)
    tok = pl.BlockSpec((1, c, wd), lambda bi, ci: (bi, ci, 0))
    return pl.pallas_call(
        _hgrn_kernel,
        grid=(b, s // c),
        in_specs=[tok, tok, tok, tok,
                  pl.BlockSpec((2, wd), lambda bi, ci: (0, 0)),
                  pl.BlockSpec((1, wd), lambda bi, ci: (0, 0)),
                  pl.BlockSpec(wmat.shape, lambda bi, ci: (0, 0)),
                  pl.BlockSpec((c, c), lambda bi, ci: (0, 0))],
        out_specs=tok,
        out_shape=jax.ShapeDtypeStruct((b, s, wd), BF16),
        scratch_shapes=[pltpu.VMEM((HGRN_HEADS, HGRN_DK, HGRN_DK), F32)],
        compiler_params=pltpu.CompilerParams(dimension_semantics=("arbitrary", "arbitrary"),
                                             vmem_limit_bytes=VMEM_LIMIT),
        name="hgrn",
    )(hq, hf, hi, hg, lb_rows, g_out, wmat, lvl)


def _outproj_kernel(x_ref, ao_ref, ho_ref, mod_ref, wo_ref, g2_ref, wr_ref, br_ref,
                    x1_out, h2_out, route_out, cnt_out):
    gate1 = mod_ref[0, 2:3, :]
    shift2 = mod_ref[0, 3:4, :]
    scale2 = mod_ref[0, 4:5, :]
    mix = _dot(ao_ref[0], wo_ref[0:ATTN_WIDTH, :]) + _dot(ho_ref[0], wo_ref[ATTN_WIDTH:D_MODEL, :])
    x1 = x_ref[0] + gate1 * mix
    x1_out[0] = x1
    ms = jnp.mean(x1 * x1, axis=-1, keepdims=True)
    h2 = (x1 * lax.rsqrt(ms + NORM_EPS) * g2_ref[...]) * (1.0 + scale2) + shift2
    h2_out[0] = h2.astype(BF16)

    h2_hi = h2.astype(BF16)
    h2_lo = (h2 - h2_hi.astype(F32)).astype(BF16)
    logits = (_dot(h2_hi, wr_ref[0]) + (_dot(h2_hi, wr_ref[1]) + _dot(h2_lo, wr_ref[0]))) + br_ref[...]
    tm = logits.shape[0]
    lane = lax.broadcasted_iota(jnp.int32, (tm, LANES), 1)
    neg = -jnp.inf
    is_group = (lane >= N_EXPERTS) & (lane < N_EXPERTS + N_GROUPS)
    gl = jnp.where(is_group, logits, neg)
    gmax = jnp.max(gl, axis=-1, keepdims=True)
    gsum = jnp.sum(jnp.exp(gl - gmax), axis=-1, keepdims=True)
    group_p = 1.0 / gsum
    gidx = jnp.min(jnp.where(is_group & (gl == gmax), lane, LANES), axis=-1, keepdims=True) - N_EXPERTS
    in_group = (lane < N_EXPERTS) & ((lane // EXPERTS_PER_GROUP) == gidx)
    el = jnp.where(in_group, logits, neg)
    top1 = jnp.max(el, axis=-1, keepdims=True)
    idx1 = jnp.min(jnp.where(in_group & (el == top1), lane, LANES), axis=-1, keepdims=True)
    el2 = jnp.where(lane == idx1, neg, el)
    top2 = jnp.max(el2, axis=-1, keepdims=True)
    idx2 = jnp.min(jnp.where(in_group & (lane != idx1) & (el2 == top2), lane, LANES),
                   axis=-1, keepdims=True)
    e2 = jnp.exp(top2 - top1)
    w1 = group_p / (1.0 + e2)
    w2 = group_p * e2 / (1.0 + e2)
    route = jnp.where(lane == idx1, 1.0,
                      jnp.where(lane == idx2 + N_EXPERTS, 1.0,
                                jnp.where(lane == ROUTE_W_LANE, w1,
                                          jnp.where(lane == ROUTE_W_LANE + 1, w2, 0.0))))
    route_out[0] = route
    for sub in range(tm // MOE_TD):
        cnt_out[0, sub:sub + 1, :] = jnp.sum(route[sub * MOE_TD:(sub + 1) * MOE_TD], axis=0,
                                             keepdims=True).astype(jnp.int32)


def _outproj(x, ao, ho, mod, wo, g2, wr, br, tm):
    b, s, d = x.shape
    const = lambda shape: pl.BlockSpec(shape, lambda bi, si: (0,) * len(shape))
    tok = lambda w: pl.BlockSpec((1, tm, w), lambda bi, si: (bi, si, 0))
    return pl.pallas_call(
        _outproj_kernel,
        grid=(b, s // tm),
        in_specs=[tok(d), tok(ATTN_WIDTH), tok(HGRN_WIDTH),
                  pl.BlockSpec((1, 6, d), lambda bi, si: (bi, 0, 0)),
                  const((d, d)), const((1, d)), const((2, d, LANES)), const((1, LANES))],
        out_specs=[tok(d), tok(d), tok(LANES),
                   pl.BlockSpec((1, tm // MOE_TD, LANES), lambda bi, si: (bi * (s // tm) + si, 0, 0))],
        out_shape=[jax.ShapeDtypeStruct((b, s, d), F32),
                   jax.ShapeDtypeStruct((b, s, d), BF16),
                   jax.ShapeDtypeStruct((b, s, LANES), F32),
                   jax.ShapeDtypeStruct((b * s // tm, tm // MOE_TD, LANES), jnp.int32)],
        compiler_params=pltpu.CompilerParams(dimension_semantics=("arbitrary", "arbitrary"),
                                             vmem_limit_bytes=VMEM_LIMIT),
        name="outproj",
    )(x, ao, ho, mod, wo, g2, wr, br)


def _ceil_to(v, m):
    return ((v + (m - 1)) // m) * m


def _moe_constants():
    a = np.arange(LANES)
    ne = N_EXPERTS
    td = MOE_TD
    lstrict = np.tril(np.ones((td, td)), -1)
    fold = ((a[:, None] < 2 * ne) & (a[None, :] < 2 * ne) & (a[:, None] % ne == a[None, :] % ne))
    upper = ((a[:, None] < ne) & (a[None, :] < 2 * ne) & (a[:, None] < a[None, :] % ne))
    selrows = np.zeros((SUBLANES, LANES))
    selrows[0, :ne] = 1.0
    selrows[1, ne:2 * ne] = 1.0
    return tuple(jnp.asarray(m, BF16) for m in (lstrict, fold, upper, selrows))


def _local_slots(route, lstrict, fold, upper):
    lane = lax.broadcasted_iota(jnp.int32, route.shape, 1)
    member = jnp.where(lane < 2 * N_EXPERTS, route, 0.0)
    rank = _dot(_dot(lstrict, member.astype(BF16)).astype(BF16), fold)
    cnt = jnp.broadcast_to(jnp.sum(member, axis=0, keepdims=True), (SUBLANES, LANES))
    cnt = _dot(cnt.astype(BF16), fold)
    run = jnp.floor((cnt + (MOE_ALIGN - 1)) * (1.0 / MOE_ALIGN)) * MOE_ALIGN
    start = _dot(run.astype(BF16), upper)[0:1, :]
    return member * (start + rank)


def _plan_kernel(c_ref, base_ref, texp_ref, meta_ref, *, ntiles, n_row_tiles):
    ne = N_EXPERTS
    off = jnp.int32(0)
    for e in range(ne):
        def body(i, run, e=e, off=off):
            c = c_ref[i * 2 * ne + e] + c_ref[i * 2 * ne + ne + e]
            base_ref[i * ne + e] = off + run
            return run + _ceil_to(c, MOE_ALIGN)
        total = lax.fori_loop(0, ntiles, body, jnp.int32(0))
        nt = (total + (MOE_TM - 1)) // MOE_TM
        first = off // MOE_TM

        def fill(j, carry, e=e, first=first):
            texp_ref[first + j] = e
            return carry
        lax.fori_loop(0, nt, fill, 0)
        off = off + nt * MOE_TM
    nvalid = off // MOE_TM

    def fill_rest(j, carry):
        texp_ref[j] = ne - 1
        return carry
    lax.fori_loop(nvalid, n_row_tiles, fill_rest, 0)
    meta_ref[0] = nvalid


def _plan(counts_flat, ntiles, n_row_tiles):
    smem = pl.BlockSpec(memory_space=pltpu.SMEM)
    return pl.pallas_call(
        functools.partial(_plan_kernel, ntiles=ntiles, n_row_tiles=n_row_tiles),
        in_specs=[smem],
        out_specs=[smem, smem, smem],
        out_shape=[jax.ShapeDtypeStruct((ntiles * N_EXPERTS,), jnp.int32),
                   jax.ShapeDtypeStruct((n_row_tiles,), jnp.int32),
                   jax.ShapeDtypeStruct((1,), jnp.int32)],
        name="moe_plan",
    )(counts_flat)


def _run_chunks(c_ref, base_ref, tile, fn):
    ne = N_EXPERTS
    local = jnp.int32(0)
    for e in range(ne):
        c = c_ref[tile * 2 * ne + e] + c_ref[tile * 2 * ne + ne + e]
        nchunk = (c + (MOE_ALIGN - 1)) // MOE_ALIGN
        hbm = base_ref[tile * ne + e]

        def body(j, carry, hbm=hbm, local=local):
            fn(pl.multiple_of(hbm + j * MOE_ALIGN, MOE_ALIGN),
               pl.multiple_of(local + j * MOE_ALIGN, MOE_ALIGN))
            return carry
        lax.fori_loop(0, nchunk, body, 0)
        local = local + nchunk * MOE_ALIGN
    return local // MOE_ALIGN


def _dispatch_kernel(c_ref, base_ref, h2_ref, route_ref, lstrict_ref, fold_ref, upper_ref,
                     selrows_ref, xs_init, ws_init, xs_ref, ws_ref, xs_scr, ws_scr, sem, nch_ref):
    del xs_init, ws_init
    i = pl.program_id(0)
    n = pl.num_programs(0)
    slot = i % 2

    def copies(slot_, hbm, local):
        return (pltpu.make_async_copy(xs_scr.at[slot_, pl.ds(local, MOE_ALIGN), :],
                                      xs_ref.at[pl.ds(hbm, MOE_ALIGN), :], sem.at[0, slot_]),
                pltpu.make_async_copy(ws_scr.at[slot_, pl.ds(local, MOE_ALIGN), :],
                                      ws_ref.at[pl.ds(hbm, MOE_ALIGN), :], sem.at[1, slot_]))

    def wait_all(slot_):
        def body(j, carry):
            for cp in copies(slot_, 0, 0):
                cp.wait()
            return carry
        lax.fori_loop(0, nch_ref[slot_], body, 0)

    @pl.when(i >= 2)
    def _():
        wait_all(slot)

    route = route_ref[...]
    lane = lax.broadcasted_iota(jnp.int32, route.shape, 1)
    v = _local_slots(route, lstrict_ref[...], fold_ref[...], upper_ref[...])
    hi = jnp.floor(v * (1.0 / MOE_ALIGN))
    lo = v - hi * MOE_ALIGN
    sel = selrows_ref[...]
    rows = _dot_nt(sel, hi.astype(BF16)) * MOE_ALIGN + _dot_nt(sel, lo.astype(BF16))
    slot_id = lax.broadcasted_iota(jnp.int32, (MOE_L, MOE_TD), 0)
    p1 = jnp.where(slot_id == rows[0:1, :].astype(jnp.int32), 1.0, 0.0)
    p2 = jnp.where(slot_id == rows[1:2, :].astype(jnp.int32), 1.0, 0.0)
    xs_scr[slot] = _dot((p1 + p2).astype(BF16), h2_ref[...]).astype(BF16)
    w = []
    for k in range(2):
        col = jnp.sum(jnp.where(lane == ROUTE_W_LANE + k, route, 0.0), axis=-1, keepdims=True)
        wrep = jnp.broadcast_to(col, route.shape)
        whi = wrep.astype(BF16)
        w.append((whi, (wrep - whi.astype(F32)).astype(BF16)))
    p1b, p2b = p1.astype(BF16), p2.astype(BF16)
    ws_scr[slot] = (_dot(p1b, w[0][0]) + _dot(p1b, w[0][1])) + (_dot(p2b, w[1][0]) + _dot(p2b, w[1][1]))

    def start(hbm, local):
        for cp in copies(slot, hbm, local):
            cp.start()
    nch_ref[slot] = _run_chunks(c_ref, base_ref, i, start)

    @pl.when(i == n - 1)
    def _():
        wait_all(slot)

        @pl.when(n >= 2)
        def _():
            wait_all(1 - slot)


def _dispatch(counts_flat, base, h2, route, n_rows):
    t, d = h2.shape
    ntiles = t // MOE_TD
    lstrict, fold, upper, selrows = _moe_constants()
    const = lambda shape: pl.BlockSpec(shape, lambda i, c, b: (0,) * len(shape))
    tok = lambda w: pl.BlockSpec((MOE_TD, w), lambda i, c, b: (i, 0))
    any_spec = pl.BlockSpec(memory_space=pl.ANY)
    grid_spec = pltpu.PrefetchScalarGridSpec(
        num_scalar_prefetch=2,
        grid=(ntiles,),
        in_specs=[tok(d), tok(LANES), const((MOE_TD, MOE_TD)), const((LANES, LANES)),
                  const((LANES, LANES)), const((SUBLANES, LANES)), any_spec, any_spec],
        out_specs=[any_spec, any_spec],
        scratch_shapes=[pltpu.VMEM((2, MOE_L, d), BF16), pltpu.VMEM((2, MOE_L, LANES), F32),
                        pltpu.SemaphoreType.DMA((2, 2)), pltpu.SMEM((2,), jnp.int32)])
    return pl.pallas_call(
        _dispatch_kernel,
        grid_spec=grid_spec,
        out_shape=[jax.ShapeDtypeStruct((n_rows, d), BF16),
                   jax.ShapeDtypeStruct((n_rows, LANES), F32)],
        input_output_aliases={8: 0, 9: 1},
        compiler_params=pltpu.CompilerParams(dimension_semantics=("arbitrary",),
                                             vmem_limit_bytes=VMEM_LIMIT),
        name="moe_dispatch",
    )(counts_flat, base, h2, route, lstrict, fold, upper, selrows,
      jnp.zeros((n_rows, d), BF16), jnp.zeros((n_rows, LANES), F32))


def _experts_kernel(texp_ref, meta_ref, xs_ref, ws_ref, wg_ref, wu_ref, wd_ref, ys_ref):
    del texp_ref

    @pl.when(pl.program_id(0) >= meta_ref[0])
    def _():
        ys_ref[...] = jnp.zeros_like(ys_ref)

    @pl.when(pl.program_id(0) < meta_ref[0])
    def _():
        x = xs_ref[...]
        act = _silu(_dot(x, wg_ref[0])) * _dot(x, wu_ref[0])
        y = _dot(act.astype(BF16), wd_ref[0])
        w = ws_ref[...]
        ys_ref[...] = (y * jnp.concatenate([w] * (y.shape[1] // LANES), axis=1)).astype(BF16)


def _experts(texp, meta, xs, ws, wg, wu, wd):
    n_rows, d = xs.shape
    row = lambda w: pl.BlockSpec((MOE_TM, w), lambda i, te, mt: (jnp.minimum(i, mt[0] - 1), 0))
    wspec = lambda shape: pl.BlockSpec((1,) + shape, lambda i, te, mt: (te[i], 0, 0))
    grid_spec = pltpu.PrefetchScalarGridSpec(
        num_scalar_prefetch=2,
        grid=(n_rows // MOE_TM,),
        in_specs=[row(d), row(LANES), wspec((d, D_EXPERT)), wspec((d, D_EXPERT)),
                  wspec((D_EXPERT, d))],
        out_specs=pl.BlockSpec((MOE_TM, d), lambda i, te, mt: (i, 0)))
    return pl.pallas_call(
        _experts_kernel,
        grid_spec=grid_spec,
        out_shape=jax.ShapeDtypeStruct((n_rows, d), BF16),
        compiler_params=pltpu.CompilerParams(dimension_semantics=("arbitrary",),
                                             vmem_limit_bytes=VMEM_LIMIT),
        name="moe_experts",
    )(texp, meta, xs, ws, wg, wu, wd)


def _combine_kernel(c_ref, base_ref, x1_ref, route_ref, mod_ref, lstrict_ref, fold_ref, upper_ref,
                    ys_ref, o_ref, ys_scr, sem, nch_ref):
    i = pl.program_id(0)
    n = pl.num_programs(0)
    slot = i % 2

    def copy(slot_, hbm, local):
        return pltpu.make_async_copy(ys_ref.at[pl.ds(hbm, MOE_ALIGN), :],
                                     ys_scr.at[slot_, pl.ds(local, MOE_ALIGN), :], sem.at[slot_])

    def fetch(tile, slot_):
        nch_ref[slot_] = _run_chunks(c_ref, base_ref, tile,
                                     lambda hbm, local: copy(slot_, hbm, local).start())

    @pl.when(i == 0)
    def _():
        ys_scr[...] = jnp.zeros_like(ys_scr)
        fetch(0, 0)

    @pl.when(i + 1 < n)
    def _():
        fetch(i + 1, 1 - slot)

    route = route_ref[...]
    lane = lax.broadcasted_iota(jnp.int32, route.shape, 1)
    v = _local_slots(route, lstrict_ref[...], fold_ref[...], upper_ref[...])
    r1 = jnp.sum(jnp.where(lane < N_EXPERTS, v, 0.0), axis=-1, keepdims=True).astype(jnp.int32)
    r2 = jnp.sum(jnp.where(lane < N_EXPERTS, 0.0, v), axis=-1, keepdims=True).astype(jnp.int32)
    col = lax.broadcasted_iota(jnp.int32, (MOE_TD, MOE_L), 1)
    pick = (jnp.where(col == r1, 1.0, 0.0) + jnp.where(col == r2, 1.0, 0.0)).astype(BF16)

    def wait(j, carry):
        copy(slot, 0, 0).wait()
        return carry
    lax.fori_loop(0, nch_ref[slot], wait, 0)
    y = _dot(pick, ys_scr[slot])
    o_ref[...] = x1_ref[...] + mod_ref[0, 5:6, :] * y


def _combine(counts_flat, base, x1, route, mod, ys, tiles_per_batch):
    t, d = x1.shape
    lstrict, fold, upper, _ = _moe_constants()
    const = lambda shape: pl.BlockSpec(shape, lambda i, c, b: (0,) * len(shape))
    tok = lambda w: pl.BlockSpec((MOE_TD, w), lambda i, c, b: (i, 0))
    grid_spec = pltpu.PrefetchScalarGridSpec(
        num_scalar_prefetch=2,
        grid=(t // MOE_TD,),
        in_specs=[tok(d), tok(LANES),
                  pl.BlockSpec((1, 6, d), lambda i, c, b: (i // tiles_per_batch, 0, 0)),
                  const((MOE_TD, MOE_TD)), const((LANES, LANES)), const((LANES, LANES)),
                  pl.BlockSpec(memory_space=pl.ANY)],
        out_specs=tok(d),
        scratch_shapes=[pltpu.VMEM((2, MOE_L, d), BF16), pltpu.SemaphoreType.DMA((2,)),
                        pltpu.SMEM((2,), jnp.int32)])
    return pl.pallas_call(
        _combine_kernel,
        grid_spec=grid_spec,
        out_shape=jax.ShapeDtypeStruct((t, d), F32),
        compiler_params=pltpu.CompilerParams(dimension_semantics=("arbitrary",),
                                             vmem_limit_bytes=VMEM_LIMIT),
        name="moe_combine",
    )(counts_flat, base, x1, route, mod, lstrict, fold, upper, ys)


def kernel(x, c, w_ada, b_ada, norm1_g, w_in, b_fox, q_norm_g, k_norm_g, attn_out_g, hgrn_lb,
           hgrn_out_g, w_out, norm2_g, w_router_group, b_router_group, w_router_expert,
           b_router_expert, w_gate, w_up, w_down):
    b, s, d = x.shape
    l = 0
    aw = ATTN_WIDTH
    mod = _ada(c, w_ada[l], b_ada[l]).reshape(b, 6, d)

    w = w_in[l]
    wq = w[:, 0:aw].astype(BF16)
    wk = w[:, aw:2 * aw].astype(BF16)
    wv = w[:, 2 * aw:3 * aw].astype(BF16)
    f0 = 3 * aw
    pad_f = LANES - FOX_PIECES * ATTN_HEADS
    wf = jnp.pad(jnp.tile(w[:, f0:f0 + ATTN_HEADS], (1, FOX_PIECES)), ((0, 0), (0, pad_f))).astype(BF16)
    wh = w[:, f0 + ATTN_HEADS:].astype(BF16)
    bfox = jnp.pad(jnp.tile(b_fox[l], FOX_PIECES), (0, pad_f)).reshape(1, LANES)
    gq = jnp.tile(q_norm_g[l], ATTN_HEADS).reshape(1, aw)
    gk = jnp.tile(k_norm_g[l], ATTN_HEADS).reshape(1, aw)

    tm = min(512, s)
    q, k, v, ck, hq, hf, hi, hg = _inproj(x, mod, norm1_g[l].reshape(1, d), wq, wk, wv, wf, wh,
                                           gq, gk, bfox, tm)
    ao = _fox(q, k, v, ck, attn_out_g[l].reshape(1, aw), min(256, s), 128)
    ho = _hgrn(hq, hf, hi, hg, hgrn_lb[0:2], hgrn_out_g[l].reshape(1, HGRN_WIDTH))

    wr = jnp.pad(jnp.concatenate([w_router_expert[l], w_router_group[l]], axis=1),
                 ((0, 0), (0, LANES - N_GROUPS - N_EXPERTS)))
    br = jnp.pad(jnp.concatenate([b_router_expert[l], b_router_group[l]]),
                 (0, LANES - N_GROUPS - N_EXPERTS)).reshape(1, LANES)
    wr_hi = wr.astype(BF16)
    wr2 = jnp.stack([wr_hi, (wr - wr_hi.astype(F32)).astype(BF16)])
    x1, h2, route, counts = _outproj(x, ao, ho, mod, w_out[l].astype(BF16),
                                     norm2_g[l].reshape(1, d), wr2, br, tm)

    t = b * s
    ntiles = t // MOE_TD
    counts_flat = counts.reshape(ntiles, LANES)[:, :2 * N_EXPERTS].reshape(-1)
    n_rows = _ceil_to(2 * t + ntiles * N_EXPERTS * (MOE_ALIGN - 1), MOE_TM) + N_EXPERTS * MOE_TM
    base, texp, meta = _plan(counts_flat, ntiles, n_rows // MOE_TM)
    route2 = route.reshape(t, LANES)
    xs, ws = _dispatch(counts_flat, base, h2.reshape(t, d), route2, n_rows)
    ys = _experts(texp, meta, xs, ws, w_gate[l].astype(BF16), w_up[l].astype(BF16),
                  w_down[l].astype(BF16))
    out = _combine(counts_flat, base, x1.reshape(t, d), route2, mod, ys, s // MOE_TD)
    return out.reshape(b, s, d)
```

```python
import functools

import numpy as np
import jax
import jax.numpy as jnp
from jax import lax
from jax.experimental import pallas as pl
from jax.experimental.pallas import tpu as pltpu

F32 = jnp.float32
BF16 = jnp.bfloat16

D_MODEL = 1024
ATTN_HEAD_DIM = 64
ATTN_WIDTH = 512
ATTN_HEADS = 8
HGRN_WIDTH = 512
HGRN_HEADS = 4
HGRN_DK = 128
N_GROUPS = 4
EXPERTS_PER_GROUP = 4
N_EXPERTS = 16
D_EXPERT = 512
NORM_EPS = 1e-6
LANES = 128
SUBLANES = 8
VMEM_LIMIT = 56 * 1024 * 1024

HGRN_CHUNK = 128
HGRN_LEVELS = (8, 16, 32, 64)
ROUTE_W_LANE = 2 * N_EXPERTS
ROUTE_ROW_LANE = ROUTE_W_LANE + 2
MOE_TD = 256
MOE_ALIGN = 16
MOE_TM = 512
MOE_L = 2 * MOE_TD + MOE_ALIGN * N_EXPERTS
LOG2E = 1.4426950408889634
FOX_GROUP = 4
FOX_PIECES = 3


def _sigmoid(x):
    return 1.0 / (1.0 + jnp.exp(-x))


def _silu(x):
    return x * _sigmoid(x)


def _split3(x):
    p1 = x.astype(BF16)
    r1 = x - p1.astype(F32)
    p2 = r1.astype(BF16)
    p3 = (r1 - p2.astype(F32)).astype(BF16)
    return p1, p2, p3


def _dot(a, b):
    return jnp.dot(a, b, preferred_element_type=F32)


def _dot_nt(a, b):
    return lax.dot_general(a, b, (((1,), (1,)), ((), ())), preferred_element_type=F32)


def _ada_kernel(c_ref, w_ref, b_ref, o_ref):
    c = c_ref[...]
    o_ref[...] = jnp.dot(_silu(c), w_ref[...], preferred_element_type=F32,
                         precision=lax.Precision.HIGHEST) + b_ref[...]


def _ada(c, w, b):
    bsz, d = c.shape
    n = w.shape[1]
    tn = 1024
    return pl.pallas_call(
        _ada_kernel,
        grid=(n // tn,),
        in_specs=[pl.BlockSpec((bsz, d), lambda j: (0, 0)),
                  pl.BlockSpec((d, tn), lambda j: (0, j)),
                  pl.BlockSpec((1, tn), lambda j: (0, j))],
        out_specs=pl.BlockSpec((bsz, tn), lambda j: (0, j)),
        out_shape=jax.ShapeDtypeStruct((bsz, n), F32),
        compiler_params=pltpu.CompilerParams(dimension_semantics=("arbitrary",),
                                             vmem_limit_bytes=VMEM_LIMIT),
        name="ada",
    )(c, w, b.reshape(1, n))


def _inproj_kernel(x_ref, mod_ref, g1_ref, wq_ref, wk_ref, wv_ref, wf_ref, wh_ref,
                   gq_ref, gk_ref, bf_ref, gsum_ref, tri_ref, place_ref,
                   q_out, k_out, v_out, ck_out, hq_out, hf_out, hi_out, hg_out,
                   carry_ref):
    si = pl.program_id(1)

    @pl.when(si == 0)
    def _():
        carry_ref[...] = jnp.zeros_like(carry_ref)

    x = x_ref[0]
    shift = mod_ref[0, 0:1, :]
    scale = mod_ref[0, 1:2, :]
    ms = jnp.mean(x * x, axis=-1, keepdims=True)
    h = (x * lax.rsqrt(ms + NORM_EPS) * g1_ref[...]) * (1.0 + scale) + shift
    hb = h.astype(BF16)

    def qk_norm(w_ref, g_ref, mult):
        t = _dot(hb, w_ref[...])
        ssq = _dot((t * t).astype(BF16), gsum_ref[...])
        return t * lax.rsqrt(ssq * (1.0 / ATTN_HEAD_DIM) + NORM_EPS) * (g_ref[...] * mult)

    q_out[0] = qk_norm(wq_ref, gq_ref, ATTN_HEAD_DIM ** -0.5 * LOG2E).T.astype(BF16)
    k_out[0] = qk_norm(wk_ref, gk_ref, 1.0).astype(BF16)
    v_out[0] = _dot(hb, wv_ref[...]).T.astype(BF16)

    af = _dot(hb, wf_ref[...]) + bf_ref[...]
    lf = jnp.minimum(af, 0.0) - jnp.log(1.0 + jnp.exp(-jnp.abs(af)))
    tri = tri_ref[...]
    p1, p2, p3 = _split3(lf)
    cum = (_dot(tri, p1) + _dot(tri, p2)) + _dot(tri, p3) + carry_ref[...]
    tm = cum.shape[0]
    carry_ref[...] = cum[tm - 1:tm, :]
    c1, c2, c3 = _split3(cum * (-LOG2E))
    lane = lax.broadcasted_iota(jnp.int32, cum.shape, 1)
    zero = jnp.zeros_like(c1)
    pieces = jnp.where(lane < ATTN_HEADS, c1,
                       jnp.where(lane < 2 * ATTN_HEADS, c2,
                                 jnp.where(lane < 3 * ATTN_HEADS, c3, zero)))
    ck_out[0] = _dot(pieces, place_ref[...]).astype(BF16)

    hq_out[0] = _dot(hb, wh_ref[:, 0 * HGRN_WIDTH:1 * HGRN_WIDTH]).astype(BF16)
    hf_out[0] = _dot(hb, wh_ref[:, 1 * HGRN_WIDTH:2 * HGRN_WIDTH]).astype(BF16)
    hi_out[0] = _dot(hb, wh_ref[:, 2 * HGRN_WIDTH:3 * HGRN_WIDTH]).astype(BF16)
    hg_out[0] = _dot(hb, wh_ref[:, 3 * HGRN_WIDTH:4 * HGRN_WIDTH]).astype(BF16)


def _inproj(x, mod, g1, wq, wk, wv, wf, wh, gq, gk, bfox, tm):
    b, s, d = x.shape
    gsum = jnp.asarray(np.kron(np.eye(ATTN_HEADS), np.ones((ATTN_HEAD_DIM, ATTN_HEAD_DIM))), BF16)
    tri = jnp.asarray(np.tril(np.ones((tm, tm))), BF16)
    place_np = np.zeros((LANES, ATTN_HEADS * LANES), np.float32)
    for piece in range(FOX_PIECES):
        for hd in range(ATTN_HEADS):
            place_np[piece * ATTN_HEADS + hd, hd * LANES + piece] = 1.0
    place = jnp.asarray(place_np, BF16)
    const = lambda shape: pl.BlockSpec(shape, lambda bi, si: (0,) * len(shape))
    tok = lambda w: pl.BlockSpec((1, tm, w), lambda bi, si: (bi, si, 0))
    tok_t = lambda w: pl.BlockSpec((1, w, tm), lambda bi, si: (bi, 0, si))
    act = lambda w: jax.ShapeDtypeStruct((b, s, w), BF16)
    act_t = lambda w: jax.ShapeDtypeStruct((b, w, s), BF16)
    return pl.pallas_call(
        _inproj_kernel,
        grid=(b, s // tm),
        in_specs=[tok(d),
                  pl.BlockSpec((1, 6, d), lambda bi, si: (bi, 0, 0)),
                  const((1, d)),
                  const((d, ATTN_WIDTH)), const((d, ATTN_WIDTH)), const((d, ATTN_WIDTH)),
                  const((d, LANES)), const((d, 4 * HGRN_WIDTH)),
                  const((1, ATTN_WIDTH)), const((1, ATTN_WIDTH)), const((1, LANES)),
                  const((ATTN_WIDTH, ATTN_WIDTH)), const((tm, tm)),
                  const((LANES, ATTN_HEADS * LANES))],
        out_specs=[tok_t(ATTN_WIDTH), tok(ATTN_WIDTH), tok_t(ATTN_WIDTH),
                   tok(ATTN_HEADS * LANES),
                   tok(HGRN_WIDTH), tok(HGRN_WIDTH), tok(HGRN_WIDTH), tok(HGRN_WIDTH)],
        out_shape=[act_t(ATTN_WIDTH), act(ATTN_WIDTH), act_t(ATTN_WIDTH),
                   act(ATTN_HEADS * LANES),
                   act(HGRN_WIDTH), act(HGRN_WIDTH), act(HGRN_WIDTH), act(HGRN_WIDTH)],
        scratch_shapes=[pltpu.VMEM((1, LANES), F32)],
        compiler_params=pltpu.CompilerParams(dimension_semantics=("arbitrary", "arbitrary"),
                                             vmem_limit_bytes=VMEM_LIMIT),
        name="inproj",
    )(x, mod, g1, wq, wk, wv, wf, wh, gq, gk, bfox, gsum, tri, place)


def _fox_kernel(qt_ref, k_ref, vt_ref, ck_ref, g_ref, o_ref, st_scr, pt_scr, *, tq, tk):
    qi = pl.program_id(2)
    qt = qt_ref[0].astype(F32)
    row = lax.broadcasted_iota(jnp.int32, (LANES, tq), 0)
    first = row < ATTN_HEAD_DIM
    ones = jnp.where(row < FOX_PIECES, 1.0, 0.0)
    rhs = []
    for hd in range(FOX_GROUP):
        qp = qt[(hd // 2) * LANES:(hd // 2 + 1) * LANES, :]
        qh = jnp.where(first, qp, 0.0) if hd % 2 == 0 else jnp.where(first, 0.0, qp)
        rhs.append(jnp.concatenate([qh, ones], axis=0).astype(BF16))

    def stage_qk(j, slot):
        k0 = pl.multiple_of(j * tk, tk)
        kb = k_ref[0, pl.ds(k0, tk), :]
        ckb = ck_ref[0, pl.ds(k0, tk), :]
        for hd in range(FOX_GROUP):
            pr = slice((hd // 2) * LANES, (hd // 2 + 1) * LANES)
            lhs = jnp.concatenate([kb[:, pr], ckb[:, hd * LANES:(hd + 1) * LANES]], axis=1)
            st_scr[slot, hd] = _dot(lhs, rhs[hd])

    def stage_softmax(j, slot, stats, masked):
        out = []
        for hd in range(FOX_GROUP):
            m, l = stats[hd]
            st = st_scr[slot, hd]
            if masked:
                key = j * tk + lax.broadcasted_iota(jnp.int32, (tk, tq), 0)
                qry = qi * tq + lax.broadcasted_iota(jnp.int32, (tk, tq), 1)
                st = jnp.where(key <= qry, st, -jnp.inf)
            m_new = jnp.maximum(m, jnp.max(st, axis=0, keepdims=True))
            alpha = jnp.exp2(m - m_new)
            pt = jnp.exp2(st - m_new)
            pt_scr[slot, hd] = pt.astype(BF16)
            out.append(((m_new, alpha * l + jnp.sum(pt, axis=0, keepdims=True)), alpha))
        return tuple(o[0] for o in out), tuple(o[1] for o in out)

    def stage_pv(j, slot, accs, alphas):
        k0 = pl.multiple_of(jnp.maximum(j, 0) * tk, tk)
        vtb = vt_ref[0, :, pl.ds(k0, tk)]
        out = []
        for hd in range(FOX_GROUP):
            vth = vtb[hd * ATTN_HEAD_DIM:(hd + 1) * ATTN_HEAD_DIM, :]
            out.append(alphas[hd] * accs[hd] + _dot(vth, pt_scr[slot, hd]))
        return tuple(out)

    per_q = tq // tk
    stats = tuple((jnp.full((1, tq), -1e30, F32), jnp.zeros((1, tq), F32))
                  for _ in range(FOX_GROUP))
    accs = tuple(jnp.zeros((ATTN_HEAD_DIM, tq), F32) for _ in range(FOX_GROUP))
    alphas = tuple(jnp.ones((1, tq), F32) for _ in range(FOX_GROUP))
    pt_scr[1] = jnp.zeros_like(pt_scr[1])
    stage_qk(0, 0)

    def full_blocks(i, carry):
        stats, accs, alphas = carry
        for u in range(per_q):
            j = i * per_q + u
            accs = stage_pv(j - 1, 1 - u, accs, alphas)
            stats, alphas = stage_softmax(j, u, stats, False)
            stage_qk(j + 1, 1 - u)
        return stats, accs, alphas

    stats, accs, alphas = lax.fori_loop(0, qi, full_blocks, (stats, accs, alphas))
    j = qi * per_q
    accs = stage_pv(j - 1, 1, accs, alphas)
    stats, alphas = stage_softmax(j, 0, stats, True)
    stage_qk(j + 1, 1)
    accs = stage_pv(j, 0, accs, alphas)
    stats, alphas = stage_softmax(j + 1, 1, stats, True)
    accs = stage_pv(j + 1, 1, accs, alphas)
    carry = tuple((stats[hd][0], stats[hd][1], accs[hd]) for hd in range(FOX_GROUP))

    head0 = lax.broadcasted_iota(jnp.int32, (tq, LANES), 1) < ATTN_HEAD_DIM
    for pr in range(FOX_GROUP // 2):
        (_, l0, a0), (_, l1, a1) = carry[2 * pr], carry[2 * pr + 1]
        ot = jnp.concatenate([a0 * (1.0 / l0), a1 * (1.0 / l1)], axis=0)
        o = ot.T
        osq = o * o
        ss0 = jnp.sum(jnp.where(head0, osq, 0.0), axis=-1, keepdims=True)
        ss1 = jnp.sum(jnp.where(head0, 0.0, osq), axis=-1, keepdims=True)
        ms = jnp.where(head0, ss0, ss1) * (1.0 / ATTN_HEAD_DIM)
        sl = slice(pr * LANES, (pr + 1) * LANES)
        o_ref[0, :, sl] = (o * lax.rsqrt(ms + NORM_EPS) * g_ref[:, sl]).astype(BF16)


def _fox(qt, k, vt, ck, g_out, tq, tk):
    b, s, _ = k.shape
    groups = ATTN_HEADS // FOX_GROUP
    gw = FOX_GROUP * ATTN_HEAD_DIM
    return pl.pallas_call(
        functools.partial(_fox_kernel, tq=tq, tk=tk),
        grid=(b, groups, s // tq),
        in_specs=[pl.BlockSpec((1, gw, tq), lambda bi, g, qi: (bi, g, qi)),
                  pl.BlockSpec((1, s, gw), lambda bi, g, qi: (bi, 0, g)),
                  pl.BlockSpec((1, gw, s), lambda bi, g, qi: (bi, g, 0)),
                  pl.BlockSpec((1, s, FOX_GROUP * LANES), lambda bi, g, qi: (bi, 0, g)),
                  pl.BlockSpec((1, gw), lambda bi, g, qi: (0, g))],
        out_specs=pl.BlockSpec((1, tq, gw), lambda bi, g, qi: (bi, qi, g)),
        out_shape=jax.ShapeDtypeStruct((b, s, ATTN_WIDTH), BF16),
        scratch_shapes=[pltpu.VMEM((2, FOX_GROUP, tk, tq), F32),
                        pltpu.VMEM((2, FOX_GROUP, tk, tq), BF16)],
        compiler_params=pltpu.CompilerParams(
            dimension_semantics=("arbitrary", "arbitrary", "arbitrary"),
            vmem_limit_bytes=VMEM_LIMIT),
        name="fox",
    )(qt, k, vt, ck, g_out)


def _hgrn_decay_matrix(c):
    t = np.arange(c)[:, None]
    j = np.arange(c)[None, :]
    blocks = [(j <= t), (j > t)]
    for m in HGRN_LEVELS:
        mid = (t // (2 * m)) * (2 * m) + m
        right = (t % (2 * m)) >= m
        blocks.append(np.where(right, (j >= mid) & (j <= t), (j > t) & (j < mid)))
    return np.concatenate(blocks, axis=0).astype(np.float32)


def _hgrn_kernel(hq_ref, hf_ref, hi_ref, hg_ref, lb_ref, g_ref, w_ref, lvl_ref, o_ref, st_ref):
    ci = pl.program_id(1)
    c = HGRN_CHUNK

    @pl.when(ci == 0)
    def _():
        st_ref[...] = jnp.zeros_like(st_ref)

    r0 = lb_ref[0:1, :]
    r1 = lb_ref[1:2, :]
    rmax = jnp.maximum(r0, r1)
    e0 = jnp.exp(r0 - rmax)
    lb = e0 / (e0 + jnp.exp(r1 - rmax))

    f = lb + (1.0 - lb) * _sigmoid(hf_ref[0].astype(F32))
    g = jnp.log(f)
    g1, g2, g3 = _split3(g)
    w = w_ref[...]
    xall = (_dot(w, g1) + _dot(w, g2)) + _dot(w, g3)
    eall = jnp.exp(xall)
    q_all = _silu(hq_ref[0].astype(F32))
    k_all = 1.0 - f
    v_all = hi_ref[0].astype(F32)

    row = lax.broadcasted_iota(jnp.int32, (c, HGRN_DK), 0)
    rowmod = row % SUBLANES
    def roll8(x, shift):
        x3 = x.reshape(c // SUBLANES, SUBLANES, HGRN_DK)
        return pltpu.roll(x3, shift, 1).reshape(c, HGRN_DK)

    lvl = lvl_ref[...]
    rights = [(row % (2 * m)) >= m for m in HGRN_LEVELS]

    for hd in range(HGRN_HEADS):
        sl = slice(hd * HGRN_DK, (hd + 1) * HGRN_DK)
        q, k, v, fh = q_all[:, sl], k_all[:, sl], v_all[:, sl], f[:, sl]
        e_pre = eall[0:c, sl]
        e_suf = eall[c:2 * c, sl]

        a = jnp.zeros((c, c), F32)
        for li, m in enumerate(HGRN_LEVELS):
            e = eall[(2 + li) * c:(3 + li) * c, sl]
            right = rights[li]
            qt =jnp.where(right, q * e, 0.0).astype(BF16)
            kt = jnp.where(right, 0.0, k * e).astype(BF16)
            a = a + jnp.where(lvl == float(li), _dot_nt(qt, kt), 0.0)
        vb = v.astype(BF16)
        out = _dot(a.astype(BF16), vb)

        out = out + jnp.sum(q * k, axis=-1, keepdims=True) * v
        prod = fh
        for dd in range(1, SUBLANES):
            kd = roll8(k, dd)
            vd = roll8(v, dd)
            coef = jnp.sum(q * prod * kd, axis=-1, keepdims=True)
            out = out + jnp.where(rowmod >= dd, coef, 0.0) * vd
            if dd + 1 < SUBLANES:
                prod = prod * roll8(fh, dd)

        st = st_ref[hd]
        out = out + _dot_nt((q * e_pre).astype(BF16), st.astype(BF16))
        kdec = (k * e_suf).astype(BF16)
        upd = lax.dot_general(vb, kdec, (((0,), (0,)), ((), ())), preferred_element_type=F32)
        st_ref[hd] = st * e_pre[c - 1:c, :] + upd

        ms = jnp.mean(out * out, axis=-1, keepdims=True)
        gate = _silu(hg_ref[0, :, sl].astype(F32))
        o_ref[0, :, sl] = (out * lax.rsqrt(ms + NORM_EPS) * g_ref[:, sl] * gate).astype(BF16)


def _hgrn(hq, hf, hi, hg, lb_rows, g_out):
    b, s, wd = hq.shape
    c = HGRN_CHUNK
    wmat = jnp.asarray(_hgrn_decay_matrix(c), BF16)
    tt = np.arange(c)[:, None]
    ss = np.arange(c)[None, :]
    lvl_np = np.full((c, c), -1.0, np.float32)
    for li, m in reversed(list(enumerate(HGRN_LEVELS))):
        lvl_np[(ss < tt) & (tt // (2 * m) == ss // (2 * m)) & (tt // m != ss // m)] = li
    lvl = jnp.asarray(lvl_np)
    tok = pl.BlockSpec((1, c, wd), lambda bi, ci: (bi, ci, 0))
    return pl.pallas_call(
        _hgrn_kernel,
        grid=(b, s // c),
        in_specs=[tok, tok, tok, tok,
                  pl.BlockSpec((2, wd), lambda bi, ci: (0, 0)),
                  pl.BlockSpec((1, wd), lambda bi, ci: (0, 0)),
                  pl.BlockSpec(wmat.shape, lambda bi, ci: (0, 0)),
                  pl.BlockSpec((c, c), lambda bi, ci: (0, 0))],
        out_specs=tok,
        out_shape=jax.ShapeDtypeStruct((b, s, wd), BF16),
        scratch_shapes=[pltpu.VMEM((HGRN_HEADS, HGRN_DK, HGRN_DK), F32)],
        compiler_params=pltpu.CompilerParams(dimension_semantics=("arbitrary", "arbitrary"),
                                             vmem_limit_bytes=VMEM_LIMIT),
        name="hgrn",
    )(hq, hf, hi, hg, lb_rows, g_out, wmat, lvl)


def _outproj_kernel(x_ref, ao_ref, ho_ref, mod_ref, wo_ref, g2_ref, wr_ref, br_ref,
                    lstrict_ref, fold_ref, upper_ref, selrows_ref,
                    x1_out, h2_out, route_out, cnt_out, rows_out):
    gate1 = mod_ref[0, 2:3, :]
    shift2 = mod_ref[0, 3:4, :]
    scale2 = mod_ref[0, 4:5, :]
    mix = _dot(ao_ref[0], wo_ref[0:ATTN_WIDTH, :]) + _dot(ho_ref[0], wo_ref[ATTN_WIDTH:D_MODEL, :])
    x1 = x_ref[0] + gate1 * mix
    x1_out[0] = x1
    ms = jnp.mean(x1 * x1, axis=-1, keepdims=True)
    h2 = (x1 * lax.rsqrt(ms + NORM_EPS) * g2_ref[...]) * (1.0 + scale2) + shift2
    h2_out[0] = h2.astype(BF16)

    h2_hi = h2.astype(BF16)
    h2_lo = (h2 - h2_hi.astype(F32)).astype(BF16)
    logits = (_dot(h2_hi, wr_ref[0]) + (_dot(h2_hi, wr_ref[1]) + _dot(h2_lo, wr_ref[0]))) + br_ref[...]
    tm = logits.shape[0]
    lane = lax.broadcasted_iota(jnp.int32, (tm, LANES), 1)
    neg = -jnp.inf
    is_group = (lane >= N_EXPERTS) & (lane < N_EXPERTS + N_GROUPS)
    gl = jnp.where(is_group, logits, neg)
    gmax = jnp.max(gl, axis=-1, keepdims=True)
    gsum = jnp.sum(jnp.exp(gl - gmax), axis=-1, keepdims=True)
    group_p = 1.0 / gsum
    gidx = jnp.min(jnp.where(is_group & (gl == gmax), lane, LANES), axis=-1, keepdims=True) - N_EXPERTS
    in_group = (lane < N_EXPERTS) & ((lane // EXPERTS_PER_GROUP) == gidx)
    el = jnp.where(in_group, logits, neg)
    top1 = jnp.max(el, axis=-1, keepdims=True)
    idx1 = jnp.min(jnp.where(in_group & (el == top1), lane, LANES), axis=-1, keepdims=True)
    el2 = jnp.where(lane == idx1, neg, el)
    top2 = jnp.max(el2, axis=-1, keepdims=True)
    idx2 = jnp.min(jnp.where(in_group & (lane != idx1) & (el2 == top2), lane, LANES),
                   axis=-1, keepdims=True)
    e2 = jnp.exp(top2 - top1)
    w1 = group_p / (1.0 + e2)
    w2 = group_p * e2 / (1.0 + e2)
    route = jnp.where(lane == idx1, 1.0,
                      jnp.where(lane == idx2 + N_EXPERTS, 1.0,
                                jnp.where(lane == ROUTE_W_LANE, w1,
                                          jnp.where(lane == ROUTE_W_LANE + 1, w2, 0.0))))
    sel = selrows_ref[...]
    ln = lax.broadcasted_iota(jnp.int32, (MOE_TD, LANES), 1)
    for sub in range(tm // MOE_TD):
        tile = slice(sub * MOE_TD, (sub + 1) * MOE_TD)
        rt = route[tile]
        cnt_out[0, sub:sub + 1, :] = jnp.sum(rt, axis=0, keepdims=True).astype(jnp.int32)
        v = _local_slots(rt, lstrict_ref[...], fold_ref[...], upper_ref[...])
        hi = jnp.floor(v * (1.0 / MOE_ALIGN))
        lo = v - hi * MOE_ALIGN
        rows_out[0, sub * SUBLANES:(sub + 1) * SUBLANES, :] = (
            _dot_nt(sel, hi.astype(BF16)) * MOE_ALIGN + _dot_nt(sel, lo.astype(BF16)))
        r1 = jnp.sum(jnp.where(ln < N_EXPERTS, v, 0.0), axis=-1, keepdims=True)
        r2 = jnp.sum(jnp.where(ln < N_EXPERTS, 0.0, v), axis=-1, keepdims=True)
        route_out[0, tile, :] = jnp.where(ln == ROUTE_ROW_LANE, r1,
                                          jnp.where(ln == ROUTE_ROW_LANE + 1, r2, rt))


def _outproj(x, ao, ho, mod, wo, g2, wr, br, tm):
    b, s, d = x.shape
    sub = tm // MOE_TD
    const = lambda shape: pl.BlockSpec(shape, lambda bi, si: (0,) * len(shape))
    tok = lambda w: pl.BlockSpec((1, tm, w), lambda bi, si: (bi, si, 0))
    return pl.pallas_call(
        _outproj_kernel,
        grid=(b, s // tm),
        in_specs=[tok(d), tok(ATTN_WIDTH), tok(HGRN_WIDTH),
                  pl.BlockSpec((1, 6, d), lambda bi, si: (bi, 0, 0)),
                  const((d, d)), const((1, d)), const((2, d, LANES)), const((1, LANES)),
                  const((MOE_TD, MOE_TD)), const((LANES, LANES)), const((LANES, LANES)),
                  const((SUBLANES, LANES))],
        out_specs=[tok(d), tok(d), tok(LANES),
                   pl.BlockSpec((1, sub, LANES), lambda bi, si: (bi * (s // tm) + si, 0, 0)),
                   pl.BlockSpec((1, sub * SUBLANES, MOE_TD),
                                lambda bi, si: (bi * (s // tm) + si, 0, 0))],
        out_shape=[jax.ShapeDtypeStruct((b, s, d), F32),
                   jax.ShapeDtypeStruct((b, s, d), BF16),
                   jax.ShapeDtypeStruct((b, s, LANES), F32),
                   jax.ShapeDtypeStruct((b * s // tm, sub, LANES), jnp.int32),
                   jax.ShapeDtypeStruct((b * s // tm, sub * SUBLANES, MOE_TD), F32)],
        compiler_params=pltpu.CompilerParams(dimension_semantics=("arbitrary", "arbitrary"),
                                             vmem_limit_bytes=VMEM_LIMIT),
        name="outproj",
    )(x, ao, ho, mod, wo, g2, wr, br, *_moe_constants())


def _ceil_to(v, m):
    return ((v + (m - 1)) // m) * m


def _moe_constants():
    a = np.arange(LANES)
    ne = N_EXPERTS
    td = MOE_TD
    lstrict = np.tril(np.ones((td, td)), -1)
    fold = ((a[:, None] < 2 * ne) & (a[None, :] < 2 * ne) & (a[:, None] % ne == a[None, :] % ne))
    upper = ((a[:, None] < ne) & (a[None, :] < 2 * ne) & (a[:, None] < a[None, :] % ne))
    selrows = np.zeros((SUBLANES, LANES))
    selrows[0, :ne] = 1.0
    selrows[1, ne:2 * ne] = 1.0
    return tuple(jnp.asarray(m, BF16) for m in (lstrict, fold, upper, selrows))


def _local_slots(route, lstrict, fold, upper):
    lane = lax.broadcasted_iota(jnp.int32, route.shape, 1)
    member = jnp.where(lane < 2 * N_EXPERTS, route, 0.0)
    rank = _dot(_dot(lstrict, member.astype(BF16)).astype(BF16), fold)
    cnt = jnp.broadcast_to(jnp.sum(member, axis=0, keepdims=True), (SUBLANES, LANES))
    cnt = _dot(cnt.astype(BF16), fold)
    run = jnp.floor((cnt + (MOE_ALIGN - 1)) * (1.0 / MOE_ALIGN)) * MOE_ALIGN
    start = _dot(run.astype(BF16), upper)[0:1, :]
    return member * (start + rank)


def _plan_kernel(c_ref, base_ref, texp_ref, meta_ref, *, ntiles, n_row_tiles):
    ne = N_EXPERTS
    off = jnp.int32(0)
    for e in range(ne):
        def body(i, run, e=e, off=off):
            c = c_ref[i * 2 * ne + e] + c_ref[i * 2 * ne + ne + e]
            base_ref[i * ne + e] = off + run
            return run + _ceil_to(c, MOE_ALIGN)
        total = lax.fori_loop(0, ntiles, body, jnp.int32(0))
        nt = (total + (MOE_TM - 1)) // MOE_TM
        first = off // MOE_TM

        def fill(j, carry, e=e, first=first):
            texp_ref[first + j] = e
            return carry
        lax.fori_loop(0, nt, fill, 0)
        off = off + nt * MOE_TM
    nvalid = off // MOE_TM

    def fill_rest(j, carry):
        texp_ref[j] = ne - 1
        return carry
    lax.fori_loop(nvalid, n_row_tiles, fill_rest, 0)
    meta_ref[0] = nvalid


def _plan(counts_flat, ntiles, n_row_tiles):
    smem = pl.BlockSpec(memory_space=pltpu.SMEM)
    return pl.pallas_call(
        functools.partial(_plan_kernel, ntiles=ntiles, n_row_tiles=n_row_tiles),
        in_specs=[smem],
        out_specs=[smem, smem, smem],
        out_shape=[jax.ShapeDtypeStruct((ntiles * N_EXPERTS,), jnp.int32),
                   jax.ShapeDtypeStruct((n_row_tiles,), jnp.int32),
                   jax.ShapeDtypeStruct((1,), jnp.int32)],
        name="moe_plan",
    )(counts_flat)


def _run_chunks(c_ref, base_ref, tile, fn):
    ne = N_EXPERTS
    local = jnp.int32(0)
    for e in range(ne):
        c = c_ref[tile * 2 * ne + e] + c_ref[tile * 2 * ne + ne + e]
        nchunk = (c + (MOE_ALIGN - 1)) // MOE_ALIGN
        hbm = base_ref[tile * ne + e]

        def body(j, carry, hbm=hbm, local=local):
            fn(pl.multiple_of(hbm + j * MOE_ALIGN, MOE_ALIGN),
               pl.multiple_of(local + j * MOE_ALIGN, MOE_ALIGN))
            return carry
        lax.fori_loop(0, nchunk, body, 0)
        local = local + nchunk * MOE_ALIGN
    return local // MOE_ALIGN


def _dispatch_kernel(c_ref, base_ref, h2_ref, route_ref, rows_ref, xs_init, ws_init,
                     xs_ref, ws_ref, xs_scr, ws_scr, sem, nch_ref):
    del xs_init, ws_init
    i = pl.program_id(0)
    n = pl.num_programs(0)
    slot = i % 2

    def copies(slot_, hbm, local):
        return (pltpu.make_async_copy(xs_scr.at[slot_, pl.ds(local, MOE_ALIGN), :],
                                      xs_ref.at[pl.ds(hbm, MOE_ALIGN), :], sem.at[0, slot_]),
                pltpu.make_async_copy(ws_scr.at[slot_, pl.ds(local, MOE_ALIGN), :],
                                      ws_ref.at[pl.ds(hbm, MOE_ALIGN), :], sem.at[1, slot_]))

    def wait_all(slot_):
        def body(j, carry):
            for cp in copies(slot_, 0, 0):
                cp.wait()
            return carry
        lax.fori_loop(0, nch_ref[slot_], body, 0)

    @pl.when(i >= 2)
    def _():
        wait_all(slot)

    route = route_ref[...]
    lane = lax.broadcasted_iota(jnp.int32, route.shape, 1)
    rows = rows_ref[0]
    slot_id = lax.broadcasted_iota(jnp.int32, (MOE_L, MOE_TD), 0)
    p1 = jnp.where(slot_id == rows[0:1, :].astype(jnp.int32), 1.0, 0.0)
    p2 = jnp.where(slot_id == rows[1:2, :].astype(jnp.int32), 1.0, 0.0)
    xs_scr[slot] = _dot((p1 + p2).astype(BF16), h2_ref[...]).astype(BF16)
    w = []
    for k in range(2):
        col = jnp.sum(jnp.where(lane == ROUTE_W_LANE + k, route, 0.0), axis=-1, keepdims=True)
        wrep = jnp.broadcast_to(col, route.shape)
        whi = wrep.astype(BF16)
        w.append((whi, (wrep - whi.astype(F32)).astype(BF16)))
    p1b, p2b = p1.astype(BF16), p2.astype(BF16)
    ws_scr[slot] = (_dot(p1b, w[0][0]) + _dot(p1b, w[0][1])) + (_dot(p2b, w[1][0]) + _dot(p2b, w[1][1]))

    def start(hbm, local):
        for cp in copies(slot, hbm, local):
            cp.start()
    nch_ref[slot] = _run_chunks(c_ref, base_ref, i, start)

    @pl.when(i == n - 1)
    def _():
        wait_all(slot)

        @pl.when(n >= 2)
        def _():
            wait_all(1 - slot)


def _dispatch(counts_flat, base, h2, route, rows, n_rows):
    t, d = h2.shape
    ntiles = t // MOE_TD
    tok = lambda w: pl.BlockSpec((MOE_TD, w), lambda i, c, b: (i, 0))
    any_spec = pl.BlockSpec(memory_space=pl.ANY)
    grid_spec = pltpu.PrefetchScalarGridSpec(
        num_scalar_prefetch=2,
        grid=(ntiles,),
        in_specs=[tok(d), tok(LANES),
                  pl.BlockSpec((1, SUBLANES, MOE_TD), lambda i, c, b: (i, 0, 0)),
                  any_spec, any_spec],
        out_specs=[any_spec, any_spec],
        scratch_shapes=[pltpu.VMEM((2, MOE_L, d), BF16), pltpu.VMEM((2, MOE_L, LANES), F32),
                        pltpu.SemaphoreType.DMA((2, 2)), pltpu.SMEM((2,), jnp.int32)])
    return pl.pallas_call(
        _dispatch_kernel,
        grid_spec=grid_spec,
        out_shape=[jax.ShapeDtypeStruct((n_rows, d), BF16),
                   jax.ShapeDtypeStruct((n_rows, LANES), F32)],
        input_output_aliases={5: 0, 6: 1},
        compiler_params=pltpu.CompilerParams(dimension_semantics=("arbitrary",),
                                             vmem_limit_bytes=VMEM_LIMIT),
        name="moe_dispatch",
    )(counts_flat, base, h2, route, rows,
      jnp.zeros((n_rows, d), BF16), jnp.zeros((n_rows, LANES), F32))


def _experts_kernel(texp_ref, meta_ref, xs_ref, ws_ref, wg_ref, wu_ref, wd_ref, ys_ref):
    del texp_ref

    @pl.when(pl.program_id(0) >= meta_ref[0])
    def _():
        ys_ref[...] = jnp.zeros_like(ys_ref)

    @pl.when(pl.program_id(0) < meta_ref[0])
    def _():
        x = xs_ref[...]
        act = _silu(_dot(x, wg_ref[0])) * _dot(x, wu_ref[0])
        y = _dot(act.astype(BF16), wd_ref[0])
        w = ws_ref[...]
        ys_ref[...] = (y * jnp.concatenate([w] * (y.shape[1] // LANES), axis=1)).astype(BF16)


def _experts(texp, meta, xs, ws, wg, wu, wd):
    n_rows, d = xs.shape
    row = lambda w: pl.BlockSpec((MOE_TM, w), lambda i, te, mt: (jnp.minimum(i, mt[0] - 1), 0))
    wspec = lambda shape: pl.BlockSpec((1,) + shape, lambda i, te, mt: (te[i], 0, 0))
    grid_spec = pltpu.PrefetchScalarGridSpec(
        num_scalar_prefetch=2,
        grid=(n_rows // MOE_TM,),
        in_specs=[row(d), row(LANES), wspec((d, D_EXPERT)), wspec((d, D_EXPERT)),
                  wspec((D_EXPERT, d))],
        out_specs=pl.BlockSpec((MOE_TM, d), lambda i, te, mt: (i, 0)))
    return pl.pallas_call(
        _experts_kernel,
        grid_spec=grid_spec,
        out_shape=jax.ShapeDtypeStruct((n_rows, d), BF16),
        compiler_params=pltpu.CompilerParams(dimension_semantics=("arbitrary",),
                                             vmem_limit_bytes=VMEM_LIMIT),
        name="moe_experts",
    )(texp, meta, xs, ws, wg, wu, wd)


def _combine_kernel(c_ref, base_ref, x1_ref, route_ref, mod_ref, ys_ref, o_ref, ys_scr, sem,
                    nch_ref):
    i = pl.program_id(0)
    n = pl.num_programs(0)
    slot = i % 2

    def copy(slot_, hbm, local):
        return pltpu.make_async_copy(ys_ref.at[pl.ds(hbm, MOE_ALIGN), :],
                                     ys_scr.at[slot_, pl.ds(local, MOE_ALIGN), :], sem.at[slot_])

    def fetch(tile, slot_):
        nch_ref[slot_] = _run_chunks(c_ref, base_ref, tile,
                                     lambda hbm, local: copy(slot_, hbm, local).start())

    @pl.when(i == 0)
    def _():
        ys_scr[...] = jnp.zeros_like(ys_scr)
        fetch(0, 0)

    @pl.when(i + 1 < n)
    def _():
        fetch(i + 1, 1 - slot)

    route = route_ref[...]
    lane = lax.broadcasted_iota(jnp.int32, route.shape, 1)
    r1 = jnp.sum(jnp.where(lane == ROUTE_ROW_LANE, route, 0.0), axis=-1, keepdims=True)
    r2 = jnp.sum(jnp.where(lane == ROUTE_ROW_LANE + 1, route, 0.0), axis=-1, keepdims=True)
    r1, r2 = r1.astype(jnp.int32), r2.astype(jnp.int32)
    col = lax.broadcasted_iota(jnp.int32, (MOE_TD, MOE_L), 1)
    pick = (jnp.where(col == r1, 1.0, 0.0) + jnp.where(col == r2, 1.0, 0.0)).astype(BF16)

    def wait(j, carry):
        copy(slot, 0, 0).wait()
        return carry
    lax.fori_loop(0, nch_ref[slot], wait, 0)
    y = _dot(pick, ys_scr[slot])
    o_ref[...] = x1_ref[...] + mod_ref[0, 5:6, :] * y


def _combine(counts_flat, base, x1, route, mod, ys, tiles_per_batch):
    t, d = x1.shape
    tok = lambda w: pl.BlockSpec((MOE_TD, w), lambda i, c, b: (i, 0))
    grid_spec = pltpu.PrefetchScalarGridSpec(
        num_scalar_prefetch=2,
        grid=(t // MOE_TD,),
        in_specs=[tok(d), tok(LANES),
                  pl.BlockSpec((1, 6, d), lambda i, c, b: (i // tiles_per_batch, 0, 0)),
                  pl.BlockSpec(memory_space=pl.ANY)],
        out_specs=tok(d),
        scratch_shapes=[pltpu.VMEM((2, MOE_L, d), BF16), pltpu.SemaphoreType.DMA((2,)),
                        pltpu.SMEM((2,), jnp.int32)])
    return pl.pallas_call(
        _combine_kernel,
        grid_spec=grid_spec,
        out_shape=jax.ShapeDtypeStruct((t, d), F32),
        compiler_params=pltpu.CompilerParams(dimension_semantics=("arbitrary",),
                                             vmem_limit_bytes=VMEM_LIMIT),
        name="moe_combine",
    )(counts_flat, base, x1, route, mod, ys)


def kernel(x, c, w_ada, b_ada, norm1_g, w_in, b_fox, q_norm_g, k_norm_g, attn_out_g, hgrn_lb,
           hgrn_out_g, w_out, norm2_g, w_router_group, b_router_group, w_router_expert,
           b_router_expert, w_gate, w_up, w_down):
    b, s, d = x.shape
    l = 0
    aw = ATTN_WIDTH
    mod = _ada(c, w_ada[l], b_ada[l]).reshape(b, 6, d)

    w = w_in[l]
    wq = w[:, 0:aw].astype(BF16)
    wk = w[:, aw:2 * aw].astype(BF16)
    wv = w[:, 2 * aw:3 * aw].astype(BF16)
    f0 = 3 * aw
    pad_f = LANES - FOX_PIECES * ATTN_HEADS
    wf = jnp.pad(jnp.tile(w[:, f0:f0 + ATTN_HEADS], (1, FOX_PIECES)), ((0, 0), (0, pad_f))).astype(BF16)
    wh = w[:, f0 + ATTN_HEADS:].astype(BF16)
    bfox = jnp.pad(jnp.tile(b_fox[l], FOX_PIECES), (0, pad_f)).reshape(1, LANES)
    gq = jnp.tile(q_norm_g[l], ATTN_HEADS).reshape(1, aw)
    gk = jnp.tile(k_norm_g[l], ATTN_HEADS).reshape(1, aw)

    tm = min(512, s)
    q, k, v, ck, hq, hf, hi, hg = _inproj(x, mod, norm1_g[l].reshape(1, d), wq, wk, wv, wf, wh,
                                           gq, gk, bfox, tm)
    ao = _fox(q, k, v, ck, attn_out_g[l].reshape(1, aw), min(256, s), 128)
    ho = _hgrn(hq, hf, hi, hg, hgrn_lb[0:2], hgrn_out_g[l].reshape(1, HGRN_WIDTH))

    wr = jnp.pad(jnp.concatenate([w_router_expert[l], w_router_group[l]], axis=1),
                 ((0, 0), (0, LANES - N_GROUPS - N_EXPERTS)))
    br = jnp.pad(jnp.concatenate([b_router_expert[l], b_router_group[l]]),
                 (0, LANES - N_GROUPS - N_EXPERTS)).reshape(1, LANES)
    wr_hi = wr.astype(BF16)
    wr2 = jnp.stack([wr_hi, (wr - wr_hi.astype(F32)).astype(BF16)])
    x1, h2, route, counts, rows = _outproj(x, ao, ho, mod, w_out[l].astype(BF16),
                                           norm2_g[l].reshape(1, d), wr2, br, tm)

    t = b * s
    ntiles = t // MOE_TD
    counts_flat = counts.reshape(ntiles, LANES)[:, :2 * N_EXPERTS].reshape(-1)
    n_rows = _ceil_to(2 * t + ntiles * N_EXPERTS * (MOE_ALIGN - 1), MOE_TM) + N_EXPERTS * MOE_TM
    base, texp, meta = _plan(counts_flat, ntiles, n_rows // MOE_TM)
    route2 = route.reshape(t, LANES)
    xs, ws = _dispatch(counts_flat, base, h2.reshape(t, d), route2,
                       rows.reshape(ntiles, SUBLANES, MOE_TD), n_rows)
    ys = _experts(texp, meta, xs, ws, w_gate[l].astype(BF16), w_up[l].astype(BF16),
                  w_down[l].astype(BF16))
    out = _combine(counts_flat, base, x1.reshape(t, d), route2, mod, ys, s // MOE_TD)
    return out.reshape(b, s, d)
```

```python
import functools

import numpy as np
import jax
import jax.numpy as jnp
from jax import lax
from jax.experimental import pallas as pl
from jax.experimental.pallas import tpu as pltpu

F32 = jnp.float32
BF16 = jnp.bfloat16

D_MODEL = 1024
ATTN_HEAD_DIM = 64
ATTN_WIDTH = 512
ATTN_HEADS = 8
HGRN_WIDTH = 512
HGRN_HEADS = 4
HGRN_DK = 128
N_GROUPS = 4
EXPERTS_PER_GROUP = 4
N_EXPERTS = 16
D_EXPERT = 512
NORM_EPS = 1e-6
LANES = 128
SUBLANES = 8
VMEM_LIMIT = 56 * 1024 * 1024

HGRN_CHUNK = 128
HGRN_LEVELS = (8, 16, 32, 64)
ROUTE_W_LANE = 2 * N_EXPERTS
ROUTE_ROW_LANE = ROUTE_W_LANE + 2
MOE_TD = 256
MOE_ALIGN = 16
MOE_TM = 512
MOE_L = 2 * MOE_TD + MOE_ALIGN * N_EXPERTS
LOG2E = 1.4426950408889634
FOX_GROUP = 4
FOX_PIECES = 3


def _sigmoid(x):
    return 1.0 / (1.0 + jnp.exp(-x))


def _silu(x):
    return x * _sigmoid(x)


def _split3(x):
    p1 = x.astype(BF16)
    r1 = x - p1.astype(F32)
    p2 = r1.astype(BF16)
    p3 = (r1 - p2.astype(F32)).astype(BF16)
    return p1, p2, p3


def _dot(a, b):
    return jnp.dot(a, b, preferred_element_type=F32)


def _dot_nt(a, b):
    return lax.dot_general(a, b, (((1,), (1,)), ((), ())), preferred_element_type=F32)


def _ada_kernel(c_ref, w_ref, b_ref, o_ref):
    c = c_ref[...]
    o_ref[...] = jnp.dot(_silu(c), w_ref[...], preferred_element_type=F32,
                         precision=lax.Precision.HIGHEST) + b_ref[...]


def _ada(c, w, b):
    bsz, d = c.shape
    n = w.shape[1]
    tn = 1024
    return pl.pallas_call(
        _ada_kernel,
        grid=(n // tn,),
        in_specs=[pl.BlockSpec((bsz, d), lambda j: (0, 0)),
                  pl.BlockSpec((d, tn), lambda j: (0, j)),
                  pl.BlockSpec((1, tn), lambda j: (0, j))],
        out_specs=pl.BlockSpec((bsz, tn), lambda j: (0, j)),
        out_shape=jax.ShapeDtypeStruct((bsz, n), F32),
        compiler_params=pltpu.CompilerParams(dimension_semantics=("arbitrary",),
                                             vmem_limit_bytes=VMEM_LIMIT),
        name="ada",
    )(c, w, b.reshape(1, n))


def _inproj_kernel(x_ref, mod_ref, g1_ref, wq_ref, wk_ref, wv_ref, wf_ref, wh_ref,
                   gq_ref, gk_ref, bf_ref, gsum_ref, tri_ref, place_ref,
                   q_out, k_out, v_out, ck_out, hq_out, hf_out, hi_out, hg_out,
                   carry_ref):
    si = pl.program_id(1)

    @pl.when(si == 0)
    def _():
        carry_ref[...] = jnp.zeros_like(carry_ref)

    x = x_ref[0]
    shift = mod_ref[0, 0:1, :]
    scale = mod_ref[0, 1:2, :]
    ms = jnp.mean(x * x, axis=-1, keepdims=True)
    h = (x * lax.rsqrt(ms + NORM_EPS) * g1_ref[...]) * (1.0 + scale) + shift
    hb = h.astype(BF16)

    def qk_norm(w_ref, g_ref, mult):
        t = _dot(hb, w_ref[...])
        ssq = _dot((t * t).astype(BF16), gsum_ref[...])
        return t * lax.rsqrt(ssq * (1.0 / ATTN_HEAD_DIM) + NORM_EPS) * (g_ref[...] * mult)

    q_out[0] = qk_norm(wq_ref, gq_ref, ATTN_HEAD_DIM ** -0.5 * LOG2E).T.astype(BF16)
    k_out[0] = qk_norm(wk_ref, gk_ref, 1.0).astype(BF16)
    v_out[0] = _dot(hb, wv_ref[...]).T.astype(BF16)

    af = _dot(hb, wf_ref[...]) + bf_ref[...]
    lf = jnp.minimum(af, 0.0) - jnp.log(1.0 + jnp.exp(-jnp.abs(af)))
    tri = tri_ref[...]
    p1, p2, p3 = _split3(lf)
    cum = (_dot(tri, p1) + _dot(tri, p2)) + _dot(tri, p3) + carry_ref[...]
    tm = cum.shape[0]
    carry_ref[...] = cum[tm - 1:tm, :]
    c1, c2, c3 = _split3(cum * (-LOG2E))
    lane = lax.broadcasted_iota(jnp.int32, cum.shape, 1)
    zero = jnp.zeros_like(c1)
    pieces = jnp.where(lane < ATTN_HEADS, c1,
                       jnp.where(lane < 2 * ATTN_HEADS, c2,
                                 jnp.where(lane < 3 * ATTN_HEADS, c3, zero)))
    ck_out[0] = _dot(pieces, place_ref[...]).astype(BF16)

    hq_out[0] = _dot(hb, wh_ref[:, 0 * HGRN_WIDTH:1 * HGRN_WIDTH]).astype(BF16)
    hf_out[0] = _dot(hb, wh_ref[:, 1 * HGRN_WIDTH:2 * HGRN_WIDTH]).astype(BF16)
    hi_out[0] = _dot(hb, wh_ref[:, 2 * HGRN_WIDTH:3 * HGRN_WIDTH]).astype(BF16)
    hg_out[0] = _dot(hb, wh_ref[:, 3 * HGRN_WIDTH:4 * HGRN_WIDTH]).astype(BF16)


def _inproj(x, mod, g1, wq, wk, wv, wf, wh, gq, gk, bfox, tm):
    b, s, d = x.shape
    gsum = jnp.asarray(np.kron(np.eye(ATTN_HEADS), np.ones((ATTN_HEAD_DIM, ATTN_HEAD_DIM))), BF16)
    tri = jnp.asarray(np.tril(np.ones((tm, tm))), BF16)
    place_np = np.zeros((LANES, ATTN_HEADS * LANES), np.float32)
    for piece in range(FOX_PIECES):
        for hd in range(ATTN_HEADS):
            place_np[piece * ATTN_HEADS + hd, hd * LANES + piece] = 1.0
    place = jnp.asarray(place_np, BF16)
    const = lambda shape: pl.BlockSpec(shape, lambda bi, si: (0,) * len(shape))
    tok = lambda w: pl.BlockSpec((1, tm, w), lambda bi, si: (bi, si, 0))
    tok_t = lambda w: pl.BlockSpec((1, w, tm), lambda bi, si: (bi, 0, si))
    act = lambda w: jax.ShapeDtypeStruct((b, s, w), BF16)
    act_t = lambda w: jax.ShapeDtypeStruct((b, w, s), BF16)
    return pl.pallas_call(
        _inproj_kernel,
        grid=(b, s // tm),
        in_specs=[tok(d),
                  pl.BlockSpec((1, 6, d), lambda bi, si: (bi, 0, 0)),
                  const((1, d)),
                  const((d, ATTN_WIDTH)), const((d, ATTN_WIDTH)), const((d, ATTN_WIDTH)),
                  const((d, LANES)), const((d, 4 * HGRN_WIDTH)),
                  const((1, ATTN_WIDTH)), const((1, ATTN_WIDTH)), const((1, LANES)),
                  const((ATTN_WIDTH, ATTN_WIDTH)), const((tm, tm)),
                  const((LANES, ATTN_HEADS * LANES))],
        out_specs=[tok_t(ATTN_WIDTH), tok(ATTN_WIDTH), tok_t(ATTN_WIDTH),
                   tok(ATTN_HEADS * LANES),
                   tok(HGRN_WIDTH), tok(HGRN_WIDTH), tok(HGRN_WIDTH), tok(HGRN_WIDTH)],
        out_shape=[act_t(ATTN_WIDTH), act(ATTN_WIDTH), act_t(ATTN_WIDTH),
                   act(ATTN_HEADS * LANES),
                   act(HGRN_WIDTH), act(HGRN_WIDTH), act(HGRN_WIDTH), act(HGRN_WIDTH)],
        scratch_shapes=[pltpu.VMEM((1, LANES), F32)],
        compiler_params=pltpu.CompilerParams(dimension_semantics=("arbitrary", "arbitrary"),
                                             vmem_limit_bytes=VMEM_LIMIT),
        name="inproj",
    )(x, mod, g1, wq, wk, wv, wf, wh, gq, gk, bfox, gsum, tri, place)


def _fox_kernel(qt_ref, k_ref, vt_ref, ck_ref, g_ref, o_ref, st_scr, pt_scr, *, tq, tk):
    qi = pl.program_id(2)
    qt = qt_ref[0].astype(F32)
    row = lax.broadcasted_iota(jnp.int32, (LANES, tq), 0)
    first = row < ATTN_HEAD_DIM
    ones = jnp.where(row < FOX_PIECES, 1.0, 0.0)
    rhs = []
    for hd in range(FOX_GROUP):
        qp = qt[(hd // 2) * LANES:(hd // 2 + 1) * LANES, :]
        qh = jnp.where(first, qp, 0.0) if hd % 2 == 0 else jnp.where(first, 0.0, qp)
        rhs.append(jnp.concatenate([qh, ones], axis=0).astype(BF16))

    def stage_qk(j, slot):
        k0 = pl.multiple_of(j * tk, tk)
        kb = k_ref[0, pl.ds(k0, tk), :]
        ckb = ck_ref[0, pl.ds(k0, tk), :]
        for hd in range(FOX_GROUP):
            pr = slice((hd // 2) * LANES, (hd // 2 + 1) * LANES)
            lhs = jnp.concatenate([kb[:, pr], ckb[:, hd * LANES:(hd + 1) * LANES]], axis=1)
            st_scr[slot, hd] = _dot(lhs, rhs[hd])

    def stage_softmax(j, slot, stats, masked):
        out = []
        for hd in range(FOX_GROUP):
            m, l = stats[hd]
            st = st_scr[slot, hd]
            if masked:
                key = j * tk + lax.broadcasted_iota(jnp.int32, (tk, tq), 0)
                qry = qi * tq + lax.broadcasted_iota(jnp.int32, (tk, tq), 1)
                st = jnp.where(key <= qry, st, -jnp.inf)
            m_new = jnp.maximum(m, jnp.max(st, axis=0, keepdims=True))
            alpha = jnp.exp2(m - m_new)
            pt = jnp.exp2(st - m_new)
            pt_scr[slot, hd] = pt.astype(BF16)
            out.append(((m_new, alpha * l + jnp.sum(pt, axis=0, keepdims=True)), alpha))
        return tuple(o[0] for o in out), tuple(o[1] for o in out)

    def stage_pv(j, slot, accs, alphas):
        k0 = pl.multiple_of(jnp.maximum(j, 0) * tk, tk)
        vtb = vt_ref[0, :, pl.ds(k0, tk)]
        out = []
        for hd in range(FOX_GROUP):
            vth = vtb[hd * ATTN_HEAD_DIM:(hd + 1) * ATTN_HEAD_DIM, :]
            out.append(alphas[hd] * accs[hd] + _dot(vth, pt_scr[slot, hd]))
        return tuple(out)

    per_q = tq // tk
    stats = tuple((jnp.full((1, tq), -1e30, F32), jnp.zeros((1, tq), F32))
                  for _ in range(FOX_GROUP))
    accs = tuple(jnp.zeros((ATTN_HEAD_DIM, tq), F32) for _ in range(FOX_GROUP))
    alphas = tuple(jnp.ones((1, tq), F32) for _ in range(FOX_GROUP))
    pt_scr[1] = jnp.zeros_like(pt_scr[1])
    stage_qk(0, 0)

    def full_blocks(i, carry):
        stats, accs, alphas = carry
        for u in range(per_q):
            j = i * per_q + u
            accs = stage_pv(j - 1, 1 - u, accs, alphas)
            stats, alphas = stage_softmax(j, u, stats, False)
            stage_qk(j + 1, 1 - u)
        return stats, accs, alphas

    stats, accs, alphas = lax.fori_loop(0, qi, full_blocks, (stats, accs, alphas))
    j = qi * per_q
    accs = stage_pv(j - 1, 1, accs, alphas)
    stats, alphas = stage_softmax(j, 0, stats, True)
    stage_qk(j + 1, 1)
    accs = stage_pv(j, 0, accs, alphas)
    stats, alphas = stage_softmax(j + 1, 1, stats, True)
    accs = stage_pv(j + 1, 1, accs, alphas)
    carry = tuple((stats[hd][0], stats[hd][1], accs[hd]) for hd in range(FOX_GROUP))

    head0 = lax.broadcasted_iota(jnp.int32, (tq, LANES), 1) < ATTN_HEAD_DIM
    for pr in range(FOX_GROUP // 2):
        (_, l0, a0), (_, l1, a1) = carry[2 * pr], carry[2 * pr + 1]
        ot = jnp.concatenate([a0 * (1.0 / l0), a1 * (1.0 / l1)], axis=0)
        o = ot.T
        osq = o * o
        ss0 = jnp.sum(jnp.where(head0, osq, 0.0), axis=-1, keepdims=True)
        ss1 = jnp.sum(jnp.where(head0, 0.0, osq), axis=-1, keepdims=True)
        ms = jnp.where(head0, ss0, ss1) * (1.0 / ATTN_HEAD_DIM)
        sl = slice(pr * LANES, (pr + 1) * LANES)
        o_ref[0, :, sl] = (o * lax.rsqrt(ms + NORM_EPS) * g_ref[:, sl]).astype(BF16)


def _fox(qt, k, vt, ck, g_out, tq, tk):
    b, s, _ = k.shape
    groups = ATTN_HEADS // FOX_GROUP
    gw = FOX_GROUP * ATTN_HEAD_DIM
    return pl.pallas_call(
        functools.partial(_fox_kernel, tq=tq, tk=tk),
        grid=(b, groups, s // tq),
        in_specs=[pl.BlockSpec((1, gw, tq), lambda bi, g, qi: (bi, g, qi)),
                  pl.BlockSpec((1, s, gw), lambda bi, g, qi: (bi, 0, g)),
                  pl.BlockSpec((1, gw, s), lambda bi, g, qi: (bi, g, 0)),
                  pl.BlockSpec((1, s, FOX_GROUP * LANES), lambda bi, g, qi: (bi, 0, g)),
                  pl.BlockSpec((1, gw), lambda bi, g, qi: (0, g))],
        out_specs=pl.BlockSpec((1, tq, gw), lambda bi, g, qi: (bi, qi, g)),
        out_shape=jax.ShapeDtypeStruct((b, s, ATTN_WIDTH), BF16),
        scratch_shapes=[pltpu.VMEM((2, FOX_GROUP, tk, tq), F32),
                        pltpu.VMEM((2, FOX_GROUP, tk, tq), BF16)],
        compiler_params=pltpu.CompilerParams(
            dimension_semantics=("arbitrary", "arbitrary", "arbitrary"),
            vmem_limit_bytes=VMEM_LIMIT),
        name="fox",
    )(qt, k, vt, ck, g_out)


def _hgrn_decay_matrix(c):
    t = np.arange(c)[:, None]
    j = np.arange(c)[None, :]
    blocks = [(j <= t), (j > t)]
    for m in HGRN_LEVELS:
        mid = (t // (2 * m)) * (2 * m) + m
        right = (t % (2 * m)) >= m
        blocks.append(np.where(right, (j >= mid) & (j <= t), (j > t) & (j < mid)))
    return np.concatenate(blocks, axis=0).astype(np.float32)


def _hgrn_kernel(hq_ref, hf_ref, hi_ref, hg_ref, lb_ref, g_ref, w_ref, lvl_ref, o_ref, st_ref):
    ci = pl.program_id(1)
    c = HGRN_CHUNK

    @pl.when(ci == 0)
    def _():
        st_ref[...] = jnp.zeros_like(st_ref)

    r0 = lb_ref[0:1, :]
    r1 = lb_ref[1:2, :]
    rmax = jnp.maximum(r0, r1)
    e0 = jnp.exp(r0 - rmax)
    lb = e0 / (e0 + jnp.exp(r1 - rmax))

    f = lb + (1.0 - lb) * _sigmoid(hf_ref[0].astype(F32))
    g = jnp.log(f)
    g1, g2, g3 = _split3(g)
    w = w_ref[...]
    xall = (_dot(w, g1) + _dot(w, g2)) + _dot(w, g3)
    eall = jnp.exp(xall)
    q_all = _silu(hq_ref[0].astype(F32))
    k_all = 1.0 - f
    v_all = hi_ref[0].astype(F32)

    row = lax.broadcasted_iota(jnp.int32, (c, HGRN_DK), 0)
    rowmod = row % SUBLANES
    def roll8(x, shift):
        x3 = x.reshape(c // SUBLANES, SUBLANES, HGRN_DK)
        return pltpu.roll(x3, shift, 1).reshape(c, HGRN_DK)

    lvl = lvl_ref[...]
    rights = [(row % (2 * m)) >= m for m in HGRN_LEVELS]

    for hd in range(HGRN_HEADS):
        sl = slice(hd * HGRN_DK, (hd + 1) * HGRN_DK)
        q, k, v, fh = q_all[:, sl], k_all[:, sl], v_all[:, sl], f[:, sl]
        e_pre = eall[0:c, sl]
        e_suf = eall[c:2 * c, sl]

        a = jnp.zeros((c, c), F32)
        for li, m in enumerate(HGRN_LEVELS):
            e = eall[(2 + li) * c:(3 + li) * c, sl]
            right = rights[li]
            qt =jnp.where(right, q * e, 0.0).astype(BF16)
            kt = jnp.where(right, 0.0, k * e).astype(BF16)
            a = a + jnp.where(lvl == float(li), _dot_nt(qt, kt), 0.0)
        vb = v.astype(BF16)
        out = _dot(a.astype(BF16), vb)

        out = out + jnp.sum(q * k, axis=-1, keepdims=True) * v
        prod = fh
        for dd in range(1, SUBLANES):
            kd = roll8(k, dd)
            vd = roll8(v, dd)
            coef = jnp.sum(q * prod * kd, axis=-1, keepdims=True)
            out = out + jnp.where(rowmod >= dd, coef, 0.0) * vd
            if dd + 1 < SUBLANES:
                prod = prod * roll8(fh, dd)

        st = st_ref[hd]
        out = out + _dot_nt((q * e_pre).astype(BF16), st.astype(BF16))
        kdec = (k * e_suf).astype(BF16)
        upd = lax.dot_general(vb, kdec, (((0,), (0,)), ((), ())), preferred_element_type=F32)
        st_ref[hd] = st * e_pre[c - 1:c, :] + upd

        ms = jnp.mean(out * out, axis=-1, keepdims=True)
        gate = _silu(hg_ref[0, :, sl].astype(F32))
        o_ref[0, :, sl] = (out * lax.rsqrt(ms + NORM_EPS) * g_ref[:, sl] * gate).astype(BF16)


def _hgrn(hq, hf, hi, hg, lb_rows, g_out):
    b, s, wd = hq.shape
    c = HGRN_CHUNK
    wmat = jnp.asarray(_hgrn_decay_matrix(c), BF16)
    tt = np.arange(c)[:, None]
    ss = np.arange(c)[None, :]
    lvl_np = np.full((c, c), -1.0, np.float32)
    for li, m in reversed(list(enumerate(HGRN_LEVELS))):
        lvl_np[(ss < tt) & (tt // (2 * m) == ss // (2 * m)) & (tt // m != ss // m)] = li
    lvl = jnp.asarray(lvl_np)
    tok = pl.BlockSpec((1, c, wd), lambda bi, ci: (bi, ci, 0))
    return pl.pallas_call(
        _hgrn_kernel,
        grid=(b, s // c),
        in_specs=[tok, tok, tok, tok,
                  pl.BlockSpec((2, wd), lambda bi, ci: (0, 0)),
                  pl.BlockSpec((1, wd), lambda bi, ci: (0, 0)),
                  pl.BlockSpec(wmat.shape, lambda bi, ci: (0, 0)),
                  pl.BlockSpec((c, c), lambda bi, ci: (0, 0))],
        out_specs=tok,
        out_shape=jax.ShapeDtypeStruct((b, s, wd), BF16),
        scratch_shapes=[pltpu.VMEM((HGRN_HEADS, HGRN_DK, HGRN_DK), F32)],
        compiler_params=pltpu.CompilerParams(dimension_semantics=("arbitrary", "arbitrary"),
                                             vmem_limit_bytes=VMEM_LIMIT),
        name="hgrn",
    )(hq, hf, hi, hg, lb_rows, g_out, wmat, lvl)


def _outproj_kernel(x_ref, ao_ref, ho_ref, mod_ref, wo_ref, g2_ref, wr_ref, br_ref,
                    lstrict_ref, fold_ref, upper_ref, selrows_ref,
                    x1_out, h2_out, route_out, cnt_out, rows_out):
    gate1 = mod_ref[0, 2:3, :]
    shift2 = mod_ref[0, 3:4, :]
    scale2 = mod_ref[0, 4:5, :]
    mix = _dot(ao_ref[0], wo_ref[0:ATTN_WIDTH, :]) + _dot(ho_ref[0], wo_ref[ATTN_WIDTH:D_MODEL, :])
    x1 = x_ref[0] + gate1 * mix
    x1_out[0] = x1
    ms = jnp.mean(x1 * x1, axis=-1, keepdims=True)
    h2 = (x1 * lax.rsqrt(ms + NORM_EPS) * g2_ref[...]) * (1.0 + scale2) + shift2
    h2_out[0] = h2.astype(BF16)

    h2_hi = h2.astype(BF16)
    h2_lo = (h2 - h2_hi.astype(F32)).astype(BF16)
    logits = (_dot(h2_hi, wr_ref[0]) + (_dot(h2_hi, wr_ref[1]) + _dot(h2_lo, wr_ref[0]))) + br_ref[...]
    tm = logits.shape[0]
    lane = lax.broadcasted_iota(jnp.int32, (tm, LANES), 1)
    neg = -jnp.inf
    is_group = (lane >= N_EXPERTS) & (lane < N_EXPERTS + N_GROUPS)
    gl = jnp.where(is_group, logits, neg)
    gmax = jnp.max(gl, axis=-1, keepdims=True)
    gsum = jnp.sum(jnp.exp(gl - gmax), axis=-1, keepdims=True)
    group_p = 1.0 / gsum
    gidx = jnp.min(jnp.where(is_group & (gl == gmax), lane, LANES), axis=-1, keepdims=True) - N_EXPERTS
    in_group = (lane < N_EXPERTS) & ((lane // EXPERTS_PER_GROUP) == gidx)
    el = jnp.where(in_group, logits, neg)
    top1 = jnp.max(el, axis=-1, keepdims=True)
    idx1 = jnp.min(jnp.where(in_group & (el == top1), lane, LANES), axis=-1, keepdims=True)
    el2 = jnp.where(lane == idx1, neg, el)
    top2 = jnp.max(el2, axis=-1, keepdims=True)
    idx2 = jnp.min(jnp.where(in_group & (lane != idx1) & (el2 == top2), lane, LANES),
                   axis=-1, keepdims=True)
    e2 = jnp.exp(top2 - top1)
    w1 = group_p / (1.0 + e2)
    w2 = group_p * e2 / (1.0 + e2)
    route = jnp.where(lane == idx1, 1.0,
                      jnp.where(lane == idx2 + N_EXPERTS, 1.0,
                                jnp.where(lane == ROUTE_W_LANE, w1,
                                          jnp.where(lane == ROUTE_W_LANE + 1, w2, 0.0))))
    sel = selrows_ref[...]
    ln = lax.broadcasted_iota(jnp.int32, (MOE_TD, LANES), 1)
    for sub in range(tm // MOE_TD):
        tile = slice(sub * MOE_TD, (sub + 1) * MOE_TD)
        rt = route[tile]
        cnt_out[0, sub:sub + 1, :] = jnp.sum(rt, axis=0, keepdims=True).astype(jnp.int32)
        v = _local_slots(rt, lstrict_ref[...], fold_ref[...], upper_ref[...])
        hi = jnp.floor(v * (1.0 / MOE_ALIGN))
        lo = v - hi * MOE_ALIGN
        rows_out[0, sub * SUBLANES:(sub + 1) * SUBLANES, :] = (
            _dot_nt(sel, hi.astype(BF16)) * MOE_ALIGN + _dot_nt(sel, lo.astype(BF16)))
        r1 = jnp.sum(jnp.where(ln < N_EXPERTS, v, 0.0), axis=-1, keepdims=True)
        r2 = jnp.sum(jnp.where(ln < N_EXPERTS, 0.0, v), axis=-1, keepdims=True)
        route_out[0, tile, :] = jnp.where(ln == ROUTE_ROW_LANE, r1,
                                          jnp.where(ln == ROUTE_ROW_LANE + 1, r2, rt))


def _outproj(x, ao, ho, mod, wo, g2, wr, br, tm):
    b, s, d = x.shape
    sub = tm // MOE_TD
    const = lambda shape: pl.BlockSpec(shape, lambda bi, si: (0,) * len(shape))
    tok = lambda w: pl.BlockSpec((1, tm, w), lambda bi, si: (bi, si, 0))
    return pl.pallas_call(
        _outproj_kernel,
        grid=(b, s // tm),
        in_specs=[tok(d), tok(ATTN_WIDTH), tok(HGRN_WIDTH),
                  pl.BlockSpec((1, 6, d), lambda bi, si: (bi, 0, 0)),
                  const((d, d)), const((1, d)), const((2, d, LANES)), const((1, LANES)),
                  const((MOE_TD, MOE_TD)), const((LANES, LANES)), const((LANES, LANES)),
                  const((SUBLANES, LANES))],
        out_specs=[tok(d), tok(d), tok(LANES),
                   pl.BlockSpec((1, sub, LANES), lambda bi, si: (bi * (s // tm) + si, 0, 0)),
                   pl.BlockSpec((1, sub * SUBLANES, MOE_TD),
                                lambda bi, si: (bi * (s // tm) + si, 0, 0))],
        out_shape=[jax.ShapeDtypeStruct((b, s, d), F32),
                   jax.ShapeDtypeStruct((b, s, d), BF16),
                   jax.ShapeDtypeStruct((b, s, LANES), F32),
                   jax.ShapeDtypeStruct((b * s // tm, sub, LANES), jnp.int32),
                   jax.ShapeDtypeStruct((b * s // tm, sub * SUBLANES, MOE_TD), F32)],
        compiler_params=pltpu.CompilerParams(dimension_semantics=("arbitrary", "arbitrary"),
                                             vmem_limit_bytes=VMEM_LIMIT),
        name="outproj",
    )(x, ao, ho, mod, wo, g2, wr, br, *_moe_constants())


def _ceil_to(v, m):
    return ((v + (m - 1)) // m) * m


def _moe_constants():
    a = np.arange(LANES)
    ne = N_EXPERTS
    td = MOE_TD
    lstrict = np.tril(np.ones((td, td)), -1)
    fold = ((a[:, None] < 2 * ne) & (a[None, :] < 2 * ne) & (a[:, None] % ne == a[None, :] % ne))
    upper = ((a[:, None] < ne) & (a[None, :] < 2 * ne) & (a[:, None] < a[None, :] % ne))
    selrows = np.zeros((SUBLANES, LANES))
    selrows[0, :ne] = 1.0
    selrows[1, ne:2 * ne] = 1.0
    return tuple(jnp.asarray(m, BF16) for m in (lstrict, fold, upper, selrows))


def _local_slots(route, lstrict, fold, upper):
    lane = lax.broadcasted_iota(jnp.int32, route.shape, 1)
    member = jnp.where(lane < 2 * N_EXPERTS, route, 0.0)
    rank = _dot(_dot(lstrict, member.astype(BF16)).astype(BF16), fold)
    cnt = jnp.broadcast_to(jnp.sum(member, axis=0, keepdims=True), (SUBLANES, LANES))
    cnt = _dot(cnt.astype(BF16), fold)
    run = jnp.floor((cnt + (MOE_ALIGN - 1)) * (1.0 / MOE_ALIGN)) * MOE_ALIGN
    start = _dot(run.astype(BF16), upper)[0:1, :]
    return member * (start + rank)


def _plan_kernel(c_ref, base_ref, texp_ref, meta_ref, tail_ref, *, ntiles, n_row_tiles):
    ne = N_EXPERTS
    off = jnp.int32(0)
    for e in range(ne):
        def body(i, run, e=e, off=off):
            c = c_ref[i * 2 * ne + e] + c_ref[i * 2 * ne + ne + e]
            base_ref[i * ne + e] = off + run
            return run + _ceil_to(c, MOE_ALIGN)
        total = lax.fori_loop(0, ntiles, body, jnp.int32(0))
        nt = (total + (MOE_TM - 1)) // MOE_TM
        first = off // MOE_TM

        def fill(j, carry, e=e, first=first):
            texp_ref[first + j] = e
            return carry
        lax.fori_loop(0, nt, fill, 0)
        tail_ref[e] = off + total
        tail_ref[ne + e] = (nt * MOE_TM - total) // MOE_ALIGN
        off = off + nt * MOE_TM
    nvalid = off // MOE_TM

    def fill_rest(j, carry):
        texp_ref[j] = ne - 1
        return carry
    lax.fori_loop(nvalid, n_row_tiles, fill_rest, 0)
    meta_ref[0] = nvalid


def _plan(counts_flat, ntiles, n_row_tiles):
    smem = pl.BlockSpec(memory_space=pltpu.SMEM)
    return pl.pallas_call(
        functools.partial(_plan_kernel, ntiles=ntiles, n_row_tiles=n_row_tiles),
        in_specs=[smem],
        out_specs=[smem, smem, smem, smem],
        out_shape=[jax.ShapeDtypeStruct((ntiles * N_EXPERTS,), jnp.int32),
                   jax.ShapeDtypeStruct((n_row_tiles,), jnp.int32),
                   jax.ShapeDtypeStruct((1,), jnp.int32),
                   jax.ShapeDtypeStruct((2 * N_EXPERTS,), jnp.int32)],
        name="moe_plan",
    )(counts_flat)


def _run_chunks(c_ref, base_ref, tile, fn):
    ne = N_EXPERTS
    local = jnp.int32(0)
    for e in range(ne):
        c = c_ref[tile * 2 * ne + e] + c_ref[tile * 2 * ne + ne + e]
        nchunk = (c + (MOE_ALIGN - 1)) // MOE_ALIGN
        hbm = base_ref[tile * ne + e]

        def body(j, carry, hbm=hbm, local=local):
            fn(pl.multiple_of(hbm + j * MOE_ALIGN, MOE_ALIGN),
               pl.multiple_of(local + j * MOE_ALIGN, MOE_ALIGN))
            return carry
        lax.fori_loop(0, nchunk, body, 0)
        local = local + nchunk * MOE_ALIGN
    return local // MOE_ALIGN


def _dispatch_kernel(c_ref, base_ref, tail_ref, meta_ref, h2_ref, rows_ref, xs_ref,
                     xs_scr, zero_scr, sem, nch_ref, *, n_row_tiles):
    i = pl.program_id(0)
    n = pl.num_programs(0)
    slot = i % 2

    def copy(slot_, hbm, local):
        return pltpu.make_async_copy(xs_scr.at[slot_, pl.ds(local, MOE_ALIGN), :],
                                     xs_ref.at[pl.ds(hbm, MOE_ALIGN), :], sem.at[slot_])

    def wait_all(slot_):
        def body(j, carry):
            copy(slot_, 0, 0).wait()
            return carry
        lax.fori_loop(0, nch_ref[slot_], body, 0)

    @pl.when(i >= 2)
    def _():
        wait_all(slot)

    rows = rows_ref[0]
    slot_id = lax.broadcasted_iota(jnp.int32, (MOE_L, MOE_TD), 0)
    p1 = jnp.where(slot_id == rows[0:1, :].astype(jnp.int32), 1.0, 0.0)
    p2 = jnp.where(slot_id == rows[1:2, :].astype(jnp.int32), 1.0, 0.0)
    xs_scr[slot] = _dot((p1 + p2).astype(BF16), h2_ref[...]).astype(BF16)
    nch_ref[slot] = _run_chunks(c_ref, base_ref, i, lambda hbm, local: copy(slot, hbm, local).start())

    @pl.when(i == n - 1)
    def _():
        zero_scr[...] = jnp.zeros_like(zero_scr)

        def chunk(hbm):
            return pltpu.make_async_copy(zero_scr.at[pl.ds(0, MOE_ALIGN), :],
                                         xs_ref.at[pl.ds(hbm, MOE_ALIGN), :], sem.at[2])

        def tile(j):
            return pltpu.make_async_copy(zero_scr, xs_ref.at[pl.ds(j * MOE_TM, MOE_TM), :], sem.at[3])

        nchunks = jnp.int32(0)
        for e in range(N_EXPERTS):
            start = tail_ref[e]
            cnt = tail_ref[N_EXPERTS + e]

            def fill(j, carry, start=start):
                chunk(pl.multiple_of(start + j * MOE_ALIGN, MOE_ALIGN)).start()
                return carry
            lax.fori_loop(0, cnt, fill, 0)
            nchunks = nchunks + cnt

        def fill_tile(j, carry):
            tile(j).start()
            return carry
        lax.fori_loop(meta_ref[0], n_row_tiles, fill_tile, 0)

        wait_all(slot)

        @pl.when(n >= 2)
        def _():
            wait_all(1 - slot)

        def wait_chunk(j, carry):
            chunk(0).wait()
            return carry
        lax.fori_loop(0, nchunks, wait_chunk, 0)

        def wait_tile(j, carry):
            tile(0).wait()
            return carry
        lax.fori_loop(meta_ref[0], n_row_tiles, wait_tile, 0)


def _dispatch(counts_flat, base, tail, meta, h2, rows, n_rows):
    t, d = h2.shape
    ntiles = t // MOE_TD
    grid_spec = pltpu.PrefetchScalarGridSpec(
        num_scalar_prefetch=4,
        grid=(ntiles,),
        in_specs=[pl.BlockSpec((MOE_TD, d), lambda i, *_: (i, 0)),
                  pl.BlockSpec((1, SUBLANES, MOE_TD), lambda i, *_: (i, 0, 0))],
        out_specs=pl.BlockSpec(memory_space=pl.ANY),
        scratch_shapes=[pltpu.VMEM((2, MOE_L, d), BF16), pltpu.VMEM((MOE_TM, d), BF16),
                        pltpu.SemaphoreType.DMA((4,)), pltpu.SMEM((2,), jnp.int32)])
    return pl.pallas_call(
        functools.partial(_dispatch_kernel, n_row_tiles=n_rows // MOE_TM),
        grid_spec=grid_spec,
        out_shape=jax.ShapeDtypeStruct((n_rows, d), BF16),
        compiler_params=pltpu.CompilerParams(dimension_semantics=("arbitrary",),
                                             vmem_limit_bytes=VMEM_LIMIT),
        name="moe_dispatch",
    )(counts_flat, base, tail, meta, h2, rows)


def _experts_kernel(texp_ref, meta_ref, xs_ref, wg_ref, wu_ref, wd_ref, ys_ref):
    del texp_ref

    @pl.when(pl.program_id(0) >= meta_ref[0])
    def _():
        ys_ref[...] = jnp.zeros_like(ys_ref)

    @pl.when(pl.program_id(0) < meta_ref[0])
    def _():
        x = xs_ref[...]
        act = _silu(_dot(x, wg_ref[0])) * _dot(x, wu_ref[0])
        ys_ref[...] = _dot(act.astype(BF16), wd_ref[0]).astype(BF16)


def _experts(texp, meta, xs, wg, wu, wd):
    n_rows, d = xs.shape
    row = lambda w: pl.BlockSpec((MOE_TM, w), lambda i, te, mt: (jnp.minimum(i, mt[0] - 1), 0))
    wspec = lambda shape: pl.BlockSpec((1,) + shape, lambda i, te, mt: (te[i], 0, 0))
    grid_spec = pltpu.PrefetchScalarGridSpec(
        num_scalar_prefetch=2,
        grid=(n_rows // MOE_TM,),
        in_specs=[row(d), wspec((d, D_EXPERT)), wspec((d, D_EXPERT)), wspec((D_EXPERT, d))],
        out_specs=pl.BlockSpec((MOE_TM, d), lambda i, te, mt: (i, 0)))
    return pl.pallas_call(
        _experts_kernel,
        grid_spec=grid_spec,
        out_shape=jax.ShapeDtypeStruct((n_rows, d), BF16),
        compiler_params=pltpu.CompilerParams(dimension_semantics=("arbitrary",),
                                             vmem_limit_bytes=VMEM_LIMIT),
        name="moe_experts",
    )(texp, meta, xs, wg, wu, wd)


def _combine_kernel(c_ref, base_ref, x1_ref, route_ref, mod_ref, ys_ref, o_ref, ys_scr, sem,
                    nch_ref):
    i = pl.program_id(0)
    n = pl.num_programs(0)
    slot = i % 2

    def copy(slot_, hbm, local):
        return pltpu.make_async_copy(ys_ref.at[pl.ds(hbm, MOE_ALIGN), :],
                                     ys_scr.at[slot_, pl.ds(local, MOE_ALIGN), :], sem.at[slot_])

    def fetch(tile, slot_):
        nch_ref[slot_] = _run_chunks(c_ref, base_ref, tile,
                                     lambda hbm, local: copy(slot_, hbm, local).start())

    @pl.when(i == 0)
    def _():
        ys_scr[...] = jnp.zeros_like(ys_scr)
        fetch(0, 0)

    @pl.when(i + 1 < n)
    def _():
        fetch(i + 1, 1 - slot)

    route = route_ref[...]
    lane = lax.broadcasted_iota(jnp.int32, route.shape, 1)
    r1 = jnp.sum(jnp.where(lane == ROUTE_ROW_LANE, route, 0.0), axis=-1, keepdims=True)
    r2 = jnp.sum(jnp.where(lane == ROUTE_ROW_LANE + 1, route, 0.0), axis=-1, keepdims=True)
    r1, r2 = r1.astype(jnp.int32), r2.astype(jnp.int32)
    col = lax.broadcasted_iota(jnp.int32, (MOE_TD, MOE_L), 1)
    pick1 = jnp.where(col == r1, 1.0, 0.0).astype(BF16)
    pick2 = jnp.where(col == r2, 1.0, 0.0).astype(BF16)
    w1 = jnp.sum(jnp.where(lane == ROUTE_W_LANE, route, 0.0), axis=-1, keepdims=True)
    w2 = jnp.sum(jnp.where(lane == ROUTE_W_LANE + 1, route, 0.0), axis=-1, keepdims=True)

    def wait(j, carry):
        copy(slot, 0, 0).wait()
        return carry
    lax.fori_loop(0, nch_ref[slot], wait, 0)
    ys = ys_scr[slot]
    y = w1 * _dot(pick1, ys) + w2 * _dot(pick2, ys)
    o_ref[...] = x1_ref[...] + mod_ref[0, 5:6, :] * y


def _combine(counts_flat, base, x1, route, mod, ys, tiles_per_batch):
    t, d = x1.shape
    tok = lambda w: pl.BlockSpec((MOE_TD, w), lambda i, c, b: (i, 0))
    grid_spec = pltpu.PrefetchScalarGridSpec(
        num_scalar_prefetch=2,
        grid=(t // MOE_TD,),
        in_specs=[tok(d), tok(LANES),
                  pl.BlockSpec((1, 6, d), lambda i, c, b: (i // tiles_per_batch, 0, 0)),
                  pl.BlockSpec(memory_space=pl.ANY)],
        out_specs=tok(d),
        scratch_shapes=[pltpu.VMEM((2, MOE_L, d), BF16), pltpu.SemaphoreType.DMA((2,)),
                        pltpu.SMEM((2,), jnp.int32)])
    return pl.pallas_call(
        _combine_kernel,
        grid_spec=grid_spec,
        out_shape=jax.ShapeDtypeStruct((t, d), F32),
        compiler_params=pltpu.CompilerParams(dimension_semantics=("arbitrary",),
                                             vmem_limit_bytes=VMEM_LIMIT),
        name="moe_combine",
    )(counts_flat, base, x1, route, mod, ys)


def kernel(x, c, w_ada, b_ada, norm1_g, w_in, b_fox, q_norm_g, k_norm_g, attn_out_g, hgrn_lb,
           hgrn_out_g, w_out, norm2_g, w_router_group, b_router_group, w_router_expert,
           b_router_expert, w_gate, w_up, w_down):
    b, s, d = x.shape
    l = 0
    aw = ATTN_WIDTH
    mod = _ada(c, w_ada[l], b_ada[l]).reshape(b, 6, d)

    w = w_in[l]
    wq = w[:, 0:aw].astype(BF16)
    wk = w[:, aw:2 * aw].astype(BF16)
    wv = w[:, 2 * aw:3 * aw].astype(BF16)
    f0 = 3 * aw
    pad_f = LANES - FOX_PIECES * ATTN_HEADS
    wf = jnp.pad(jnp.tile(w[:, f0:f0 + ATTN_HEADS], (1, FOX_PIECES)), ((0, 0), (0, pad_f))).astype(BF16)
    wh = w[:, f0 + ATTN_HEADS:].astype(BF16)
    bfox = jnp.pad(jnp.tile(b_fox[l], FOX_PIECES), (0, pad_f)).reshape(1, LANES)
    gq = jnp.tile(q_norm_g[l], ATTN_HEADS).reshape(1, aw)
    gk = jnp.tile(k_norm_g[l], ATTN_HEADS).reshape(1, aw)

    tm = min(512, s)
    q, k, v, ck, hq, hf, hi, hg = _inproj(x, mod, norm1_g[l].reshape(1, d), wq, wk, wv, wf, wh,
                                           gq, gk, bfox, tm)
    ao = _fox(q, k, v, ck, attn_out_g[l].reshape(1, aw), min(256, s), 128)
    ho = _hgrn(hq, hf, hi, hg, hgrn_lb[0:2], hgrn_out_g[l].reshape(1, HGRN_WIDTH))

    wr = jnp.pad(jnp.concatenate([w_router_expert[l], w_router_group[l]], axis=1),
                 ((0, 0), (0, LANES - N_GROUPS - N_EXPERTS)))
    br = jnp.pad(jnp.concatenate([b_router_expert[l], b_router_group[l]]),
                 (0, LANES - N_GROUPS - N_EXPERTS)).reshape(1, LANES)
    wr_hi = wr.astype(BF16)
    wr2 = jnp.stack([wr_hi, (wr - wr_hi.astype(F32)).astype(BF16)])
    x1, h2, route, counts, rows = _outproj(x, ao, ho, mod, w_out[l].astype(BF16),
                                           norm2_g[l].reshape(1, d), wr2, br, tm)

    t = b * s
    ntiles = t // MOE_TD
    counts_flat = counts.reshape(ntiles, LANES)[:, :2 * N_EXPERTS].reshape(-1)
    n_rows = _ceil_to(2 * t + ntiles * N_EXPERTS * (MOE_ALIGN - 1), MOE_TM) + N_EXPERTS * MOE_TM
    base, texp, meta, tail = _plan(counts_flat, ntiles, n_rows // MOE_TM)
    route2 = route.reshape(t, LANES)
    xs = _dispatch(counts_flat, base, tail, meta, h2.reshape(t, d),
                   rows.reshape(ntiles, SUBLANES, MOE_TD), n_rows)
    ys = _experts(texp, meta, xs, w_gate[l].astype(BF16), w_up[l].astype(BF16),
                  w_down[l].astype(BF16))
    out = _combine(counts_flat, base, x1.reshape(t, d), route2, mod, ys, s // MOE_TD)
    return out.reshape(b, s, d)
```

```python
import functools

import numpy as np
import jax
import jax.numpy as jnp
from jax import lax
from jax.experimental import pallas as pl
from jax.experimental.pallas import tpu as pltpu

F32 = jnp.float32
BF16 = jnp.bfloat16

D_MODEL = 1024
ATTN_HEAD_DIM = 64
ATTN_WIDTH = 512
ATTN_HEADS = 8
HGRN_WIDTH = 512
HGRN_HEADS = 4
HGRN_DK = 128
N_GROUPS = 4
EXPERTS_PER_GROUP = 4
N_EXPERTS = 16
D_EXPERT = 512
NORM_EPS = 1e-6
LANES = 128
SUBLANES = 8
VMEM_LIMIT = 56 * 1024 * 1024

HGRN_CHUNK = 128
HGRN_STEP = 512
HGRN_LEVELS = (1, 2, 4, 8, 16, 32, 64)
HGRN_MXU_LEVELS = (1, 2)
ROUTE_W_LANE = 2 * N_EXPERTS
ROUTE_ROW_LANE = ROUTE_W_LANE + 2
MOE_TD = 256
MOE_ALIGN = 16
MOE_TM = 512
MOE_L = 2 * MOE_TD + MOE_ALIGN * N_EXPERTS
LOG2E = 1.4426950408889634
FOX_GROUP = 4
FOX_PIECES = 3


def _sigmoid(x):
    return 1.0 / (1.0 + jnp.exp(-x))


def _silu(x):
    return x * (0.5 * jnp.tanh(0.5 * x) + 0.5)


def _split3(x):
    p1 = x.astype(BF16)
    r1 = x - p1.astype(F32)
    p2 = r1.astype(BF16)
    p3 = (r1 - p2.astype(F32)).astype(BF16)
    return p1, p2, p3


def _dot(a, b):
    return jnp.dot(a, b, preferred_element_type=F32)


def _dot_nt(a, b):
    return lax.dot_general(a, b, (((1,), (1,)), ((), ())), preferred_element_type=F32)


def _ada_kernel(c_ref, w_ref, b_ref, o_ref):
    c = c_ref[...]
    o_ref[...] = jnp.dot(_silu(c), w_ref[...], preferred_element_type=F32,
                         precision=lax.Precision.HIGHEST) + b_ref[...]


def _ada(c, w, b):
    bsz, d = c.shape
    n = w.shape[1]
    tn = 1024
    return pl.pallas_call(
        _ada_kernel,
        grid=(n // tn,),
        in_specs=[pl.BlockSpec((bsz, d), lambda j: (0, 0)),
                  pl.BlockSpec((d, tn), lambda j: (0, j)),
                  pl.BlockSpec((1, tn), lambda j: (0, j))],
        out_specs=pl.BlockSpec((bsz, tn), lambda j: (0, j)),
        out_shape=jax.ShapeDtypeStruct((bsz, n), F32),
        compiler_params=pltpu.CompilerParams(dimension_semantics=("arbitrary",),
                                             vmem_limit_bytes=VMEM_LIMIT),
        name="ada",
    )(c, w, b.reshape(1, n))


def _inproj_kernel(x_ref, mod_ref, g1_ref, wq_ref, wk_ref, wv_ref, wf_ref, wh_ref,
                   gq_ref, gk_ref, bf_ref, gsum_ref, tri_ref, place_ref,
                   q_out, k_out, v_out, ck_out, hq_out, hf_out, hi_out, hg_out,
                   carry_ref):
    si = pl.program_id(1)

    @pl.when(si == 0)
    def _():
        carry_ref[...] = jnp.zeros_like(carry_ref)

    x = x_ref[0]
    shift = mod_ref[0, 0:1, :]
    scale = mod_ref[0, 1:2, :]
    ms = jnp.mean(x * x, axis=-1, keepdims=True)
    h = (x * lax.rsqrt(ms + NORM_EPS) * g1_ref[...]) * (1.0 + scale) + shift
    hb = h.astype(BF16)

    def qk_norm(w_ref, g_ref, mult):
        t = _dot(hb, w_ref[...])
        ssq = _dot((t * t).astype(BF16), gsum_ref[...])
        return t * lax.rsqrt(ssq * (1.0 / ATTN_HEAD_DIM) + NORM_EPS) * (g_ref[...] * mult)

    q_out[0] = qk_norm(wq_ref, gq_ref, ATTN_HEAD_DIM ** -0.5 * LOG2E).T.astype(BF16)
    k_out[0] = qk_norm(wk_ref, gk_ref, 1.0).astype(BF16)
    v_out[0] = _dot(hb, wv_ref[...]).T.astype(BF16)

    af = _dot(hb, wf_ref[...]) + bf_ref[...]
    lf = jnp.minimum(af, 0.0) - jnp.log(1.0 + jnp.exp(-jnp.abs(af)))
    tri = tri_ref[...]
    p1, p2, p3 = _split3(lf)
    cum = (_dot(tri, p1) + _dot(tri, p2)) + _dot(tri, p3) + carry_ref[...]
    tm = cum.shape[0]
    carry_ref[...] = cum[tm - 1:tm, :]
    c1, c2, c3 = _split3(cum * (-LOG2E))
    lane = lax.broadcasted_iota(jnp.int32, cum.shape, 1)
    zero = jnp.zeros_like(c1)
    pieces = jnp.where(lane < ATTN_HEADS, c1,
                       jnp.where(lane < 2 * ATTN_HEADS, c2,
                                 jnp.where(lane < 3 * ATTN_HEADS, c3, zero)))
    ck_out[0] = _dot(pieces, place_ref[...]).astype(BF16)

    hq_out[0] = _dot(hb, wh_ref[:, 0 * HGRN_WIDTH:1 * HGRN_WIDTH]).astype(BF16)
    hf_out[0] = _dot(hb, wh_ref[:, 1 * HGRN_WIDTH:2 * HGRN_WIDTH]).astype(BF16)
    hi_out[0] = _dot(hb, wh_ref[:, 2 * HGRN_WIDTH:3 * HGRN_WIDTH]).astype(BF16)
    hg_out[0] = _dot(hb, wh_ref[:, 3 * HGRN_WIDTH:4 * HGRN_WIDTH]).astype(BF16)


def _inproj(x, mod, g1, wq, wk, wv, wf, wh, gq, gk, bfox, tm):
    b, s, d = x.shape
    gsum = jnp.asarray(np.kron(np.eye(ATTN_HEADS), np.ones((ATTN_HEAD_DIM, ATTN_HEAD_DIM))), BF16)
    tri = jnp.asarray(np.tril(np.ones((tm, tm))), BF16)
    place_np = np.zeros((LANES, ATTN_HEADS * LANES), np.float32)
    for piece in range(FOX_PIECES):
        for hd in range(ATTN_HEADS):
            place_np[piece * ATTN_HEADS + hd, hd * LANES + piece] = 1.0
    place = jnp.asarray(place_np, BF16)
    const = lambda shape: pl.BlockSpec(shape, lambda bi, si: (0,) * len(shape))
    tok = lambda w: pl.BlockSpec((1, tm, w), lambda bi, si: (bi, si, 0))
    tok_t = lambda w: pl.BlockSpec((1, w, tm), lambda bi, si: (bi, 0, si))
    act = lambda w: jax.ShapeDtypeStruct((b, s, w), BF16)
    act_t = lambda w: jax.ShapeDtypeStruct((b, w, s), BF16)
    return pl.pallas_call(
        _inproj_kernel,
        grid=(b, s // tm),
        in_specs=[tok(d),
                  pl.BlockSpec((1, 6, d), lambda bi, si: (bi, 0, 0)),
                  const((1, d)),
                  const((d, ATTN_WIDTH)), const((d, ATTN_WIDTH)), const((d, ATTN_WIDTH)),
                  const((d, LANES)), const((d, 4 * HGRN_WIDTH)),
                  const((1, ATTN_WIDTH)), const((1, ATTN_WIDTH)), const((1, LANES)),
                  const((ATTN_WIDTH, ATTN_WIDTH)), const((tm, tm)),
                  const((LANES, ATTN_HEADS * LANES))],
        out_specs=[tok_t(ATTN_WIDTH), tok(ATTN_WIDTH), tok_t(ATTN_WIDTH),
                   tok(ATTN_HEADS * LANES),
                   tok(HGRN_WIDTH), tok(HGRN_WIDTH), tok(HGRN_WIDTH), tok(HGRN_WIDTH)],
        out_shape=[act_t(ATTN_WIDTH), act(ATTN_WIDTH), act_t(ATTN_WIDTH),
                   act(ATTN_HEADS * LANES),
                   act(HGRN_WIDTH), act(HGRN_WIDTH), act(HGRN_WIDTH), act(HGRN_WIDTH)],
        scratch_shapes=[pltpu.VMEM((1, LANES), F32)],
        compiler_params=pltpu.CompilerParams(dimension_semantics=("arbitrary", "arbitrary"),
                                             vmem_limit_bytes=VMEM_LIMIT),
        name="inproj",
    )(x, mod, g1, wq, wk, wv, wf, wh, gq, gk, bfox, gsum, tri, place)


def _fox_kernel(qt_ref, k_ref, vt_ref, ck_ref, g_ref, o_ref, st_scr, pt_scr, *, tq, tk):
    qi = pl.program_id(2)
    qt = qt_ref[0].astype(F32)
    row = lax.broadcasted_iota(jnp.int32, (LANES, tq), 0)
    first = row < ATTN_HEAD_DIM
    ones = jnp.where(row < FOX_PIECES, 1.0, 0.0)
    rhs = []
    for hd in range(FOX_GROUP):
        qp = qt[(hd // 2) * LANES:(hd // 2 + 1) * LANES, :]
        qh = jnp.where(first, qp, 0.0) if hd % 2 == 0 else jnp.where(first, 0.0, qp)
        rhs.append(jnp.concatenate([qh, ones], axis=0).astype(BF16))

    def stage_qk(j, slot):
        k0 = pl.multiple_of(j * tk, tk)
        kb = k_ref[0, pl.ds(k0, tk), :]
        ckb = ck_ref[0, pl.ds(k0, tk), :]
        for hd in range(FOX_GROUP):
            pr = slice((hd // 2) * LANES, (hd // 2 + 1) * LANES)
            lhs = jnp.concatenate([kb[:, pr], ckb[:, hd * LANES:(hd + 1) * LANES]], axis=1)
            st_scr[slot, hd] = _dot(lhs, rhs[hd])

    def stage_softmax(j, slot, stats, masked):
        out = []
        for hd in range(FOX_GROUP):
            m, l = stats[hd]
            st = st_scr[slot, hd]
            if masked:
                key = j * tk + lax.broadcasted_iota(jnp.int32, (tk, tq), 0)
                qry = qi * tq + lax.broadcasted_iota(jnp.int32, (tk, tq), 1)
                st = jnp.where(key <= qry, st, -jnp.inf)
            m_new = jnp.maximum(m, jnp.max(st, axis=0, keepdims=True))
            alpha = jnp.exp2(m - m_new)
            pt = jnp.exp2(st - m_new)
            pt_scr[slot, hd] = pt.astype(BF16)
            out.append(((m_new, alpha * l + jnp.sum(pt, axis=0, keepdims=True)), alpha))
        return tuple(o[0] for o in out), tuple(o[1] for o in out)

    def stage_pv(j, slot, accs, alphas):
        k0 = pl.multiple_of(jnp.maximum(j, 0) * tk, tk)
        vtb = vt_ref[0, :, pl.ds(k0, tk)]
        out = []
        for hd in range(FOX_GROUP):
            vth = vtb[hd * ATTN_HEAD_DIM:(hd + 1) * ATTN_HEAD_DIM, :]
            out.append(alphas[hd] * accs[hd] + _dot(vth, pt_scr[slot, hd]))
        return tuple(out)

    per_q = tq // tk
    stats = tuple((jnp.full((1, tq), -1e30, F32), jnp.zeros((1, tq), F32))
                  for _ in range(FOX_GROUP))
    accs = tuple(jnp.zeros((ATTN_HEAD_DIM, tq), F32) for _ in range(FOX_GROUP))
    alphas = tuple(jnp.ones((1, tq), F32) for _ in range(FOX_GROUP))
    pt_scr[1] = jnp.zeros_like(pt_scr[1])
    stage_qk(0, 0)

    def full_blocks(i, carry):
        stats, accs, alphas = carry
        for u in range(per_q):
            j = i * per_q + u
            accs = stage_pv(j - 1, 1 - u, accs, alphas)
            stats, alphas = stage_softmax(j, u, stats, False)
            stage_qk(j + 1, 1 - u)
        return stats, accs, alphas

    stats, accs, alphas = lax.fori_loop(0, qi, full_blocks, (stats, accs, alphas))
    j = qi * per_q
    accs = stage_pv(j - 1, 1, accs, alphas)
    stats, alphas = stage_softmax(j, 0, stats, True)
    stage_qk(j + 1, 1)
    accs = stage_pv(j, 0, accs, alphas)
    stats, alphas = stage_softmax(j + 1, 1, stats, True)
    accs = stage_pv(j + 1, 1, accs, alphas)
    carry = tuple((stats[hd][0], stats[hd][1], accs[hd]) for hd in range(FOX_GROUP))

    head0 = lax.broadcasted_iota(jnp.int32, (tq, LANES), 1) < ATTN_HEAD_DIM
    for pr in range(FOX_GROUP // 2):
        (_, l0, a0), (_, l1, a1) = carry[2 * pr], carry[2 * pr + 1]
        ot = jnp.concatenate([a0 * (1.0 / l0), a1 * (1.0 / l1)], axis=0)
        o = ot.T
        osq = o * o
        ss0 = jnp.sum(jnp.where(head0, osq, 0.0), axis=-1, keepdims=True)
        ss1 = jnp.sum(jnp.where(head0, 0.0, osq), axis=-1, keepdims=True)
        ms = jnp.where(head0, ss0, ss1) * (1.0 / ATTN_HEAD_DIM)
        sl = slice(pr * LANES, (pr + 1) * LANES)
        o_ref[0, :, sl] = (o * lax.rsqrt(ms + NORM_EPS) * g_ref[:, sl]).astype(BF16)


def _fox(qt, k, vt, ck, g_out, tq, tk):
    b, s, _ = k.shape
    groups = ATTN_HEADS // FOX_GROUP
    gw = FOX_GROUP * ATTN_HEAD_DIM
    return pl.pallas_call(
        functools.partial(_fox_kernel, tq=tq, tk=tk),
        grid=(b, groups, s // tq),
        in_specs=[pl.BlockSpec((1, gw, tq), lambda bi, g, qi: (bi, g, qi)),
                  pl.BlockSpec((1, s, gw), lambda bi, g, qi: (bi, 0, g)),
                  pl.BlockSpec((1, gw, s), lambda bi, g, qi: (bi, g, 0)),
                  pl.BlockSpec((1, s, FOX_GROUP * LANES), lambda bi, g, qi: (bi, 0, g)),
                  pl.BlockSpec((1, gw), lambda bi, g, qi: (0, g))],
        out_specs=pl.BlockSpec((1, tq, gw), lambda bi, g, qi: (bi, qi, g)),
        out_shape=jax.ShapeDtypeStruct((b, s, ATTN_WIDTH), BF16),
        scratch_shapes=[pltpu.VMEM((2, FOX_GROUP, tk, tq), F32),
                        pltpu.VMEM((2, FOX_GROUP, tk, tq), BF16)],
        compiler_params=pltpu.CompilerParams(
            dimension_semantics=("arbitrary", "arbitrary", "arbitrary"),
            vmem_limit_bytes=VMEM_LIMIT),
        name="fox",
    )(qt, k, vt, ck, g_out)


def _hgrn_decay_matrix(c):
    t = np.arange(c)[:, None]
    j = np.arange(c)[None, :]
    blocks = [(j <= t)]
    for m in HGRN_MXU_LEVELS:
        mid = (t // (2 * m)) * (2 * m) + m
        right = (t % (2 * m)) >= m
        blocks.append(np.where(right, (j >= mid) & (j <= t), (j > t) & (j < mid)))
    return np.concatenate(blocks, axis=0).astype(np.float32)


def _hgrn_kernel(hq_ref, hf_ref, hi_ref, hg_ref, lb_ref, g_ref, w_ref, lvl_ref, o_ref, st_ref):
    ci = pl.program_id(1)
    c = HGRN_CHUNK

    @pl.when(ci == 0)
    def _():
        st_ref[...] = jnp.zeros_like(st_ref)

    r0 = lb_ref[0:1, :]
    r1 = lb_ref[1:2, :]
    rmax = jnp.maximum(r0, r1)
    e0 = jnp.exp(r0 - rmax)
    lb = e0 / (e0 + jnp.exp(r1 - rmax))

    lvl = lvl_ref[...]
    at_level = [lvl == float(li) for li in range(len(HGRN_LEVELS))]
    for sub in range(hq_ref.shape[1] // c):
        rows = slice(sub * c, (sub + 1) * c)
        f = lb + (1.0 - lb) * _sigmoid(hf_ref[0, rows, :].astype(F32))
        g = jnp.log2(f)
        g1 = g.astype(BF16)
        g2 = (g - g1.astype(F32)).astype(BF16)
        w = w_ref[...]
        xw = _dot(w, g1) + _dot(w, g2)
        cum = xw[0:c]
        cum3 = cum.reshape(c // SUBLANES, SUBLANES, cum.shape[1])

        def level_sums(m):
            if m in HGRN_MXU_LEVELS:
                at = 1 + HGRN_MXU_LEVELS.index(m)
                return xw[at * c:(at + 1) * c]
            if 2 * m == SUBLANES:
                ref = cum3[:, m - 1:m, :]
            else:
                last = cum3[:, SUBLANES - 1:SUBLANES, :]
                per = 2 * m // SUBLANES
                pick = [(r // per) * per + per // 2 - 1 for r in range(c // SUBLANES)]
                ref = jnp.concatenate([last[p:p + 1] for p in pick], axis=0)
            return -jnp.abs(cum - jnp.broadcast_to(ref, cum3.shape).reshape(cum.shape))

        e_levels = [jnp.exp2(level_sums(m)) for m in HGRN_LEVELS]
        e_pre_all = jnp.exp2(cum)
        e_suf_all = jnp.exp2(cum[c - 1:c, :] - cum)
        q_all = _silu(hq_ref[0, rows, :].astype(F32))
        k_all = 1.0 - f
        v_all = hi_ref[0, rows, :].astype(F32)


        for hd in range(HGRN_HEADS):
            sl = slice(hd * HGRN_DK, (hd + 1) * HGRN_DK)
            q, k, v = q_all[:, sl], k_all[:, sl], v_all[:, sl]
            e_pre = e_pre_all[:, sl]
            e_suf = e_suf_all[:, sl]

            a = jnp.zeros((c, c), F32)
            for li in range(len(HGRN_LEVELS)):
                e = e_levels[li][:, sl]
                a = jnp.where(at_level[li], _dot_nt((q * e).astype(BF16), (k * e).astype(BF16)), a)
            vb = v.astype(BF16)
            out = _dot(a.astype(BF16), vb)

            out = out + jnp.sum(q * k, axis=-1, keepdims=True) * v

            st = st_ref[hd]
            out = out + _dot_nt((q * e_pre).astype(BF16), st.astype(BF16))
            kdec = (k * e_suf).astype(BF16)
            upd = lax.dot_general(vb, kdec, (((0,), (0,)), ((), ())), preferred_element_type=F32)
            st_ref[hd] = st * e_pre[c - 1:c, :] + upd

            ms = jnp.mean(out * out, axis=-1, keepdims=True)
            gate = _silu(hg_ref[0, rows, sl].astype(F32))
            o_ref[0, rows, sl] = (out * lax.rsqrt(ms + NORM_EPS) * g_ref[:, sl] * gate).astype(BF16)


def _hgrn(hq, hf, hi, hg, lb_rows, g_out):
    b, s, wd = hq.shape
    c = HGRN_CHUNK
    wmat = jnp.asarray(_hgrn_decay_matrix(c), BF16)
    tt = np.arange(c)[:, None]
    ss = np.arange(c)[None, :]
    lvl_np = np.full((c, c), -1.0, np.float32)
    for li, m in reversed(list(enumerate(HGRN_LEVELS))):
        lvl_np[(ss < tt) & (tt // (2 * m) == ss // (2 * m)) & (tt // m != ss // m)] = li
    lvl = jnp.asarray(lvl_np)
    tok = pl.BlockSpec((1, HGRN_STEP, wd), lambda bi, ci: (bi, ci, 0))
    return pl.pallas_call(
        _hgrn_kernel,
        grid=(b, s // HGRN_STEP),
        in_specs=[tok, tok, tok, tok,
                  pl.BlockSpec((2, wd), lambda bi, ci: (0, 0)),
                  pl.BlockSpec((1, wd), lambda bi, ci: (0, 0)),
                  pl.BlockSpec(wmat.shape, lambda bi, ci: (0, 0)),
                  pl.BlockSpec((c, c), lambda bi, ci: (0, 0))],
        out_specs=tok,
        out_shape=jax.ShapeDtypeStruct((b, s, wd), BF16),
        scratch_shapes=[pltpu.VMEM((HGRN_HEADS, HGRN_DK, HGRN_DK), F32)],
        compiler_params=pltpu.CompilerParams(dimension_semantics=("arbitrary", "arbitrary"),
                                             vmem_limit_bytes=VMEM_LIMIT),
        name="hgrn",
    )(hq, hf, hi, hg, lb_rows, g_out, wmat, lvl)


def _outproj_kernel(x_ref, ao_ref, ho_ref, mod_ref, wo_ref, g2_ref, wr_ref, br_ref,
                    lstrict_ref, fold_ref, upper_ref, selrows_ref,
                    x1_out, h2_out, route_out, cnt_out, rows_out):
    gate1 = mod_ref[0, 2:3, :]
    shift2 = mod_ref[0, 3:4, :]
    scale2 = mod_ref[0, 4:5, :]
    mix = _dot(ao_ref[0], wo_ref[0:ATTN_WIDTH, :]) + _dot(ho_ref[0], wo_ref[ATTN_WIDTH:D_MODEL, :])
    x1 = x_ref[0] + gate1 * mix
    x1_out[0] = x1
    ms = jnp.mean(x1 * x1, axis=-1, keepdims=True)
    h2 = (x1 * lax.rsqrt(ms + NORM_EPS) * g2_ref[...]) * (1.0 + scale2) + shift2
    h2_out[0] = h2.astype(BF16)

    h2_hi = h2.astype(BF16)
    h2_lo = (h2 - h2_hi.astype(F32)).astype(BF16)
    logits = (_dot(h2_hi, wr_ref[0]) + (_dot(h2_hi, wr_ref[1]) + _dot(h2_lo, wr_ref[0]))) + br_ref[...]
    tm = logits.shape[0]
    lane = lax.broadcasted_iota(jnp.int32, (tm, LANES), 1)
    neg = -jnp.inf
    is_group = (lane >= N_EXPERTS) & (lane < N_EXPERTS + N_GROUPS)
    gl = jnp.where(is_group, logits, neg)
    gmax = jnp.max(gl, axis=-1, keepdims=True)
    gsum = jnp.sum(jnp.exp(gl - gmax), axis=-1, keepdims=True)
    group_p = 1.0 / gsum
    gidx = jnp.min(jnp.where(is_group & (gl == gmax), lane, LANES), axis=-1, keepdims=True) - N_EXPERTS
    in_group = (lane < N_EXPERTS) & ((lane // EXPERTS_PER_GROUP) == gidx)
    el = jnp.where(in_group, logits, neg)
    top1 = jnp.max(el, axis=-1, keepdims=True)
    idx1 = jnp.min(jnp.where(in_group & (el == top1), lane, LANES), axis=-1, keepdims=True)
    el2 = jnp.where(lane == idx1, neg, el)
    top2 = jnp.max(el2, axis=-1, keepdims=True)
    idx2 = jnp.min(jnp.where(in_group & (lane != idx1) & (el2 == top2), lane, LANES),
                   axis=-1, keepdims=True)
    e2 = jnp.exp(top2 - top1)
    w1 = group_p / (1.0 + e2)
    w2 = group_p * e2 / (1.0 + e2)
    route = jnp.where(lane == idx1, 1.0,
                      jnp.where(lane == idx2 + N_EXPERTS, 1.0,
                                jnp.where(lane == ROUTE_W_LANE, w1,
                                          jnp.where(lane == ROUTE_W_LANE + 1, w2, 0.0))))
    sel = selrows_ref[...]
    ln = lax.broadcasted_iota(jnp.int32, (MOE_TD, LANES), 1)
    for sub in range(tm // MOE_TD):
        tile = slice(sub * MOE_TD, (sub + 1) * MOE_TD)
        rt = route[tile]
        cnt_out[0, sub:sub + 1, :] = jnp.sum(rt, axis=0, keepdims=True).astype(jnp.int32)
        v = _local_slots(rt, lstrict_ref[...], fold_ref[...], upper_ref[...])
        hi = jnp.floor(v * (1.0 / MOE_ALIGN))
        lo = v - hi * MOE_ALIGN
        rows_out[0, sub * SUBLANES:(sub + 1) * SUBLANES, :] = (
            _dot_nt(sel, hi.astype(BF16)) * MOE_ALIGN + _dot_nt(sel, lo.astype(BF16)))
        r1 = jnp.sum(jnp.where(ln < N_EXPERTS, v, 0.0), axis=-1, keepdims=True)
        r2 = jnp.sum(jnp.where(ln < N_EXPERTS, 0.0, v), axis=-1, keepdims=True)
        route_out[0, tile, :] = jnp.where(ln == ROUTE_ROW_LANE, r1,
                                          jnp.where(ln == ROUTE_ROW_LANE + 1, r2, rt))


def _outproj(x, ao, ho, mod, wo, g2, wr, br, tm):
    b, s, d = x.shape
    sub = tm // MOE_TD
    const = lambda shape: pl.BlockSpec(shape, lambda bi, si: (0,) * len(shape))
    tok = lambda w: pl.BlockSpec((1, tm, w), lambda bi, si: (bi, si, 0))
    return pl.pallas_call(
        _outproj_kernel,
        grid=(b, s // tm),
        in_specs=[tok(d), tok(ATTN_WIDTH), tok(HGRN_WIDTH),
                  pl.BlockSpec((1, 6, d), lambda bi, si: (bi, 0, 0)),
                  const((d, d)), const((1, d)), const((2, d, LANES)), const((1, LANES)),
                  const((MOE_TD, MOE_TD)), const((LANES, LANES)), const((LANES, LANES)),
                  const((SUBLANES, LANES))],
        out_specs=[tok(d), tok(d), tok(LANES),
                   pl.BlockSpec((1, sub, LANES), lambda bi, si: (bi * (s // tm) + si, 0, 0)),
                   pl.BlockSpec((1, sub * SUBLANES, MOE_TD),
                                lambda bi, si: (bi * (s // tm) + si, 0, 0))],
        out_shape=[jax.ShapeDtypeStruct((b, s, d), F32),
                   jax.ShapeDtypeStruct((b, s, d), BF16),
                   jax.ShapeDtypeStruct((b, s, LANES), F32),
                   jax.ShapeDtypeStruct((b * s // tm, sub, LANES), jnp.int32),
                   jax.ShapeDtypeStruct((b * s // tm, sub * SUBLANES, MOE_TD), F32)],
        compiler_params=pltpu.CompilerParams(dimension_semantics=("arbitrary", "arbitrary"),
                                             vmem_limit_bytes=VMEM_LIMIT),
        name="outproj",
    )(x, ao, ho, mod, wo, g2, wr, br, *_moe_constants())


def _ceil_to(v, m):
    return ((v + (m - 1)) // m) * m


def _moe_constants():
    a = np.arange(LANES)
    ne = N_EXPERTS
    td = MOE_TD
    lstrict = np.tril(np.ones((td, td)), -1)
    fold = ((a[:, None] < 2 * ne) & (a[None, :] < 2 * ne) & (a[:, None] % ne == a[None, :] % ne))
    upper = ((a[:, None] < ne) & (a[None, :] < 2 * ne) & (a[:, None] < a[None, :] % ne))
    selrows = np.zeros((SUBLANES, LANES))
    selrows[0, :ne] = 1.0
    selrows[1, ne:2 * ne] = 1.0
    return tuple(jnp.asarray(m, BF16) for m in (lstrict, fold, upper, selrows))


def _local_slots(route, lstrict, fold, upper):
    lane = lax.broadcasted_iota(jnp.int32, route.shape, 1)
    member = jnp.where(lane < 2 * N_EXPERTS, route, 0.0)
    rank = _dot(_dot(lstrict, member.astype(BF16)).astype(BF16), fold)
    cnt = jnp.broadcast_to(jnp.sum(member, axis=0, keepdims=True), (SUBLANES, LANES))
    cnt = _dot(cnt.astype(BF16), fold)
    run = jnp.floor((cnt + (MOE_ALIGN - 1)) * (1.0 / MOE_ALIGN)) * MOE_ALIGN
    start = _dot(run.astype(BF16), upper)[0:1, :]
    return member * (start + rank)


def _plan_kernel(c_ref, base_ref, texp_ref, meta_ref, tail_ref, *, ntiles, n_row_tiles):
    ne = N_EXPERTS
    off = jnp.int32(0)
    for e in range(ne):
        def body(i, run, e=e, off=off):
            c = c_ref[i * 2 * ne + e] + c_ref[i * 2 * ne + ne + e]
            base_ref[i * ne + e] = off + run
            return run + _ceil_to(c, MOE_ALIGN)
        total = lax.fori_loop(0, ntiles, body, jnp.int32(0))
        nt = (total + (MOE_TM - 1)) // MOE_TM
        first = off // MOE_TM

        def fill(j, carry, e=e, first=first):
            texp_ref[first + j] = e
            return carry
        lax.fori_loop(0, nt, fill, 0)
        tail_ref[e] = off + total
        tail_ref[ne + e] = (nt * MOE_TM - total) // MOE_ALIGN
        off = off + nt * MOE_TM
    nvalid = off // MOE_TM

    def fill_rest(j, carry):
        texp_ref[j] = ne - 1
        return carry
    lax.fori_loop(nvalid, n_row_tiles, fill_rest, 0)
    meta_ref[0] = nvalid


def _plan(counts_flat, ntiles, n_row_tiles):
    smem = pl.BlockSpec(memory_space=pltpu.SMEM)
    return pl.pallas_call(
        functools.partial(_plan_kernel, ntiles=ntiles, n_row_tiles=n_row_tiles),
        in_specs=[smem],
        out_specs=[smem, smem, smem, smem],
        out_shape=[jax.ShapeDtypeStruct((ntiles * N_EXPERTS,), jnp.int32),
                   jax.ShapeDtypeStruct((n_row_tiles,), jnp.int32),
                   jax.ShapeDtypeStruct((1,), jnp.int32),
                   jax.ShapeDtypeStruct((2 * N_EXPERTS,), jnp.int32)],
        name="moe_plan",
    )(counts_flat)


def _run_chunks(c_ref, base_ref, tile, fn):
    ne = N_EXPERTS
    local = jnp.int32(0)
    for e in range(ne):
        c = c_ref[tile * 2 * ne + e] + c_ref[tile * 2 * ne + ne + e]
        nchunk = (c + (MOE_ALIGN - 1)) // MOE_ALIGN
        hbm = base_ref[tile * ne + e]

        def body(j, carry, hbm=hbm, local=local):
            fn(pl.multiple_of(hbm + j * MOE_ALIGN, MOE_ALIGN),
               pl.multiple_of(local + j * MOE_ALIGN, MOE_ALIGN))
            return carry
        lax.fori_loop(0, nchunk, body, 0)
        local = local + nchunk * MOE_ALIGN
    return local // MOE_ALIGN


def _dispatch_kernel(c_ref, base_ref, tail_ref, meta_ref, h2_ref, rows_ref, xs_ref,
                     xs_scr, zero_scr, sem, nch_ref, *, n_row_tiles):
    i = pl.program_id(0)
    n = pl.num_programs(0)
    slot = i % 2

    def copy(slot_, hbm, local):
        return pltpu.make_async_copy(xs_scr.at[slot_, pl.ds(local, MOE_ALIGN), :],
                                     xs_ref.at[pl.ds(hbm, MOE_ALIGN), :], sem.at[slot_])

    def wait_all(slot_):
        def body(j, carry):
            copy(slot_, 0, 0).wait()
            return carry
        lax.fori_loop(0, nch_ref[slot_], body, 0)

    @pl.when(i >= 2)
    def _():
        wait_all(slot)

    rows = rows_ref[0]
    slot_id = lax.broadcasted_iota(jnp.int32, (MOE_L, MOE_TD), 0)
    p1 = jnp.where(slot_id == rows[0:1, :].astype(jnp.int32), 1.0, 0.0)
    p2 = jnp.where(slot_id == rows[1:2, :].astype(jnp.int32), 1.0, 0.0)
    xs_scr[slot] = _dot((p1 + p2).astype(BF16), h2_ref[...]).astype(BF16)
    nch_ref[slot] = _run_chunks(c_ref, base_ref, i, lambda hbm, local: copy(slot, hbm, local).start())

    @pl.when(i == n - 1)
    def _():
        zero_scr[...] = jnp.zeros_like(zero_scr)

        def chunk(hbm):
            return pltpu.make_async_copy(zero_scr.at[pl.ds(0, MOE_ALIGN), :],
                                         xs_ref.at[pl.ds(hbm, MOE_ALIGN), :], sem.at[2])

        def tile(j):
            return pltpu.make_async_copy(zero_scr, xs_ref.at[pl.ds(j * MOE_TM, MOE_TM), :], sem.at[3])

        nchunks = jnp.int32(0)
        for e in range(N_EXPERTS):
            start = tail_ref[e]
            cnt = tail_ref[N_EXPERTS + e]

            def fill(j, carry, start=start):
                chunk(pl.multiple_of(start + j * MOE_ALIGN, MOE_ALIGN)).start()
                return carry
            lax.fori_loop(0, cnt, fill, 0)
            nchunks = nchunks + cnt

        def fill_tile(j, carry):
            tile(j).start()
            return carry
        lax.fori_loop(meta_ref[0], n_row_tiles, fill_tile, 0)

        wait_all(slot)

        @pl.when(n >= 2)
        def _():
            wait_all(1 - slot)

        def wait_chunk(j, carry):
            chunk(0).wait()
            return carry
        lax.fori_loop(0, nchunks, wait_chunk, 0)

        def wait_tile(j, carry):
            tile(0).wait()
            return carry
        lax.fori_loop(meta_ref[0], n_row_tiles, wait_tile, 0)


def _dispatch(counts_flat, base, tail, meta, h2, rows, n_rows):
    t, d = h2.shape
    ntiles = t // MOE_TD
    grid_spec = pltpu.PrefetchScalarGridSpec(
        num_scalar_prefetch=4,
        grid=(ntiles,),
        in_specs=[pl.BlockSpec((MOE_TD, d), lambda i, *_: (i, 0)),
                  pl.BlockSpec((1, SUBLANES, MOE_TD), lambda i, *_: (i, 0, 0))],
        out_specs=pl.BlockSpec(memory_space=pl.ANY),
        scratch_shapes=[pltpu.VMEM((2, MOE_L, d), BF16), pltpu.VMEM((MOE_TM, d), BF16),
                        pltpu.SemaphoreType.DMA((4,)), pltpu.SMEM((2,), jnp.int32)])
    return pl.pallas_call(
        functools.partial(_dispatch_kernel, n_row_tiles=n_rows // MOE_TM),
        grid_spec=grid_spec,
        out_shape=jax.ShapeDtypeStruct((n_rows, d), BF16),
        compiler_params=pltpu.CompilerParams(dimension_semantics=("arbitrary",),
                                             vmem_limit_bytes=VMEM_LIMIT),
        name="moe_dispatch",
    )(counts_flat, base, tail, meta, h2, rows)


def _experts_kernel(texp_ref, meta_ref, xs_ref, wg_ref, wu_ref, wd_ref, ys_ref):
    del texp_ref

    @pl.when(pl.program_id(0) >= meta_ref[0])
    def _():
        ys_ref[...] = jnp.zeros_like(ys_ref)

    @pl.when(pl.program_id(0) < meta_ref[0])
    def _():
        x = xs_ref[...]
        act = _silu(_dot(x, wg_ref[0])) * _dot(x, wu_ref[0])
        ys_ref[...] = _dot(act.astype(BF16), wd_ref[0]).astype(BF16)


def _experts(texp, meta, xs, wg, wu, wd):
    n_rows, d = xs.shape
    row = lambda w: pl.BlockSpec((MOE_TM, w), lambda i, te, mt: (jnp.minimum(i, mt[0] - 1), 0))
    wspec = lambda shape: pl.BlockSpec((1,) + shape, lambda i, te, mt: (te[i], 0, 0))
    grid_spec = pltpu.PrefetchScalarGridSpec(
        num_scalar_prefetch=2,
        grid=(n_rows // MOE_TM,),
        in_specs=[row(d), wspec((d, D_EXPERT)), wspec((d, D_EXPERT)), wspec((D_EXPERT, d))],
        out_specs=pl.BlockSpec((MOE_TM, d), lambda i, te, mt: (i, 0)))
    return pl.pallas_call(
        _experts_kernel,
        grid_spec=grid_spec,
        out_shape=jax.ShapeDtypeStruct((n_rows, d), BF16),
        compiler_params=pltpu.CompilerParams(dimension_semantics=("arbitrary",),
                                             vmem_limit_bytes=VMEM_LIMIT),
        name="moe_experts",
    )(texp, meta, xs, wg, wu, wd)


def _combine_kernel(c_ref, base_ref, x1_ref, route_ref, mod_ref, ys_ref, o_ref, ys_scr, sem,
                    nch_ref):
    i = pl.program_id(0)
    n = pl.num_programs(0)
    slot = i % 2

    def copy(slot_, hbm, local):
        return pltpu.make_async_copy(ys_ref.at[pl.ds(hbm, MOE_ALIGN), :],
                                     ys_scr.at[slot_, pl.ds(local, MOE_ALIGN), :], sem.at[slot_])

    def fetch(tile, slot_):
        nch_ref[slot_] = _run_chunks(c_ref, base_ref, tile,
                                     lambda hbm, local: copy(slot_, hbm, local).start())

    @pl.when(i == 0)
    def _():
        ys_scr[...] = jnp.zeros_like(ys_scr)
        fetch(0, 0)

    @pl.when(i + 1 < n)
    def _():
        fetch(i + 1, 1 - slot)

    route = route_ref[...]
    lane = lax.broadcasted_iota(jnp.int32, route.shape, 1)
    r1 = jnp.sum(jnp.where(lane == ROUTE_ROW_LANE, route, 0.0), axis=-1, keepdims=True)
    r2 = jnp.sum(jnp.where(lane == ROUTE_ROW_LANE + 1, route, 0.0), axis=-1, keepdims=True)
    r1, r2 = r1.astype(jnp.int32), r2.astype(jnp.int32)
    col = lax.broadcasted_iota(jnp.int32, (MOE_TD, MOE_L), 1)
    pick1 = jnp.where(col == r1, 1.0, 0.0).astype(BF16)
    pick2 = jnp.where(col == r2, 1.0, 0.0).astype(BF16)
    w1 = jnp.sum(jnp.where(lane == ROUTE_W_LANE, route, 0.0), axis=-1, keepdims=True)
    w2 = jnp.sum(jnp.where(lane == ROUTE_W_LANE + 1, route, 0.0), axis=-1, keepdims=True)

    def wait(j, carry):
        copy(slot, 0, 0).wait()
        return carry
    lax.fori_loop(0, nch_ref[slot], wait, 0)
    ys = ys_scr[slot]
    y = w1 * _dot(pick1, ys) + w2 * _dot(pick2, ys)
    o_ref[...] = x1_ref[...] + mod_ref[0, 5:6, :] * y


def _combine(counts_flat, base, x1, route, mod, ys, tiles_per_batch):
    t, d = x1.shape
    tok = lambda w: pl.BlockSpec((MOE_TD, w), lambda i, c, b: (i, 0))
    grid_spec = pltpu.PrefetchScalarGridSpec(
        num_scalar_prefetch=2,
        grid=(t // MOE_TD,),
        in_specs=[tok(d), tok(LANES),
                  pl.BlockSpec((1, 6, d), lambda i, c, b: (i // tiles_per_batch, 0, 0)),
                  pl.BlockSpec(memory_space=pl.ANY)],
        out_specs=tok(d),
        scratch_shapes=[pltpu.VMEM((2, MOE_L, d), BF16), pltpu.SemaphoreType.DMA((2,)),
                        pltpu.SMEM((2,), jnp.int32)])
    return pl.pallas_call(
        _combine_kernel,
        grid_spec=grid_spec,
        out_shape=jax.ShapeDtypeStruct((t, d), F32),
        compiler_params=pltpu.CompilerParams(dimension_semantics=("arbitrary",),
                                             vmem_limit_bytes=VMEM_LIMIT),
        name="moe_combine",
    )(counts_flat, base, x1, route, mod, ys)


def kernel(x, c, w_ada, b_ada, norm1_g, w_in, b_fox, q_norm_g, k_norm_g, attn_out_g, hgrn_lb,
           hgrn_out_g, w_out, norm2_g, w_router_group, b_router_group, w_router_expert,
           b_router_expert, w_gate, w_up, w_down):
    b, s, d = x.shape
    l = 0
    aw = ATTN_WIDTH
    mod = _ada(c, w_ada[l], b_ada[l]).reshape(b, 6, d)

    w = w_in[l]
    wq = w[:, 0:aw].astype(BF16)
    wk = w[:, aw:2 * aw].astype(BF16)
    wv = w[:, 2 * aw:3 * aw].astype(BF16)
    f0 = 3 * aw
    pad_f = LANES - FOX_PIECES * ATTN_HEADS
    wf = jnp.pad(jnp.tile(w[:, f0:f0 + ATTN_HEADS], (1, FOX_PIECES)), ((0, 0), (0, pad_f))).astype(BF16)
    wh = w[:, f0 + ATTN_HEADS:].astype(BF16)
    bfox = jnp.pad(jnp.tile(b_fox[l], FOX_PIECES), (0, pad_f)).reshape(1, LANES)
    gq = jnp.tile(q_norm_g[l], ATTN_HEADS).reshape(1, aw)
    gk = jnp.tile(k_norm_g[l], ATTN_HEADS).reshape(1, aw)

    tm = min(512, s)
    q, k, v, ck, hq, hf, hi, hg = _inproj(x, mod, norm1_g[l].reshape(1, d), wq, wk, wv, wf, wh,
                                           gq, gk, bfox, tm)
    ao = _fox(q, k, v, ck, attn_out_g[l].reshape(1, aw), min(256, s), 128)
    ho = _hgrn(hq, hf, hi, hg, hgrn_lb[0:2], hgrn_out_g[l].reshape(1, HGRN_WIDTH))

    wr = jnp.pad(jnp.concatenate([w_router_expert[l], w_router_group[l]], axis=1),
                 ((0, 0), (0, LANES - N_GROUPS - N_EXPERTS)))
    br = jnp.pad(jnp.concatenate([b_router_expert[l], b_router_group[l]]),
                 (0, LANES - N_GROUPS - N_EXPERTS)).reshape(1, LANES)
    wr_hi = wr.astype(BF16)
    wr2 = jnp.stack([wr_hi, (wr - wr_hi.astype(F32)).astype(BF16)])
    x1, h2, route, counts, rows = _outproj(x, ao, ho, mod, w_out[l].astype(BF16),
                                           norm2_g[l].reshape(1, d), wr2, br, tm)

    t = b * s
    ntiles = t // MOE_TD
    counts_flat = counts.reshape(ntiles, LANES)[:, :2 * N_EXPERTS].reshape(-1)
    n_rows = _ceil_to(2 * t + ntiles * N_EXPERTS * (MOE_ALIGN - 1), MOE_TM) + N_EXPERTS * MOE_TM
    base, texp, meta, tail = _plan(counts_flat, ntiles, n_rows // MOE_TM)
    route2 = route.reshape(t, LANES)
    xs = _dispatch(counts_flat, base, tail, meta, h2.reshape(t, d),
                   rows.reshape(ntiles, SUBLANES, MOE_TD), n_rows)
    ys = _experts(texp, meta, xs, w_gate[l].astype(BF16), w_up[l].astype(BF16),
                  w_down[l].astype(BF16))
    out = _combine(counts_flat, base, x1.reshape(t, d), route2, mod, ys, s // MOE_TD)
    return out.reshape(b, s, d)
```

```python
import functools

import numpy as np
import jax
import jax.numpy as jnp
from jax import lax
from jax.experimental import pallas as pl
from jax.experimental.pallas import tpu as pltpu

F32 = jnp.float32
BF16 = jnp.bfloat16

D_MODEL = 1024
ATTN_HEAD_DIM = 64
ATTN_WIDTH = 512
ATTN_HEADS = 8
HGRN_WIDTH = 512
HGRN_HEADS = 4
HGRN_DK = 128
N_GROUPS = 4
EXPERTS_PER_GROUP = 4
N_EXPERTS = 16
D_EXPERT = 512
NORM_EPS = 1e-6
LANES = 128
SUBLANES = 8
VMEM_LIMIT = 56 * 1024 * 1024

HGRN_CHUNK = 128
HGRN_STEP = 512
HGRN_LEVELS = (1, 2, 4, 8, 16, 32, 64)
HGRN_MXU_LEVELS = (1, 2)
ROUTE_W_LANE = 2 * N_EXPERTS
ROUTE_ROW_LANE = ROUTE_W_LANE + 2
MOE_TD = 256
MOE_ALIGN = 16
MOE_TM = 512
MOE_L = 2 * MOE_TD + MOE_ALIGN * N_EXPERTS
LOG2E = 1.4426950408889634
FOX_GROUP = 4
FOX_PIECES = 3


def _sigmoid(x):
    return 1.0 / (1.0 + jnp.exp(-x))


def _silu(x):
    return x * (0.5 * jnp.tanh(0.5 * x) + 0.5)


def _split3(x):
    p1 = x.astype(BF16)
    r1 = x - p1.astype(F32)
    p2 = r1.astype(BF16)
    p3 = (r1 - p2.astype(F32)).astype(BF16)
    return p1, p2, p3


def _dot(a, b):
    return jnp.dot(a, b, preferred_element_type=F32)


def _dot_nt(a, b):
    return lax.dot_general(a, b, (((1,), (1,)), ((), ())), preferred_element_type=F32)


def _ada_kernel(c_ref, w_ref, b_ref, o_ref):
    c = c_ref[...]
    o_ref[...] = jnp.dot(_silu(c), w_ref[...], preferred_element_type=F32,
                         precision=lax.Precision.HIGHEST) + b_ref[...]


def _ada(c, w, b):
    bsz, d = c.shape
    n = w.shape[1]
    tn = 1024
    return pl.pallas_call(
        _ada_kernel,
        grid=(n // tn,),
        in_specs=[pl.BlockSpec((bsz, d), lambda j: (0, 0)),
                  pl.BlockSpec((d, tn), lambda j: (0, j)),
                  pl.BlockSpec((1, tn), lambda j: (0, j))],
        out_specs=pl.BlockSpec((bsz, tn), lambda j: (0, j)),
        out_shape=jax.ShapeDtypeStruct((bsz, n), F32),
        compiler_params=pltpu.CompilerParams(dimension_semantics=("arbitrary",),
                                             vmem_limit_bytes=VMEM_LIMIT),
        name="ada",
    )(c, w, b.reshape(1, n))


def _inproj_kernel(x_ref, mod_ref, g1_ref, wq_ref, wk_ref, wv_ref, wf_ref, wh_ref,
                   gq_ref, gk_ref, bf_ref, gsum_ref, tri_ref, place_ref,
                   q_out, k_out, v_out, ck_out, hq_out, hf_out, hi_out, hg_out,
                   carry_ref):
    si = pl.program_id(1)

    @pl.when(si == 0)
    def _():
        carry_ref[...] = jnp.zeros_like(carry_ref)

    x = x_ref[0]
    shift = mod_ref[0, 0:1, :]
    scale = mod_ref[0, 1:2, :]
    ms = jnp.mean(x * x, axis=-1, keepdims=True)
    h = (x * lax.rsqrt(ms + NORM_EPS) * g1_ref[...]) * (1.0 + scale) + shift
    hb = h.astype(BF16)

    def qk_norm(w_ref, g_ref, mult):
        t = _dot(hb, w_ref[...])
        ssq = _dot((t * t).astype(BF16), gsum_ref[...])
        return t * lax.rsqrt(ssq * (1.0 / ATTN_HEAD_DIM) + NORM_EPS) * (g_ref[...] * mult)

    q_out[0] = qk_norm(wq_ref, gq_ref, ATTN_HEAD_DIM ** -0.5 * LOG2E).T.astype(BF16)
    k_out[0] = qk_norm(wk_ref, gk_ref, 1.0).astype(BF16)
    v_out[0] = _dot(hb, wv_ref[...]).T.astype(BF16)

    af = _dot(hb, wf_ref[...]) + bf_ref[...]
    lf = jnp.minimum(af, 0.0) - jnp.log(1.0 + jnp.exp(-jnp.abs(af)))
    tri = tri_ref[...]
    p1, p2, p3 = _split3(lf)
    cum = (_dot(tri, p1) + _dot(tri, p2)) + _dot(tri, p3) + carry_ref[...]
    tm = cum.shape[0]
    carry_ref[...] = cum[tm - 1:tm, :]
    c1, c2, c3 = _split3(cum * (-LOG2E))
    lane = lax.broadcasted_iota(jnp.int32, cum.shape, 1)
    zero = jnp.zeros_like(c1)
    pieces = jnp.where(lane < ATTN_HEADS, c1,
                       jnp.where(lane < 2 * ATTN_HEADS, c2,
                                 jnp.where(lane < 3 * ATTN_HEADS, c3, zero)))
    ck_out[0] = _dot(pieces, place_ref[...]).astype(BF16)

    hq_out[0] = _dot(hb, wh_ref[:, 0 * HGRN_WIDTH:1 * HGRN_WIDTH]).astype(BF16)
    hf_out[0] = _dot(hb, wh_ref[:, 1 * HGRN_WIDTH:2 * HGRN_WIDTH]).astype(BF16)
    hi_out[0] = _dot(hb, wh_ref[:, 2 * HGRN_WIDTH:3 * HGRN_WIDTH]).astype(BF16)
    hg_out[0] = _dot(hb, wh_ref[:, 3 * HGRN_WIDTH:4 * HGRN_WIDTH]).astype(BF16)


def _inproj(x, mod, g1, wq, wk, wv, wf, wh, gq, gk, bfox, tm):
    b, s, d = x.shape
    gsum = jnp.asarray(np.kron(np.eye(ATTN_HEADS), np.ones((ATTN_HEAD_DIM, ATTN_HEAD_DIM))), BF16)
    tri = jnp.asarray(np.tril(np.ones((tm, tm))), BF16)
    place_np = np.zeros((LANES, ATTN_HEADS * LANES), np.float32)
    for piece in range(FOX_PIECES):
        for hd in range(ATTN_HEADS):
            place_np[piece * ATTN_HEADS + hd, hd * LANES + piece] = 1.0
    place = jnp.asarray(place_np, BF16)
    const = lambda shape: pl.BlockSpec(shape, lambda bi, si: (0,) * len(shape))
    tok = lambda w: pl.BlockSpec((1, tm, w), lambda bi, si: (bi, si, 0))
    tok_t = lambda w: pl.BlockSpec((1, w, tm), lambda bi, si: (bi, 0, si))
    act = lambda w: jax.ShapeDtypeStruct((b, s, w), BF16)
    act_t = lambda w: jax.ShapeDtypeStruct((b, w, s), BF16)
    return pl.pallas_call(
        _inproj_kernel,
        grid=(b, s // tm),
        in_specs=[tok(d),
                  pl.BlockSpec((1, 6, d), lambda bi, si: (bi, 0, 0)),
                  const((1, d)),
                  const((d, ATTN_WIDTH)), const((d, ATTN_WIDTH)), const((d, ATTN_WIDTH)),
                  const((d, LANES)), const((d, 4 * HGRN_WIDTH)),
                  const((1, ATTN_WIDTH)), const((1, ATTN_WIDTH)), const((1, LANES)),
                  const((ATTN_WIDTH, ATTN_WIDTH)), const((tm, tm)),
                  const((LANES, ATTN_HEADS * LANES))],
        out_specs=[tok_t(ATTN_WIDTH), tok(ATTN_WIDTH), tok_t(ATTN_WIDTH),
                   tok(ATTN_HEADS * LANES),
                   tok(HGRN_WIDTH), tok(HGRN_WIDTH), tok(HGRN_WIDTH), tok(HGRN_WIDTH)],
        out_shape=[act_t(ATTN_WIDTH), act(ATTN_WIDTH), act_t(ATTN_WIDTH),
                   act(ATTN_HEADS * LANES),
                   act(HGRN_WIDTH), act(HGRN_WIDTH), act(HGRN_WIDTH), act(HGRN_WIDTH)],
        scratch_shapes=[pltpu.VMEM((1, LANES), F32)],
        compiler_params=pltpu.CompilerParams(dimension_semantics=("arbitrary", "arbitrary"),
                                             vmem_limit_bytes=VMEM_LIMIT),
        name="inproj",
    )(x, mod, g1, wq, wk, wv, wf, wh, gq, gk, bfox, gsum, tri, place)


def _fox_kernel(qt_ref, k_ref, vt_ref, ck_ref, g_ref, o_ref, st_scr, pt_scr, *, tq, tk):
    qi = pl.program_id(2)
    qt = qt_ref[0].astype(F32)
    row = lax.broadcasted_iota(jnp.int32, (LANES, tq), 0)
    first = row < ATTN_HEAD_DIM
    ones = jnp.where(row < FOX_PIECES, 1.0, 0.0)
    rhs = []
    for hd in range(FOX_GROUP):
        qp = qt[(hd // 2) * LANES:(hd // 2 + 1) * LANES, :]
        qh = jnp.where(first, qp, 0.0) if hd % 2 == 0 else jnp.where(first, 0.0, qp)
        rhs.append(jnp.concatenate([qh, ones], axis=0).astype(BF16))

    def stage_qk(j, slot):
        k0 = pl.multiple_of(j * tk, tk)
        kb = k_ref[0, pl.ds(k0, tk), :]
        ckb = ck_ref[0, pl.ds(k0, tk), :]
        for hd in range(FOX_GROUP):
            pr = slice((hd // 2) * LANES, (hd // 2 + 1) * LANES)
            lhs = jnp.concatenate([kb[:, pr], ckb[:, hd * LANES:(hd + 1) * LANES]], axis=1)
            st_scr[slot, hd] = _dot(lhs, rhs[hd])

    def stage_softmax(j, slot, stats, masked):
        out = []
        for hd in range(FOX_GROUP):
            m, l = stats[hd]
            st = st_scr[slot, hd]
            if masked:
                key = j * tk + lax.broadcasted_iota(jnp.int32, (tk, tq), 0)
                qry = qi * tq + lax.broadcasted_iota(jnp.int32, (tk, tq), 1)
                st = jnp.where(key <= qry, st, -jnp.inf)
            m_new = jnp.maximum(m, jnp.max(st, axis=0, keepdims=True))
            alpha = jnp.exp2(m - m_new)
            pt = jnp.exp2(st - m_new)
            pt_scr[slot, hd] = pt.astype(BF16)
            out.append(((m_new, alpha * l + jnp.sum(pt, axis=0, keepdims=True)), alpha))
        return tuple(o[0] for o in out), tuple(o[1] for o in out)

    def stage_pv(j, slot, accs, alphas):
        k0 = pl.multiple_of(jnp.maximum(j, 0) * tk, tk)
        vtb = vt_ref[0, :, pl.ds(k0, tk)]
        out = []
        for hd in range(FOX_GROUP):
            vth = vtb[hd * ATTN_HEAD_DIM:(hd + 1) * ATTN_HEAD_DIM, :]
            out.append(alphas[hd] * accs[hd] + _dot(vth, pt_scr[slot, hd]))
        return tuple(out)

    per_q = tq // tk
    stats = tuple((jnp.full((1, tq), -1e30, F32), jnp.zeros((1, tq), F32))
                  for _ in range(FOX_GROUP))
    accs = tuple(jnp.zeros((ATTN_HEAD_DIM, tq), F32) for _ in range(FOX_GROUP))
    alphas = tuple(jnp.ones((1, tq), F32) for _ in range(FOX_GROUP))
    pt_scr[1] = jnp.zeros_like(pt_scr[1])
    stage_qk(0, 0)

    def full_blocks(i, carry):
        stats, accs, alphas = carry
        for u in range(per_q):
            j = i * per_q + u
            accs = stage_pv(j - 1, 1 - u, accs, alphas)
            stats, alphas = stage_softmax(j, u, stats, False)
            stage_qk(j + 1, 1 - u)
        return stats, accs, alphas

    stats, accs, alphas = lax.fori_loop(0, qi, full_blocks, (stats, accs, alphas))
    j = qi * per_q
    accs = stage_pv(j - 1, 1, accs, alphas)
    stats, alphas = stage_softmax(j, 0, stats, True)
    stage_qk(j + 1, 1)
    accs = stage_pv(j, 0, accs, alphas)
    stats, alphas = stage_softmax(j + 1, 1, stats, True)
    accs = stage_pv(j + 1, 1, accs, alphas)
    carry = tuple((stats[hd][0], stats[hd][1], accs[hd]) for hd in range(FOX_GROUP))

    head0 = lax.broadcasted_iota(jnp.int32, (tq, LANES), 1) < ATTN_HEAD_DIM
    for pr in range(FOX_GROUP // 2):
        (_, l0, a0), (_, l1, a1) = carry[2 * pr], carry[2 * pr + 1]
        ot = jnp.concatenate([a0 * (1.0 / l0), a1 * (1.0 / l1)], axis=0)
        o = ot.T
        osq = o * o
        ss0 = jnp.sum(jnp.where(head0, osq, 0.0), axis=-1, keepdims=True)
        ss1 = jnp.sum(jnp.where(head0, 0.0, osq), axis=-1, keepdims=True)
        ms = jnp.where(head0, ss0, ss1) * (1.0 / ATTN_HEAD_DIM)
        sl = slice(pr * LANES, (pr + 1) * LANES)
        o_ref[0, :, sl] = (o * lax.rsqrt(ms + NORM_EPS) * g_ref[:, sl]).astype(BF16)


def _fox(qt, k, vt, ck, g_out, tq, tk):
    b, s, _ = k.shape
    groups = ATTN_HEADS // FOX_GROUP
    gw = FOX_GROUP * ATTN_HEAD_DIM
    return pl.pallas_call(
        functools.partial(_fox_kernel, tq=tq, tk=tk),
        grid=(b, groups, s // tq),
        in_specs=[pl.BlockSpec((1, gw, tq), lambda bi, g, qi: (bi, g, qi)),
                  pl.BlockSpec((1, s, gw), lambda bi, g, qi: (bi, 0, g)),
                  pl.BlockSpec((1, gw, s), lambda bi, g, qi: (bi, g, 0)),
                  pl.BlockSpec((1, s, FOX_GROUP * LANES), lambda bi, g, qi: (bi, 0, g)),
                  pl.BlockSpec((1, gw), lambda bi, g, qi: (0, g))],
        out_specs=pl.BlockSpec((1, tq, gw), lambda bi, g, qi: (bi, qi, g)),
        out_shape=jax.ShapeDtypeStruct((b, s, ATTN_WIDTH), BF16),
        scratch_shapes=[pltpu.VMEM((2, FOX_GROUP, tk, tq), F32),
                        pltpu.VMEM((2, FOX_GROUP, tk, tq), BF16)],
        compiler_params=pltpu.CompilerParams(
            dimension_semantics=("arbitrary", "arbitrary", "arbitrary"),
            vmem_limit_bytes=VMEM_LIMIT),
        name="fox",
    )(qt, k, vt, ck, g_out)


def _hgrn_decay_matrix(c):
    t = np.arange(c)[:, None]
    j = np.arange(c)[None, :]
    blocks = [(j <= t)]
    for m in HGRN_MXU_LEVELS:
        mid = (t // (2 * m)) * (2 * m) + m
        right = (t % (2 * m)) >= m
        blocks.append(np.where(right, (j >= mid) & (j <= t), (j > t) & (j < mid)))
    return np.concatenate(blocks, axis=0).astype(np.float32)


def _hgrn_kernel(hq_ref, hf_ref, hi_ref, hg_ref, lb_ref, g_ref, w_ref, lvl_ref, o_ref, st_ref):
    ci = pl.program_id(1)
    c = HGRN_CHUNK

    @pl.when(ci == 0)
    def _():
        st_ref[...] = jnp.zeros_like(st_ref)

    r0 = lb_ref[0:1, :]
    r1 = lb_ref[1:2, :]
    rmax = jnp.maximum(r0, r1)
    e0 = jnp.exp(r0 - rmax)
    lb = e0 / (e0 + jnp.exp(r1 - rmax))

    lvl = lvl_ref[...]
    at_level = [lvl == float(li) for li in range(len(HGRN_LEVELS))]
    for sub in range(hq_ref.shape[1] // c):
        rows = slice(sub * c, (sub + 1) * c)
        f = lb + (1.0 - lb) * _sigmoid(hf_ref[0, rows, :].astype(F32))
        g = jnp.log2(f)
        g1 = g.astype(BF16)
        g2 = (g - g1.astype(F32)).astype(BF16)
        w = w_ref[...]
        xw = _dot(w, g1) + _dot(w, g2)
        cum = xw[0:c]
        cum3 = cum.reshape(c // SUBLANES, SUBLANES, cum.shape[1])

        def level_sums(m):
            if m in HGRN_MXU_LEVELS:
                at = 1 + HGRN_MXU_LEVELS.index(m)
                return xw[at * c:(at + 1) * c]
            if 2 * m == SUBLANES:
                ref = cum3[:, m - 1:m, :]
            else:
                last = cum3[:, SUBLANES - 1:SUBLANES, :]
                per = 2 * m // SUBLANES
                pick = [(r // per) * per + per // 2 - 1 for r in range(c // SUBLANES)]
                ref = jnp.concatenate([last[p:p + 1] for p in pick], axis=0)
            return -jnp.abs(cum - jnp.broadcast_to(ref, cum3.shape).reshape(cum.shape))

        e_levels = [jnp.exp2(level_sums(m)) for m in HGRN_LEVELS]
        e_pre_all = jnp.exp2(cum)
        e_suf_all = jnp.exp2(cum[c - 1:c, :] - cum)
        q_all = _silu(hq_ref[0, rows, :].astype(F32))
        k_all = 1.0 - f
        v_all = hi_ref[0, rows, :].astype(F32)


        for hd in range(HGRN_HEADS):
            sl = slice(hd * HGRN_DK, (hd + 1) * HGRN_DK)
            q, k, v = q_all[:, sl], k_all[:, sl], v_all[:, sl]
            e_pre = e_pre_all[:, sl]
            e_suf = e_suf_all[:, sl]

            a = jnp.zeros((c, c), F32)
            for li in range(len(HGRN_LEVELS)):
                e = e_levels[li][:, sl]
                a = jnp.where(at_level[li], _dot_nt((q * e).astype(BF16), (k * e).astype(BF16)), a)
            vb = v.astype(BF16)
            out = _dot(a.astype(BF16), vb)

            out = out + jnp.sum(q * k, axis=-1, keepdims=True) * v

            st = st_ref[hd]
            out = out + _dot_nt((q * e_pre).astype(BF16), st.astype(BF16))
            kdec = (k * e_suf).astype(BF16)
            upd = lax.dot_general(vb, kdec, (((0,), (0,)), ((), ())), preferred_element_type=F32)
            st_ref[hd] = st * e_pre[c - 1:c, :] + upd

            ms = jnp.mean(out * out, axis=-1, keepdims=True)
            gate = _silu(hg_ref[0, rows, sl].astype(F32))
            o_ref[0, rows, sl] = (out * lax.rsqrt(ms + NORM_EPS) * g_ref[:, sl] * gate).astype(BF16)


def _hgrn(hq, hf, hi, hg, lb_rows, g_out):
    b, s, wd = hq.shape
    c = HGRN_CHUNK
    wmat = jnp.asarray(_hgrn_decay_matrix(c), BF16)
    tt = np.arange(c)[:, None]
    ss = np.arange(c)[None, :]
    lvl_np = np.full((c, c), -1.0, np.float32)
    for li, m in reversed(list(enumerate(HGRN_LEVELS))):
        lvl_np[(ss < tt) & (tt // (2 * m) == ss // (2 * m)) & (tt // m != ss // m)] = li
    lvl = jnp.asarray(lvl_np)
    tok = pl.BlockSpec((1, HGRN_STEP, wd), lambda bi, ci: (bi, ci, 0))
    return pl.pallas_call(
        _hgrn_kernel,
        grid=(b, s // HGRN_STEP),
        in_specs=[tok, tok, tok, tok,
                  pl.BlockSpec((2, wd), lambda bi, ci: (0, 0)),
                  pl.BlockSpec((1, wd), lambda bi, ci: (0, 0)),
                  pl.BlockSpec(wmat.shape, lambda bi, ci: (0, 0)),
                  pl.BlockSpec((c, c), lambda bi, ci: (0, 0))],
        out_specs=tok,
        out_shape=jax.ShapeDtypeStruct((b, s, wd), BF16),
        scratch_shapes=[pltpu.VMEM((HGRN_HEADS, HGRN_DK, HGRN_DK), F32)],
        compiler_params=pltpu.CompilerParams(dimension_semantics=("arbitrary", "arbitrary"),
                                             vmem_limit_bytes=VMEM_LIMIT),
        name="hgrn",
    )(hq, hf, hi, hg, lb_rows, g_out, wmat, lvl)


def _outproj_kernel(x_ref, ao_ref, ho_ref, mod_ref, wo_ref, g2_ref, wr_ref, br_ref,
                    lstrict_ref, fold_ref, upper_ref, selrows_ref,
                    x1_out, h2_out, route_out, cnt_out, rows_out, lg_scr):
    @pl.when(pl.program_id(0) == 0)
    def _():
        lg_scr[...] = jnp.zeros_like(lg_scr)

    prev_logits = lg_scr[...]
    gate1 = mod_ref[0, 2:3, :]
    shift2 = mod_ref[0, 3:4, :]
    scale2 = mod_ref[0, 4:5, :]
    mix = _dot(ao_ref[...], wo_ref[0:ATTN_WIDTH, :]) + _dot(ho_ref[...], wo_ref[ATTN_WIDTH:D_MODEL, :])
    x1 = x_ref[...] + gate1 * mix
    x1_out[...] = x1
    ms = jnp.mean(x1 * x1, axis=-1, keepdims=True)
    h2 = (x1 * lax.rsqrt(ms + NORM_EPS) * g2_ref[...]) * (1.0 + scale2) + shift2
    h2_out[...] = h2.astype(BF16)

    h2_hi = h2.astype(BF16)
    h2_lo = (h2 - h2_hi.astype(F32)).astype(BF16)
    lg_scr[...] = (_dot(h2_hi, wr_ref[0]) + (_dot(h2_hi, wr_ref[1]) + _dot(h2_lo, wr_ref[0]))) + br_ref[...]
    _route_tile(prev_logits, lstrict_ref, fold_ref, upper_ref, selrows_ref, route_out, cnt_out, rows_out)


def _route_tile(logits, lstrict_ref, fold_ref, upper_ref, selrows_ref, route_out, cnt_out, rows_out):
    tm = logits.shape[0]
    lane = lax.broadcasted_iota(jnp.int32, (tm, LANES), 1)
    neg = -jnp.inf
    is_group = (lane >= N_EXPERTS) & (lane < N_EXPERTS + N_GROUPS)
    gl = jnp.where(is_group, logits, neg)
    gmax = jnp.max(gl, axis=-1, keepdims=True)
    gsum = jnp.sum(jnp.exp(gl - gmax), axis=-1, keepdims=True)
    group_p = 1.0 / gsum
    gidx = jnp.min(jnp.where(is_group & (gl == gmax), lane, LANES), axis=-1, keepdims=True) - N_EXPERTS
    in_group = (lane < N_EXPERTS) & ((lane // EXPERTS_PER_GROUP) == gidx)
    el = jnp.where(in_group, logits, neg)
    top1 = jnp.max(el, axis=-1, keepdims=True)
    idx1 = jnp.min(jnp.where(in_group & (el == top1), lane, LANES), axis=-1, keepdims=True)
    el2 = jnp.where(lane == idx1, neg, el)
    top2 = jnp.max(el2, axis=-1, keepdims=True)
    idx2 = jnp.min(jnp.where(in_group & (lane != idx1) & (el2 == top2), lane, LANES),
                   axis=-1, keepdims=True)
    e2 = jnp.exp(top2 - top1)
    w1 = group_p / (1.0 + e2)
    w2 = group_p * e2 / (1.0 + e2)
    route = jnp.where(lane == idx1, 1.0,
                      jnp.where(lane == idx2 + N_EXPERTS, 1.0,
                                jnp.where(lane == ROUTE_W_LANE, w1,
                                          jnp.where(lane == ROUTE_W_LANE + 1, w2, 0.0))))
    sel = selrows_ref[...]
    ln = lax.broadcasted_iota(jnp.int32, (MOE_TD, LANES), 1)
    for sub in range(tm // MOE_TD):
        tile = slice(sub * MOE_TD, (sub + 1) * MOE_TD)
        rt = route[tile]
        cnt_out[0, sub:sub + 1, :] = jnp.sum(rt, axis=0, keepdims=True).astype(jnp.int32)
        v = _local_slots(rt, lstrict_ref[...], fold_ref[...], upper_ref[...])
        hi = jnp.floor(v * (1.0 / MOE_ALIGN))
        lo = v - hi * MOE_ALIGN
        rows_out[0, sub * SUBLANES:(sub + 1) * SUBLANES, :] = (
            _dot_nt(sel, hi.astype(BF16)) * MOE_ALIGN + _dot_nt(sel, lo.astype(BF16)))
        r1 = jnp.sum(jnp.where(ln < N_EXPERTS, v, 0.0), axis=-1, keepdims=True)
        r2 = jnp.sum(jnp.where(ln < N_EXPERTS, 0.0, v), axis=-1, keepdims=True)
        route_out[tile, :] = jnp.where(ln == ROUTE_ROW_LANE, r1,
                                       jnp.where(ln == ROUTE_ROW_LANE + 1, r2, rt))


def _outproj(x, ao, ho, mod, wo, g2, wr, br, tm, tiles_per_batch):
    t, d = x.shape
    n = t // tm
    sub = tm // MOE_TD
    const = lambda shape: pl.BlockSpec(shape, lambda i: (0,) * len(shape))
    cur = lambda w: pl.BlockSpec((tm, w), lambda i: (jnp.minimum(i, n - 1), 0))
    prev = lambda shape: pl.BlockSpec(shape, lambda i: (jnp.maximum(i - 1, 0),) + (0,) * (len(shape) - 1))
    return pl.pallas_call(
        _outproj_kernel,
        grid=(n + 1,),
        in_specs=[cur(d), cur(ATTN_WIDTH), cur(HGRN_WIDTH),
                  pl.BlockSpec((1, 6, d), lambda i: (jnp.minimum(i, n - 1) // tiles_per_batch, 0, 0)),
                  const((d, d)), const((1, d)), const((2, d, LANES)), const((1, LANES)),
                  const((MOE_TD, MOE_TD)), const((LANES, LANES)), const((LANES, LANES)),
                  const((SUBLANES, LANES))],
        out_specs=[cur(d), cur(d), prev((tm, LANES)), prev((1, sub, LANES)),
                   prev((1, sub * SUBLANES, MOE_TD))],
        out_shape=[jax.ShapeDtypeStruct((t, d), F32),
                   jax.ShapeDtypeStruct((t, d), BF16),
                   jax.ShapeDtypeStruct((t, LANES), F32),
                   jax.ShapeDtypeStruct((n, sub, LANES), jnp.int32),
                   jax.ShapeDtypeStruct((n, sub * SUBLANES, MOE_TD), F32)],
        scratch_shapes=[pltpu.VMEM((tm, LANES), F32)],
        compiler_params=pltpu.CompilerParams(dimension_semantics=("arbitrary",),
                                             vmem_limit_bytes=VMEM_LIMIT),
        name="outproj",
    )(x, ao, ho, mod, wo, g2, wr, br, *_moe_constants())


def _ceil_to(v, m):
    return ((v + (m - 1)) // m) * m


def _moe_constants():
    a = np.arange(LANES)
    ne = N_EXPERTS
    td = MOE_TD
    lstrict = np.tril(np.ones((td, td)), -1)
    fold = ((a[:, None] < 2 * ne) & (a[None, :] < 2 * ne) & (a[:, None] % ne == a[None, :] % ne))
    upper = ((a[:, None] < ne) & (a[None, :] < 2 * ne) & (a[:, None] < a[None, :] % ne))
    selrows = np.zeros((SUBLANES, LANES))
    selrows[0, :ne] = 1.0
    selrows[1, ne:2 * ne] = 1.0
    return tuple(jnp.asarray(m, BF16) for m in (lstrict, fold, upper, selrows))


def _local_slots(route, lstrict, fold, upper):
    lane = lax.broadcasted_iota(jnp.int32, route.shape, 1)
    member = jnp.where(lane < 2 * N_EXPERTS, route, 0.0)
    rank = _dot(_dot(lstrict, member.astype(BF16)).astype(BF16), fold)
    cnt = jnp.broadcast_to(jnp.sum(member, axis=0, keepdims=True), (SUBLANES, LANES))
    cnt = _dot(cnt.astype(BF16), fold)
    run = jnp.floor((cnt + (MOE_ALIGN - 1)) * (1.0 / MOE_ALIGN)) * MOE_ALIGN
    start = _dot(run.astype(BF16), upper)[0:1, :]
    return member * (start + rank)


def _plan_kernel(c_ref, base_ref, texp_ref, meta_ref, tail_ref, *, ntiles, n_row_tiles):
    ne = N_EXPERTS
    off = jnp.int32(0)
    for e in range(ne):
        def body(i, run, e=e, off=off):
            c = c_ref[i * 2 * ne + e] + c_ref[i * 2 * ne + ne + e]
            base_ref[i * ne + e] = off + run
            return run + _ceil_to(c, MOE_ALIGN)
        total = lax.fori_loop(0, ntiles, body, jnp.int32(0), unroll=8)
        nt = (total + (MOE_TM - 1)) // MOE_TM
        first = off // MOE_TM

        def fill(j, carry, e=e, first=first):
            texp_ref[first + j] = e
            return carry
        lax.fori_loop(0, nt, fill, 0)
        tail_ref[e] = off + total
        tail_ref[ne + e] = (nt * MOE_TM - total) // MOE_ALIGN
        off = off + nt * MOE_TM
    nvalid = off // MOE_TM

    def fill_rest(j, carry):
        texp_ref[j] = ne - 1
        return carry
    lax.fori_loop(nvalid, n_row_tiles, fill_rest, 0)
    meta_ref[0] = nvalid


def _plan(counts_flat, ntiles, n_row_tiles):
    smem = pl.BlockSpec(memory_space=pltpu.SMEM)
    return pl.pallas_call(
        functools.partial(_plan_kernel, ntiles=ntiles, n_row_tiles=n_row_tiles),
        in_specs=[smem],
        out_specs=[smem, smem, smem, smem],
        out_shape=[jax.ShapeDtypeStruct((ntiles * N_EXPERTS,), jnp.int32),
                   jax.ShapeDtypeStruct((n_row_tiles,), jnp.int32),
                   jax.ShapeDtypeStruct((1,), jnp.int32),
                   jax.ShapeDtypeStruct((2 * N_EXPERTS,), jnp.int32)],
        name="moe_plan",
    )(counts_flat)


def _run_chunks(c_ref, base_ref, tile, fn):
    ne = N_EXPERTS
    local = jnp.int32(0)
    for e in range(ne):
        c = c_ref[tile * 2 * ne + e] + c_ref[tile * 2 * ne + ne + e]
        nchunk = (c + (MOE_ALIGN - 1)) // MOE_ALIGN
        hbm = base_ref[tile * ne + e]

        def body(j, carry, hbm=hbm, local=local):
            fn(pl.multiple_of(hbm + j * MOE_ALIGN, MOE_ALIGN),
               pl.multiple_of(local + j * MOE_ALIGN, MOE_ALIGN))
            return carry
        lax.fori_loop(0, nchunk, body, 0)
        local = local + nchunk * MOE_ALIGN
    return local // MOE_ALIGN


def _dispatch_kernel(c_ref, base_ref, tail_ref, meta_ref, h2_ref, rows_ref, xs_ref,
                     xs_scr, zero_scr, sem, nch_ref, *, n_row_tiles):
    i = pl.program_id(0)
    n = pl.num_programs(0)
    slot = i % 2

    def copy(slot_, hbm, local):
        return pltpu.make_async_copy(xs_scr.at[slot_, pl.ds(local, MOE_ALIGN), :],
                                     xs_ref.at[pl.ds(hbm, MOE_ALIGN), :], sem.at[slot_])

    def wait_all(slot_):
        def body(j, carry):
            copy(slot_, 0, 0).wait()
            return carry
        lax.fori_loop(0, nch_ref[slot_], body, 0)

    @pl.when(i >= 2)
    def _():
        wait_all(slot)

    rows = rows_ref[0]
    slot_id = lax.broadcasted_iota(jnp.int32, (MOE_L, MOE_TD), 0)
    p1 = jnp.where(slot_id == rows[0:1, :].astype(jnp.int32), 1.0, 0.0)
    p2 = jnp.where(slot_id == rows[1:2, :].astype(jnp.int32), 1.0, 0.0)
    xs_scr[slot] = _dot((p1 + p2).astype(BF16), h2_ref[...]).astype(BF16)
    nch_ref[slot] = _run_chunks(c_ref, base_ref, i, lambda hbm, local: copy(slot, hbm, local).start())

    @pl.when(i == n - 1)
    def _():
        zero_scr[...] = jnp.zeros_like(zero_scr)

        def chunk(hbm):
            return pltpu.make_async_copy(zero_scr.at[pl.ds(0, MOE_ALIGN), :],
                                         xs_ref.at[pl.ds(hbm, MOE_ALIGN), :], sem.at[2])

        def tile(j):
            return pltpu.make_async_copy(zero_scr, xs_ref.at[pl.ds(j * MOE_TM, MOE_TM), :], sem.at[3])

        nchunks = jnp.int32(0)
        for e in range(N_EXPERTS):
            start = tail_ref[e]
            cnt = tail_ref[N_EXPERTS + e]

            def fill(j, carry, start=start):
                chunk(pl.multiple_of(start + j * MOE_ALIGN, MOE_ALIGN)).start()
                return carry
            lax.fori_loop(0, cnt, fill, 0)
            nchunks = nchunks + cnt

        def fill_tile(j, carry):
            tile(j).start()
            return carry
        lax.fori_loop(meta_ref[0], n_row_tiles, fill_tile, 0)

        wait_all(slot)

        @pl.when(n >= 2)
        def _():
            wait_all(1 - slot)

        def wait_chunk(j, carry):
            chunk(0).wait()
            return carry
        lax.fori_loop(0, nchunks, wait_chunk, 0)

        def wait_tile(j, carry):
            tile(0).wait()
            return carry
        lax.fori_loop(meta_ref[0], n_row_tiles, wait_tile, 0)


def _dispatch(counts_flat, base, tail, meta, h2, rows, n_rows):
    t, d = h2.shape
    ntiles = t // MOE_TD
    grid_spec = pltpu.PrefetchScalarGridSpec(
        num_scalar_prefetch=4,
        grid=(ntiles,),
        in_specs=[pl.BlockSpec((MOE_TD, d), lambda i, *_: (i, 0)),
                  pl.BlockSpec((1, SUBLANES, MOE_TD), lambda i, *_: (i, 0, 0))],
        out_specs=pl.BlockSpec(memory_space=pl.ANY),
        scratch_shapes=[pltpu.VMEM((2, MOE_L, d), BF16), pltpu.VMEM((MOE_TM, d), BF16),
                        pltpu.SemaphoreType.DMA((4,)), pltpu.SMEM((2,), jnp.int32)])
    return pl.pallas_call(
        functools.partial(_dispatch_kernel, n_row_tiles=n_rows // MOE_TM),
        grid_spec=grid_spec,
        out_shape=jax.ShapeDtypeStruct((n_rows, d), BF16),
        compiler_params=pltpu.CompilerParams(dimension_semantics=("arbitrary",),
                                             vmem_limit_bytes=VMEM_LIMIT),
        name="moe_dispatch",
    )(counts_flat, base, tail, meta, h2, rows)


def _experts_kernel(texp_ref, meta_ref, xs_ref, wg_ref, wu_ref, wd_ref, ys_ref):
    del texp_ref

    @pl.when(pl.program_id(0) >= meta_ref[0])
    def _():
        ys_ref[...] = jnp.zeros_like(ys_ref)

    @pl.when(pl.program_id(0) < meta_ref[0])
    def _():
        x = xs_ref[...]
        act = _silu(_dot(x, wg_ref[0])) * _dot(x, wu_ref[0])
        ys_ref[...] = _dot(act.astype(BF16), wd_ref[0]).astype(BF16)


def _experts(texp, meta, xs, wg, wu, wd):
    n_rows, d = xs.shape
    row = lambda w: pl.BlockSpec((MOE_TM, w), lambda i, te, mt: (jnp.minimum(i, mt[0] - 1), 0))
    wspec = lambda shape: pl.BlockSpec((1,) + shape, lambda i, te, mt: (te[i], 0, 0))
    grid_spec = pltpu.PrefetchScalarGridSpec(
        num_scalar_prefetch=2,
        grid=(n_rows // MOE_TM,),
        in_specs=[row(d), wspec((d, D_EXPERT)), wspec((d, D_EXPERT)), wspec((D_EXPERT, d))],
        out_specs=pl.BlockSpec((MOE_TM, d), lambda i, te, mt: (i, 0)))
    return pl.pallas_call(
        _experts_kernel,
        grid_spec=grid_spec,
        out_shape=jax.ShapeDtypeStruct((n_rows, d), BF16),
        compiler_params=pltpu.CompilerParams(dimension_semantics=("arbitrary",),
                                             vmem_limit_bytes=VMEM_LIMIT),
        name="moe_experts",
    )(texp, meta, xs, wg, wu, wd)


def _combine_kernel(c_ref, base_ref, x1_ref, route_ref, mod_ref, ys_ref, o_ref, ys_scr, sem,
                    nch_ref):
    i = pl.program_id(0)
    n = pl.num_programs(0)
    slot = i % 2

    def copy(slot_, hbm, local):
        return pltpu.make_async_copy(ys_ref.at[pl.ds(hbm, MOE_ALIGN), :],
                                     ys_scr.at[slot_, pl.ds(local, MOE_ALIGN), :], sem.at[slot_])

    def fetch(tile, slot_):
        nch_ref[slot_] = _run_chunks(c_ref, base_ref, tile,
                                     lambda hbm, local: copy(slot_, hbm, local).start())

    @pl.when(i == 0)
    def _():
        ys_scr[...] = jnp.zeros_like(ys_scr)
        fetch(0, 0)

    @pl.when(i + 1 < n)
    def _():
        fetch(i + 1, 1 - slot)

    route = route_ref[...]
    lane = lax.broadcasted_iota(jnp.int32, route.shape, 1)
    r1 = jnp.sum(jnp.where(lane == ROUTE_ROW_LANE, route, 0.0), axis=-1, keepdims=True)
    r2 = jnp.sum(jnp.where(lane == ROUTE_ROW_LANE + 1, route, 0.0), axis=-1, keepdims=True)
    r1, r2 = r1.astype(jnp.int32), r2.astype(jnp.int32)
    col = lax.broadcasted_iota(jnp.int32, (MOE_TD, MOE_L), 1)
    pick1 = jnp.where(col == r1, 1.0, 0.0).astype(BF16)
    pick2 = jnp.where(col == r2, 1.0, 0.0).astype(BF16)
    w1 = jnp.sum(jnp.where(lane == ROUTE_W_LANE, route, 0.0), axis=-1, keepdims=True)
    w2 = jnp.sum(jnp.where(lane == ROUTE_W_LANE + 1, route, 0.0), axis=-1, keepdims=True)

    def wait(j, carry):
        copy(slot, 0, 0).wait()
        return carry
    lax.fori_loop(0, nch_ref[slot], wait, 0)
    ys = ys_scr[slot]
    y = w1 * _dot(pick1, ys) + w2 * _dot(pick2, ys)
    o_ref[...] = x1_ref[...] + mod_ref[0, 5:6, :] * y


def _combine(counts_flat, base, x1, route, mod, ys, tiles_per_batch):
    t, d = x1.shape
    tok = lambda w: pl.BlockSpec((MOE_TD, w), lambda i, c, b: (i, 0))
    grid_spec = pltpu.PrefetchScalarGridSpec(
        num_scalar_prefetch=2,
        grid=(t // MOE_TD,),
        in_specs=[tok(d), tok(LANES),
                  pl.BlockSpec((1, 6, d), lambda i, c, b: (i // tiles_per_batch, 0, 0)),
                  pl.BlockSpec(memory_space=pl.ANY)],
        out_specs=tok(d),
        scratch_shapes=[pltpu.VMEM((2, MOE_L, d), BF16), pltpu.SemaphoreType.DMA((2,)),
                        pltpu.SMEM((2,), jnp.int32)])
    return pl.pallas_call(
        _combine_kernel,
        grid_spec=grid_spec,
        out_shape=jax.ShapeDtypeStruct((t, d), F32),
        compiler_params=pltpu.CompilerParams(dimension_semantics=("arbitrary",),
                                             vmem_limit_bytes=VMEM_LIMIT),
        name="moe_combine",
    )(counts_flat, base, x1, route, mod, ys)


def kernel(x, c, w_ada, b_ada, norm1_g, w_in, b_fox, q_norm_g, k_norm_g, attn_out_g, hgrn_lb,
           hgrn_out_g, w_out, norm2_g, w_router_group, b_router_group, w_router_expert,
           b_router_expert, w_gate, w_up, w_down):
    b, s, d = x.shape
    l = 0
    aw = ATTN_WIDTH
    mod = _ada(c, w_ada[l], b_ada[l]).reshape(b, 6, d)

    w = w_in[l]
    wq = w[:, 0:aw].astype(BF16)
    wk = w[:, aw:2 * aw].astype(BF16)
    wv = w[:, 2 * aw:3 * aw].astype(BF16)
    f0 = 3 * aw
    pad_f = LANES - FOX_PIECES * ATTN_HEADS
    wf = jnp.pad(jnp.tile(w[:, f0:f0 + ATTN_HEADS], (1, FOX_PIECES)), ((0, 0), (0, pad_f))).astype(BF16)
    wh = w[:, f0 + ATTN_HEADS:].astype(BF16)
    bfox = jnp.pad(jnp.tile(b_fox[l], FOX_PIECES), (0, pad_f)).reshape(1, LANES)
    gq = jnp.tile(q_norm_g[l], ATTN_HEADS).reshape(1, aw)
    gk = jnp.tile(k_norm_g[l], ATTN_HEADS).reshape(1, aw)

    tm = min(512, s)
    q, k, v, ck, hq, hf, hi, hg = _inproj(x, mod, norm1_g[l].reshape(1, d), wq, wk, wv, wf, wh,
                                           gq, gk, bfox, tm)
    ao = _fox(q, k, v, ck, attn_out_g[l].reshape(1, aw), min(256, s), 128)
    ho = _hgrn(hq, hf, hi, hg, hgrn_lb[0:2], hgrn_out_g[l].reshape(1, HGRN_WIDTH))

    wr = jnp.pad(jnp.concatenate([w_router_expert[l], w_router_group[l]], axis=1),
                 ((0, 0), (0, LANES - N_GROUPS - N_EXPERTS)))
    br = jnp.pad(jnp.concatenate([b_router_expert[l], b_router_group[l]]),
                 (0, LANES - N_GROUPS - N_EXPERTS)).reshape(1, LANES)
    wr_hi = wr.astype(BF16)
    wr2 = jnp.stack([wr_hi, (wr - wr_hi.astype(F32)).astype(BF16)])
    t = b * s
    x1, h2, route2, counts, rows = _outproj(x.reshape(t, d), ao.reshape(t, ATTN_WIDTH),
                                            ho.reshape(t, HGRN_WIDTH), mod, w_out[l].astype(BF16),
                                            norm2_g[l].reshape(1, d), wr2, br, tm, s // tm)
    ntiles = t // MOE_TD
    counts_flat = counts.reshape(ntiles, LANES)[:, :2 * N_EXPERTS].reshape(-1)
    n_rows = _ceil_to(2 * t + ntiles * N_EXPERTS * (MOE_ALIGN - 1), MOE_TM) + N_EXPERTS * MOE_TM
    base, texp, meta, tail = _plan(counts_flat, ntiles, n_rows // MOE_TM)
    xs = _dispatch(counts_flat, base, tail, meta, h2,
                   rows.reshape(ntiles, SUBLANES, MOE_TD), n_rows)
    ys = _experts(texp, meta, xs, w_gate[l].astype(BF16), w_up[l].astype(BF16),
                  w_down[l].astype(BF16))
    out = _combine(counts_flat, base, x1, route2, mod, ys, s // MOE_TD)
    return out.reshape(b, s, d)
```

```python
import functools

import numpy as np
import jax
import jax.numpy as jnp
from jax import lax
from jax.experimental import pallas as pl
from jax.experimental.pallas import tpu as pltpu

F32 = jnp.float32
BF16 = jnp.bfloat16

D_MODEL = 1024
ATTN_HEAD_DIM = 64
ATTN_WIDTH = 512
ATTN_HEADS = 8
HGRN_WIDTH = 512
HGRN_HEADS = 4
HGRN_DK = 128
N_GROUPS = 4
EXPERTS_PER_GROUP = 4
N_EXPERTS = 16
D_EXPERT = 512
NORM_EPS = 1e-6
LANES = 128
SUBLANES = 8
VMEM_LIMIT = 56 * 1024 * 1024

HGRN_CHUNK = 128
HGRN_STEP = 512
HGRN_LEVELS = (1, 2, 4, 8, 16, 32, 64)
HGRN_MXU_LEVELS = (1, 2)
ROUTE_W_LANE = 2 * N_EXPERTS
ROUTE_ROW_LANE = ROUTE_W_LANE + 2
MOE_TD = 256
MOE_ALIGN = 16
MOE_TM = 512
MOE_L = 2 * MOE_TD + MOE_ALIGN * N_EXPERTS
LOG2E = 1.4426950408889634
FOX_GROUP = 4
FOX_PIECES = 3


def _sigmoid(x):
    return 1.0 / (1.0 + jnp.exp(-x))


def _silu(x):
    return x * (0.5 * jnp.tanh(0.5 * x) + 0.5)


def _split3(x):
    p1 = x.astype(BF16)
    r1 = x - p1.astype(F32)
    p2 = r1.astype(BF16)
    p3 = (r1 - p2.astype(F32)).astype(BF16)
    return p1, p2, p3


def _dot(a, b):
    return jnp.dot(a, b, preferred_element_type=F32)


def _dot_nt(a, b):
    return lax.dot_general(a, b, (((1,), (1,)), ((), ())), preferred_element_type=F32)


def _ada_kernel(c_ref, w_ref, b_ref, o_ref):
    c = c_ref[...]
    o_ref[...] = jnp.dot(_silu(c), w_ref[...], preferred_element_type=F32,
                         precision=lax.Precision.HIGHEST) + b_ref[...]


def _ada(c, w, b):
    bsz, d = c.shape
    n = w.shape[1]
    tn = 1024
    return pl.pallas_call(
        _ada_kernel,
        grid=(n // tn,),
        in_specs=[pl.BlockSpec((bsz, d), lambda j: (0, 0)),
                  pl.BlockSpec((d, tn), lambda j: (0, j)),
                  pl.BlockSpec((1, tn), lambda j: (0, j))],
        out_specs=pl.BlockSpec((bsz, tn), lambda j: (0, j)),
        out_shape=jax.ShapeDtypeStruct((bsz, n), F32),
        compiler_params=pltpu.CompilerParams(dimension_semantics=("arbitrary",),
                                             vmem_limit_bytes=VMEM_LIMIT),
        name="ada",
    )(c, w, b.reshape(1, n))


def _inproj_kernel(x_ref, mod_ref, g1_ref, wq_ref, wk_ref, wv_ref, wf_ref, wh_ref,
                   gq_ref, gk_ref, bf_ref, gsum_ref, tri_ref, place_ref,
                   q_out, k_out, v_out, ck_out, hq_out, hf_out, hi_out, hg_out,
                   carry_ref):
    si = pl.program_id(1)

    @pl.when(si == 0)
    def _():
        carry_ref[...] = jnp.zeros_like(carry_ref)

    x = x_ref[0]
    shift = mod_ref[0, 0:1, :]
    scale = mod_ref[0, 1:2, :]
    ms = jnp.mean(x * x, axis=-1, keepdims=True)
    h = (x * lax.rsqrt(ms + NORM_EPS) * g1_ref[...]) * (1.0 + scale) + shift
    hb = h.astype(BF16)

    def qk_norm(w_ref, g_ref, mult):
        t = _dot(hb, w_ref[...])
        ssq = _dot((t * t).astype(BF16), gsum_ref[...])
        return t * lax.rsqrt(ssq * (1.0 / ATTN_HEAD_DIM) + NORM_EPS) * (g_ref[...] * mult)

    q_out[0] = qk_norm(wq_ref, gq_ref, ATTN_HEAD_DIM ** -0.5 * LOG2E).T.astype(BF16)
    k_out[0] = qk_norm(wk_ref, gk_ref, 1.0).astype(BF16)
    v_out[0] = _dot(hb, wv_ref[...]).T.astype(BF16)

    af = _dot(hb, wf_ref[...]) + bf_ref[...]
    lf = jnp.minimum(af, 0.0) - jnp.log(1.0 + jnp.exp(-jnp.abs(af)))
    tri = tri_ref[...]
    p1, p2, p3 = _split3(lf)
    cum = (_dot(tri, p1) + _dot(tri, p2)) + _dot(tri, p3) + carry_ref[...]
    tm = cum.shape[0]
    carry_ref[...] = cum[tm - 1:tm, :]
    c1, c2, c3 = _split3(cum * (-LOG2E))
    lane = lax.broadcasted_iota(jnp.int32, cum.shape, 1)
    zero = jnp.zeros_like(c1)
    pieces = jnp.where(lane < ATTN_HEADS, c1,
                       jnp.where(lane < 2 * ATTN_HEADS, c2,
                                 jnp.where(lane < 3 * ATTN_HEADS, c3, zero)))
    ck_out[0] = _dot(pieces, place_ref[...]).astype(BF16)

    hq_out[0] = _dot(hb, wh_ref[:, 0 * HGRN_WIDTH:1 * HGRN_WIDTH]).astype(BF16)
    hf_out[0] = _dot(hb, wh_ref[:, 1 * HGRN_WIDTH:2 * HGRN_WIDTH]).astype(BF16)
    hi_out[0] = _dot(hb, wh_ref[:, 2 * HGRN_WIDTH:3 * HGRN_WIDTH]).astype(BF16)
    hg_out[0] = _dot(hb, wh_ref[:, 3 * HGRN_WIDTH:4 * HGRN_WIDTH]).astype(BF16)


def _inproj(x, mod, g1, wq, wk, wv, wf, wh, gq, gk, bfox, tm):
    b, s, d = x.shape
    gsum = jnp.asarray(np.kron(np.eye(ATTN_HEADS), np.ones((ATTN_HEAD_DIM, ATTN_HEAD_DIM))), BF16)
    tri = jnp.asarray(np.tril(np.ones((tm, tm))), BF16)
    place_np = np.zeros((LANES, ATTN_HEADS * LANES), np.float32)
    for piece in range(FOX_PIECES):
        for hd in range(ATTN_HEADS):
            place_np[piece * ATTN_HEADS + hd, hd * LANES + piece] = 1.0
    place = jnp.asarray(place_np, BF16)
    const = lambda shape: pl.BlockSpec(shape, lambda bi, si: (0,) * len(shape))
    tok = lambda w: pl.BlockSpec((1, tm, w), lambda bi, si: (bi, si, 0))
    tok_t = lambda w: pl.BlockSpec((1, w, tm), lambda bi, si: (bi, 0, si))
    act = lambda w: jax.ShapeDtypeStruct((b, s, w), BF16)
    act_t = lambda w: jax.ShapeDtypeStruct((b, w, s), BF16)
    return pl.pallas_call(
        _inproj_kernel,
        grid=(b, s // tm),
        in_specs=[tok(d),
                  pl.BlockSpec((1, 6, d), lambda bi, si: (bi, 0, 0)),
                  const((1, d)),
                  const((d, ATTN_WIDTH)), const((d, ATTN_WIDTH)), const((d, ATTN_WIDTH)),
                  const((d, LANES)), const((d, 4 * HGRN_WIDTH)),
                  const((1, ATTN_WIDTH)), const((1, ATTN_WIDTH)), const((1, LANES)),
                  const((ATTN_WIDTH, ATTN_WIDTH)), const((tm, tm)),
                  const((LANES, ATTN_HEADS * LANES))],
        out_specs=[tok_t(ATTN_WIDTH), tok(ATTN_WIDTH), tok_t(ATTN_WIDTH),
                   tok(ATTN_HEADS * LANES),
                   tok(HGRN_WIDTH), tok(HGRN_WIDTH), tok(HGRN_WIDTH), tok(HGRN_WIDTH)],
        out_shape=[act_t(ATTN_WIDTH), act(ATTN_WIDTH), act_t(ATTN_WIDTH),
                   act(ATTN_HEADS * LANES),
                   act(HGRN_WIDTH), act(HGRN_WIDTH), act(HGRN_WIDTH), act(HGRN_WIDTH)],
        scratch_shapes=[pltpu.VMEM((1, LANES), F32)],
        compiler_params=pltpu.CompilerParams(dimension_semantics=("arbitrary", "arbitrary"),
                                             vmem_limit_bytes=VMEM_LIMIT),
        name="inproj",
    )(x, mod, g1, wq, wk, wv, wf, wh, gq, gk, bfox, gsum, tri, place)


def _fox_kernel(qt_ref, k_ref, vt_ref, ck_ref, g_ref, o_ref, st_scr, pt_scr, *, tq, tk):
    qi = pl.program_id(2)
    qt = qt_ref[0].astype(F32)
    row = lax.broadcasted_iota(jnp.int32, (LANES, tq), 0)
    first = row < ATTN_HEAD_DIM
    ones = jnp.where(row < FOX_PIECES, 1.0, 0.0)
    rhs = []
    for hd in range(FOX_GROUP):
        qp = qt[(hd // 2) * LANES:(hd // 2 + 1) * LANES, :]
        qh = jnp.where(first, qp, 0.0) if hd % 2 == 0 else jnp.where(first, 0.0, qp)
        rhs.append(jnp.concatenate([qh, ones], axis=0).astype(BF16))

    def stage_qk(j, slot):
        k0 = pl.multiple_of(j * tk, tk)
        kb = k_ref[0, pl.ds(k0, tk), :]
        ckb = ck_ref[0, pl.ds(k0, tk), :]
        for hd in range(FOX_GROUP):
            pr = slice((hd // 2) * LANES, (hd // 2 + 1) * LANES)
            lhs = jnp.concatenate([kb[:, pr], ckb[:, hd * LANES:(hd + 1) * LANES]], axis=1)
            st_scr[slot, hd] = _dot(lhs, rhs[hd])

    def stage_softmax(j, slot, stats, masked):
        out = []
        for hd in range(FOX_GROUP):
            m, l = stats[hd]
            st = st_scr[slot, hd]
            if masked:
                key = j * tk + lax.broadcasted_iota(jnp.int32, (tk, tq), 0)
                qry = qi * tq + lax.broadcasted_iota(jnp.int32, (tk, tq), 1)
                st = jnp.where(key <= qry, st, -jnp.inf)
            m_new = jnp.maximum(m, jnp.max(st, axis=0, keepdims=True))
            alpha = jnp.exp2(m - m_new)
            pt = jnp.exp2(st - m_new)
            pt_scr[slot, hd] = pt.astype(BF16)
            out.append(((m_new, alpha * l + jnp.sum(pt, axis=0, keepdims=True)), alpha))
        return tuple(o[0] for o in out), tuple(o[1] for o in out)

    def stage_pv(j, slot, accs, alphas):
        k0 = pl.multiple_of(jnp.maximum(j, 0) * tk, tk)
        vtb = vt_ref[0, :, pl.ds(k0, tk)]
        out = []
        for hd in range(FOX_GROUP):
            vth = vtb[hd * ATTN_HEAD_DIM:(hd + 1) * ATTN_HEAD_DIM, :]
            out.append(alphas[hd] * accs[hd] + _dot(vth, pt_scr[slot, hd]))
        return tuple(out)

    per_q = tq // tk
    stats = tuple((jnp.full((1, tq), -1e30, F32), jnp.zeros((1, tq), F32))
                  for _ in range(FOX_GROUP))
    accs = tuple(jnp.zeros((ATTN_HEAD_DIM, tq), F32) for _ in range(FOX_GROUP))
    alphas = tuple(jnp.ones((1, tq), F32) for _ in range(FOX_GROUP))
    pt_scr[1] = jnp.zeros_like(pt_scr[1])
    stage_qk(0, 0)

    def full_blocks(i, carry):
        stats, accs, alphas = carry
        for u in range(per_q):
            j = i * per_q + u
            accs = stage_pv(j - 1, 1 - u, accs, alphas)
            stats, alphas = stage_softmax(j, u, stats, False)
            stage_qk(j + 1, 1 - u)
        return stats, accs, alphas

    stats, accs, alphas = lax.fori_loop(0, qi, full_blocks, (stats, accs, alphas))
    j = qi * per_q
    accs = stage_pv(j - 1, 1, accs, alphas)
    stats, alphas = stage_softmax(j, 0, stats, True)
    stage_qk(j + 1, 1)
    accs = stage_pv(j, 0, accs, alphas)
    stats, alphas = stage_softmax(j + 1, 1, stats, True)
    accs = stage_pv(j + 1, 1, accs, alphas)
    carry = tuple((stats[hd][0], stats[hd][1], accs[hd]) for hd in range(FOX_GROUP))

    head0 = lax.broadcasted_iota(jnp.int32, (tq, LANES), 1) < ATTN_HEAD_DIM
    for pr in range(FOX_GROUP // 2):
        (_, l0, a0), (_, l1, a1) = carry[2 * pr], carry[2 * pr + 1]
        ot = jnp.concatenate([a0 * (1.0 / l0), a1 * (1.0 / l1)], axis=0)
        o = ot.T
        osq = o * o
        ss0 = jnp.sum(jnp.where(head0, osq, 0.0), axis=-1, keepdims=True)
        ss1 = jnp.sum(jnp.where(head0, 0.0, osq), axis=-1, keepdims=True)
        ms = jnp.where(head0, ss0, ss1) * (1.0 / ATTN_HEAD_DIM)
        sl = slice(pr * LANES, (pr + 1) * LANES)
        o_ref[0, :, sl] = (o * lax.rsqrt(ms + NORM_EPS) * g_ref[:, sl]).astype(BF16)


def _fox(qt, k, vt, ck, g_out, tq, tk):
    b, s, _ = k.shape
    groups = ATTN_HEADS // FOX_GROUP
    gw = FOX_GROUP * ATTN_HEAD_DIM
    return pl.pallas_call(
        functools.partial(_fox_kernel, tq=tq, tk=tk),
        grid=(b, groups, s // tq),
        in_specs=[pl.BlockSpec((1, gw, tq), lambda bi, g, qi: (bi, g, qi)),
                  pl.BlockSpec((1, s, gw), lambda bi, g, qi: (bi, 0, g)),
                  pl.BlockSpec((1, gw, s), lambda bi, g, qi: (bi, g, 0)),
                  pl.BlockSpec((1, s, FOX_GROUP * LANES), lambda bi, g, qi: (bi, 0, g)),
                  pl.BlockSpec((1, gw), lambda bi, g, qi: (0, g))],
        out_specs=pl.BlockSpec((1, tq, gw), lambda bi, g, qi: (bi, qi, g)),
        out_shape=jax.ShapeDtypeStruct((b, s, ATTN_WIDTH), BF16),
        scratch_shapes=[pltpu.VMEM((2, FOX_GROUP, tk, tq), F32),
                        pltpu.VMEM((2, FOX_GROUP, tk, tq), BF16)],
        compiler_params=pltpu.CompilerParams(
            dimension_semantics=("arbitrary", "arbitrary", "arbitrary"),
            vmem_limit_bytes=VMEM_LIMIT),
        name="fox",
    )(qt, k, vt, ck, g_out)


def _hgrn_decay_matrix(c):
    t = np.arange(c)[:, None]
    j = np.arange(c)[None, :]
    blocks = [(j <= t)]
    for m in HGRN_MXU_LEVELS:
        mid = (t // (2 * m)) * (2 * m) + m
        right = (t % (2 * m)) >= m
        blocks.append(np.where(right, (j >= mid) & (j <= t), (j > t) & (j < mid)))
    return np.concatenate(blocks, axis=0).astype(np.float32)


def _hgrn_kernel(hq_ref, hf_ref, hi_ref, hg_ref, lb_ref, g_ref, w_ref, lvl_ref, o_ref, st_ref):
    ci = pl.program_id(1)
    c = HGRN_CHUNK

    @pl.when(ci == 0)
    def _():
        st_ref[...] = jnp.zeros_like(st_ref)

    r0 = lb_ref[0:1, :]
    r1 = lb_ref[1:2, :]
    rmax = jnp.maximum(r0, r1)
    e0 = jnp.exp(r0 - rmax)
    lb = e0 / (e0 + jnp.exp(r1 - rmax))

    lvl = lvl_ref[...]
    at_level = [lvl == float(li) for li in range(len(HGRN_LEVELS))]
    for sub in range(hq_ref.shape[1] // c):
        rows = slice(sub * c, (sub + 1) * c)
        f = lb + (1.0 - lb) * _sigmoid(hf_ref[0, rows, :].astype(F32))
        g = jnp.log2(f)
        g1 = g.astype(BF16)
        g2 = (g - g1.astype(F32)).astype(BF16)
        w = w_ref[...]
        xw = _dot(w, g1) + _dot(w, g2)
        cum = xw[0:c]
        cum3 = cum.reshape(c // SUBLANES, SUBLANES, cum.shape[1])

        def level_sums(m):
            if m in HGRN_MXU_LEVELS:
                at = 1 + HGRN_MXU_LEVELS.index(m)
                return xw[at * c:(at + 1) * c]
            if 2 * m == SUBLANES:
                ref = cum3[:, m - 1:m, :]
            else:
                last = cum3[:, SUBLANES - 1:SUBLANES, :]
                per = 2 * m // SUBLANES
                pick = [(r // per) * per + per // 2 - 1 for r in range(c // SUBLANES)]
                ref = jnp.concatenate([last[p:p + 1] for p in pick], axis=0)
            return -jnp.abs(cum - jnp.broadcast_to(ref, cum3.shape).reshape(cum.shape))

        e_levels = [jnp.exp2(level_sums(m)) for m in HGRN_LEVELS]
        e_pre_all = jnp.exp2(cum)
        e_suf_all = jnp.exp2(cum[c - 1:c, :] - cum)
        q_all = _silu(hq_ref[0, rows, :].astype(F32))
        k_all = 1.0 - f
        v_all = hi_ref[0, rows, :].astype(F32)


        for hd in range(HGRN_HEADS):
            sl = slice(hd * HGRN_DK, (hd + 1) * HGRN_DK)
            q, k, v = q_all[:, sl], k_all[:, sl], v_all[:, sl]
            e_pre = e_pre_all[:, sl]
            e_suf = e_suf_all[:, sl]

            a = jnp.zeros((c, c), F32)
            for li in range(len(HGRN_LEVELS)):
                e = e_levels[li][:, sl]
                a = jnp.where(at_level[li], _dot_nt((q * e).astype(BF16), (k * e).astype(BF16)), a)
            vb = v.astype(BF16)
            out = _dot(a.astype(BF16), vb)

            out = out + jnp.sum(q * k, axis=-1, keepdims=True) * v

            st = st_ref[hd]
            out = out + _dot_nt((q * e_pre).astype(BF16), st.astype(BF16))
            kdec = (k * e_suf).astype(BF16)
            upd = lax.dot_general(vb, kdec, (((0,), (0,)), ((), ())), preferred_element_type=F32)
            st_ref[hd] = st * e_pre[c - 1:c, :] + upd

            ms = jnp.mean(out * out, axis=-1, keepdims=True)
            gate = _silu(hg_ref[0, rows, sl].astype(F32))
            o_ref[0, rows, sl] = (out * lax.rsqrt(ms + NORM_EPS) * g_ref[:, sl] * gate).astype(BF16)


def _hgrn(hq, hf, hi, hg, lb_rows, g_out):
    b, s, wd = hq.shape
    c = HGRN_CHUNK
    wmat = jnp.asarray(_hgrn_decay_matrix(c), BF16)
    tt = np.arange(c)[:, None]
    ss = np.arange(c)[None, :]
    lvl_np = np.full((c, c), -1.0, np.float32)
    for li, m in reversed(list(enumerate(HGRN_LEVELS))):
        lvl_np[(ss < tt) & (tt // (2 * m) == ss // (2 * m)) & (tt // m != ss // m)] = li
    lvl = jnp.asarray(lvl_np)
    tok = pl.BlockSpec((1, HGRN_STEP, wd), lambda bi, ci: (bi, ci, 0))
    return pl.pallas_call(
        _hgrn_kernel,
        grid=(b, s // HGRN_STEP),
        in_specs=[tok, tok, tok, tok,
                  pl.BlockSpec((2, wd), lambda bi, ci: (0, 0)),
                  pl.BlockSpec((1, wd), lambda bi, ci: (0, 0)),
                  pl.BlockSpec(wmat.shape, lambda bi, ci: (0, 0)),
                  pl.BlockSpec((c, c), lambda bi, ci: (0, 0))],
        out_specs=tok,
        out_shape=jax.ShapeDtypeStruct((b, s, wd), BF16),
        scratch_shapes=[pltpu.VMEM((HGRN_HEADS, HGRN_DK, HGRN_DK), F32)],
        compiler_params=pltpu.CompilerParams(dimension_semantics=("arbitrary", "arbitrary"),
                                             vmem_limit_bytes=VMEM_LIMIT),
        name="hgrn",
    )(hq, hf, hi, hg, lb_rows, g_out, wmat, lvl)


def _outproj_kernel(x_ref, ao_ref, ho_ref, mod_ref, wo_ref, g2_ref, wr_ref, br_ref,
                    lstrict_ref, fold_ref, upper_ref, selrows_ref,
                    x1_out, h2_out, route_out, cnt_out, rows_out, lg_scr):
    @pl.when(pl.program_id(0) == 0)
    def _():
        lg_scr[...] = jnp.zeros_like(lg_scr)

    prev_logits = lg_scr[...]
    gate1 = mod_ref[0, 2:3, :]
    shift2 = mod_ref[0, 3:4, :]
    scale2 = mod_ref[0, 4:5, :]
    mix = _dot(ao_ref[...], wo_ref[0:ATTN_WIDTH, :]) + _dot(ho_ref[...], wo_ref[ATTN_WIDTH:D_MODEL, :])
    x1 = x_ref[...] + gate1 * mix
    x1_out[...] = x1
    ms = jnp.mean(x1 * x1, axis=-1, keepdims=True)
    h2 = (x1 * lax.rsqrt(ms + NORM_EPS) * g2_ref[...]) * (1.0 + scale2) + shift2
    h2_out[...] = h2.astype(BF16)

    h2_hi = h2.astype(BF16)
    h2_lo = (h2 - h2_hi.astype(F32)).astype(BF16)
    lg_scr[...] = (_dot(h2_hi, wr_ref[0]) + (_dot(h2_hi, wr_ref[1]) + _dot(h2_lo, wr_ref[0]))) + br_ref[...]
    _route_tile(prev_logits, lstrict_ref, fold_ref, upper_ref, selrows_ref, route_out, cnt_out, rows_out)


def _route_tile(logits, lstrict_ref, fold_ref, upper_ref, selrows_ref, route_out, cnt_out, rows_out):
    tm = logits.shape[0]
    lane = lax.broadcasted_iota(jnp.int32, (tm, LANES), 1)
    neg = -jnp.inf
    is_group = (lane >= N_EXPERTS) & (lane < N_EXPERTS + N_GROUPS)
    gl = jnp.where(is_group, logits, neg)
    gmax = jnp.max(gl, axis=-1, keepdims=True)
    gsum = jnp.sum(jnp.exp(gl - gmax), axis=-1, keepdims=True)
    group_p = 1.0 / gsum
    gidx = jnp.min(jnp.where(is_group & (gl == gmax), lane, LANES), axis=-1, keepdims=True) - N_EXPERTS
    in_group = (lane < N_EXPERTS) & ((lane // EXPERTS_PER_GROUP) == gidx)
    el = jnp.where(in_group, logits, neg)
    top1 = jnp.max(el, axis=-1, keepdims=True)
    idx1 = jnp.min(jnp.where(in_group & (el == top1), lane, LANES), axis=-1, keepdims=True)
    el2 = jnp.where(lane == idx1, neg, el)
    top2 = jnp.max(el2, axis=-1, keepdims=True)
    idx2 = jnp.min(jnp.where(in_group & (lane != idx1) & (el2 == top2), lane, LANES),
                   axis=-1, keepdims=True)
    e2 = jnp.exp(top2 - top1)
    w1 = group_p / (1.0 + e2)
    w2 = group_p * e2 / (1.0 + e2)
    route = jnp.where(lane == idx1, 1.0,
                      jnp.where(lane == idx2 + N_EXPERTS, 1.0,
                                jnp.where(lane == ROUTE_W_LANE, w1,
                                          jnp.where(lane == ROUTE_W_LANE + 1, w2, 0.0))))
    sel = selrows_ref[...]
    ln = lax.broadcasted_iota(jnp.int32, (MOE_TD, LANES), 1)
    for sub in range(tm // MOE_TD):
        tile = slice(sub * MOE_TD, (sub + 1) * MOE_TD)
        rt = route[tile]
        cnt_out[0, sub:sub + 1, :] = jnp.sum(rt, axis=0, keepdims=True).astype(jnp.int32)
        v = _local_slots(rt, lstrict_ref[...], fold_ref[...], upper_ref[...])
        hi = jnp.floor(v * (1.0 / MOE_ALIGN))
        lo = v - hi * MOE_ALIGN
        rows_out[0, sub * SUBLANES:(sub + 1) * SUBLANES, :] = (
            _dot_nt(sel, hi.astype(BF16)) * MOE_ALIGN + _dot_nt(sel, lo.astype(BF16)))
        r1 = jnp.sum(jnp.where(ln < N_EXPERTS, v, 0.0), axis=-1, keepdims=True)
        r2 = jnp.sum(jnp.where(ln < N_EXPERTS, 0.0, v), axis=-1, keepdims=True)
        route_out[tile, :] = jnp.where(ln == ROUTE_ROW_LANE, r1,
                                       jnp.where(ln == ROUTE_ROW_LANE + 1, r2, rt))


def _outproj(x, ao, ho, mod, wo, g2, wr, br, tm, tiles_per_batch):
    t, d = x.shape
    n = t // tm
    sub = tm // MOE_TD
    const = lambda shape: pl.BlockSpec(shape, lambda i: (0,) * len(shape))
    cur = lambda w: pl.BlockSpec((tm, w), lambda i: (jnp.minimum(i, n - 1), 0))
    prev = lambda shape: pl.BlockSpec(shape, lambda i: (jnp.maximum(i - 1, 0),) + (0,) * (len(shape) - 1))
    return pl.pallas_call(
        _outproj_kernel,
        grid=(n + 1,),
        in_specs=[cur(d), cur(ATTN_WIDTH), cur(HGRN_WIDTH),
                  pl.BlockSpec((1, 6, d), lambda i: (jnp.minimum(i, n - 1) // tiles_per_batch, 0, 0)),
                  const((d, d)), const((1, d)), const((2, d, LANES)), const((1, LANES)),
                  const((MOE_TD, MOE_TD)), const((LANES, LANES)), const((LANES, LANES)),
                  const((SUBLANES, LANES))],
        out_specs=[cur(d), cur(d), prev((tm, LANES)), prev((1, sub, LANES)),
                   prev((1, sub * SUBLANES, MOE_TD))],
        out_shape=[jax.ShapeDtypeStruct((t, d), F32),
                   jax.ShapeDtypeStruct((t, d), BF16),
                   jax.ShapeDtypeStruct((t, LANES), F32),
                   jax.ShapeDtypeStruct((n, sub, LANES), jnp.int32),
                   jax.ShapeDtypeStruct((n, sub * SUBLANES, MOE_TD), F32)],
        scratch_shapes=[pltpu.VMEM((tm, LANES), F32)],
        compiler_params=pltpu.CompilerParams(dimension_semantics=("arbitrary",),
                                             vmem_limit_bytes=VMEM_LIMIT),
        name="outproj",
    )(x, ao, ho, mod, wo, g2, wr, br, *_moe_constants())


def _ceil_to(v, m):
    return ((v + (m - 1)) // m) * m


def _cdiv_pow2(v, m):
    return lax.shift_right_logical(v + (m - 1), m.bit_length() - 1)


def _moe_constants():
    a = np.arange(LANES)
    ne = N_EXPERTS
    td = MOE_TD
    lstrict = np.tril(np.ones((td, td)), -1)
    fold = ((a[:, None] < 2 * ne) & (a[None, :] < 2 * ne) & (a[:, None] % ne == a[None, :] % ne))
    upper = ((a[:, None] < ne) & (a[None, :] < 2 * ne) & (a[:, None] < a[None, :] % ne))
    selrows = np.zeros((SUBLANES, LANES))
    selrows[0, :ne] = 1.0
    selrows[1, ne:2 * ne] = 1.0
    return tuple(jnp.asarray(m, BF16) for m in (lstrict, fold, upper, selrows))


def _local_slots(route, lstrict, fold, upper):
    lane = lax.broadcasted_iota(jnp.int32, route.shape, 1)
    member = jnp.where(lane < 2 * N_EXPERTS, route, 0.0)
    rank = _dot(_dot(lstrict, member.astype(BF16)).astype(BF16), fold)
    cnt = jnp.broadcast_to(jnp.sum(member, axis=0, keepdims=True), (SUBLANES, LANES))
    cnt = _dot(cnt.astype(BF16), fold)
    run = jnp.floor((cnt + (MOE_ALIGN - 1)) * (1.0 / MOE_ALIGN)) * MOE_ALIGN
    start = _dot(run.astype(BF16), upper)[0:1, :]
    return member * (start + rank)


def _plan_kernel(c_ref, base_ref, texp_ref, meta_ref, tail_ref, *, ntiles, n_row_tiles):
    ne = N_EXPERTS
    off = jnp.int32(0)
    for e in range(ne):
        def body(i, run, e=e, off=off):
            c = c_ref[i * 2 * ne + e] + c_ref[i * 2 * ne + ne + e]
            base_ref[i * ne + e] = off + run
            return run + _cdiv_pow2(c, MOE_ALIGN) * MOE_ALIGN
        total = lax.fori_loop(0, ntiles, body, jnp.int32(0), unroll=8)
        nt = (total + (MOE_TM - 1)) // MOE_TM
        first = off // MOE_TM

        def fill(j, carry, e=e, first=first):
            texp_ref[first + j] = e
            return carry
        lax.fori_loop(0, nt, fill, 0)
        tail_ref[e] = off + total
        tail_ref[ne + e] = (nt * MOE_TM - total) // MOE_ALIGN
        off = off + nt * MOE_TM
    nvalid = off // MOE_TM

    def fill_rest(j, carry):
        texp_ref[j] = ne - 1
        return carry
    lax.fori_loop(nvalid, n_row_tiles, fill_rest, 0)
    meta_ref[0] = nvalid


def _plan(counts_flat, ntiles, n_row_tiles):
    smem = pl.BlockSpec(memory_space=pltpu.SMEM)
    return pl.pallas_call(
        functools.partial(_plan_kernel, ntiles=ntiles, n_row_tiles=n_row_tiles),
        in_specs=[smem],
        out_specs=[smem, smem, smem, smem],
        out_shape=[jax.ShapeDtypeStruct((ntiles * N_EXPERTS,), jnp.int32),
                   jax.ShapeDtypeStruct((n_row_tiles,), jnp.int32),
                   jax.ShapeDtypeStruct((1,), jnp.int32),
                   jax.ShapeDtypeStruct((2 * N_EXPERTS,), jnp.int32)],
        name="moe_plan",
    )(counts_flat)


def _run_chunks(c_ref, base_ref, tile, fn):
    ne = N_EXPERTS
    local = jnp.int32(0)
    for e in range(ne):
        c = c_ref[tile * 2 * ne + e] + c_ref[tile * 2 * ne + ne + e]
        nchunk = _cdiv_pow2(c, MOE_ALIGN)
        hbm = base_ref[tile * ne + e]

        def body(j, carry, hbm=hbm, local=local):
            fn(pl.multiple_of(hbm + j * MOE_ALIGN, MOE_ALIGN),
               pl.multiple_of(local + j * MOE_ALIGN, MOE_ALIGN))
            return carry
        lax.fori_loop(0, nchunk, body, 0)
        local = local + nchunk * MOE_ALIGN
    return _cdiv_pow2(local, MOE_ALIGN)


def _dispatch_kernel(c_ref, base_ref, tail_ref, meta_ref, h2_ref, rows_ref, xs_ref,
                     xs_scr, zero_scr, sem, nch_ref, *, n_row_tiles):
    i = pl.program_id(0)
    n = pl.num_programs(0)
    slot = i % 2

    def copy(slot_, hbm, local):
        return pltpu.make_async_copy(xs_scr.at[slot_, pl.ds(local, MOE_ALIGN), :],
                                     xs_ref.at[pl.ds(hbm, MOE_ALIGN), :], sem.at[slot_])

    def wait_all(slot_):
        def body(j, carry):
            copy(slot_, 0, 0).wait()
            return carry
        lax.fori_loop(0, nch_ref[slot_], body, 0)

    @pl.when(i >= 2)
    def _():
        wait_all(slot)

    rows = rows_ref[0]
    slot_id = lax.broadcasted_iota(jnp.int32, (MOE_L, MOE_TD), 0)
    p1 = jnp.where(slot_id == rows[0:1, :].astype(jnp.int32), 1.0, 0.0)
    p2 = jnp.where(slot_id == rows[1:2, :].astype(jnp.int32), 1.0, 0.0)
    xs_scr[slot] = _dot((p1 + p2).astype(BF16), h2_ref[...]).astype(BF16)
    nch_ref[slot] = _run_chunks(c_ref, base_ref, i, lambda hbm, local: copy(slot, hbm, local).start())

    @pl.when(i == n - 1)
    def _():
        zero_scr[...] = jnp.zeros_like(zero_scr)

        def chunk(hbm):
            return pltpu.make_async_copy(zero_scr.at[pl.ds(0, MOE_ALIGN), :],
                                         xs_ref.at[pl.ds(hbm, MOE_ALIGN), :], sem.at[2])

        def tile(j):
            return pltpu.make_async_copy(zero_scr, xs_ref.at[pl.ds(j * MOE_TM, MOE_TM), :], sem.at[3])

        nchunks = jnp.int32(0)
        for e in range(N_EXPERTS):
            start = tail_ref[e]
            cnt = tail_ref[N_EXPERTS + e]

            def fill(j, carry, start=start):
                chunk(pl.multiple_of(start + j * MOE_ALIGN, MOE_ALIGN)).start()
                return carry
            lax.fori_loop(0, cnt, fill, 0)
            nchunks = nchunks + cnt

        def fill_tile(j, carry):
            tile(j).start()
            return carry
        lax.fori_loop(meta_ref[0], n_row_tiles, fill_tile, 0)

        wait_all(slot)

        @pl.when(n >= 2)
        def _():
            wait_all(1 - slot)

        def wait_chunk(j, carry):
            chunk(0).wait()
            return carry
        lax.fori_loop(0, nchunks, wait_chunk, 0)

        def wait_tile(j, carry):
            tile(0).wait()
            return carry
        lax.fori_loop(meta_ref[0], n_row_tiles, wait_tile, 0)


def _dispatch(counts_flat, base, tail, meta, h2, rows, n_rows):
    t, d = h2.shape
    ntiles = t // MOE_TD
    grid_spec = pltpu.PrefetchScalarGridSpec(
        num_scalar_prefetch=4,
        grid=(ntiles,),
        in_specs=[pl.BlockSpec((MOE_TD, d), lambda i, *_: (i, 0)),
                  pl.BlockSpec((1, SUBLANES, MOE_TD), lambda i, *_: (i, 0, 0))],
        out_specs=pl.BlockSpec(memory_space=pl.ANY),
        scratch_shapes=[pltpu.VMEM((2, MOE_L, d), BF16), pltpu.VMEM((MOE_TM, d), BF16),
                        pltpu.SemaphoreType.DMA((4,)), pltpu.SMEM((2,), jnp.int32)])
    return pl.pallas_call(
        functools.partial(_dispatch_kernel, n_row_tiles=n_rows // MOE_TM),
        grid_spec=grid_spec,
        out_shape=jax.ShapeDtypeStruct((n_rows, d), BF16),
        compiler_params=pltpu.CompilerParams(dimension_semantics=("arbitrary",),
                                             vmem_limit_bytes=VMEM_LIMIT),
        name="moe_dispatch",
    )(counts_flat, base, tail, meta, h2, rows)


def _experts_kernel(texp_ref, meta_ref, xs_ref, wg_ref, wu_ref, wd_ref, ys_ref):
    del texp_ref

    @pl.when(pl.program_id(0) >= meta_ref[0])
    def _():
        ys_ref[...] = jnp.zeros_like(ys_ref)

    @pl.when(pl.program_id(0) < meta_ref[0])
    def _():
        x = xs_ref[...]
        act = _silu(_dot(x, wg_ref[0])) * _dot(x, wu_ref[0])
        ys_ref[...] = _dot(act.astype(BF16), wd_ref[0]).astype(BF16)


def _experts(texp, meta, xs, wg, wu, wd):
    n_rows, d = xs.shape
    row = lambda w: pl.BlockSpec((MOE_TM, w), lambda i, te, mt: (jnp.minimum(i, mt[0] - 1), 0))
    wspec = lambda shape: pl.BlockSpec((1,) + shape, lambda i, te, mt: (te[i], 0, 0))
    grid_spec = pltpu.PrefetchScalarGridSpec(
        num_scalar_prefetch=2,
        grid=(n_rows // MOE_TM,),
        in_specs=[row(d), wspec((d, D_EXPERT)), wspec((d, D_EXPERT)), wspec((D_EXPERT, d))],
        out_specs=pl.BlockSpec((MOE_TM, d), lambda i, te, mt: (i, 0)))
    return pl.pallas_call(
        _experts_kernel,
        grid_spec=grid_spec,
        out_shape=jax.ShapeDtypeStruct((n_rows, d), BF16),
        compiler_params=pltpu.CompilerParams(dimension_semantics=("arbitrary",),
                                             vmem_limit_bytes=VMEM_LIMIT),
        name="moe_experts",
    )(texp, meta, xs, wg, wu, wd)


def _combine_kernel(c_ref, base_ref, x1_ref, route_ref, mod_ref, ys_ref, o_ref, ys_scr, sem,
                    nch_ref):
    i = pl.program_id(0)
    n = pl.num_programs(0)
    slot = i % 2

    def copy(slot_, hbm, local):
        return pltpu.make_async_copy(ys_ref.at[pl.ds(hbm, MOE_ALIGN), :],
                                     ys_scr.at[slot_, pl.ds(local, MOE_ALIGN), :], sem.at[slot_])

    def fetch(tile, slot_):
        nch_ref[slot_] = _run_chunks(c_ref, base_ref, tile,
                                     lambda hbm, local: copy(slot_, hbm, local).start())

    @pl.when(i == 0)
    def _():
        ys_scr[...] = jnp.zeros_like(ys_scr)
        fetch(0, 0)

    @pl.when(i + 1 < n)
    def _():
        fetch(i + 1, 1 - slot)

    route = route_ref[...]
    lane = lax.broadcasted_iota(jnp.int32, route.shape, 1)
    r1 = jnp.sum(jnp.where(lane == ROUTE_ROW_LANE, route, 0.0), axis=-1, keepdims=True)
    r2 = jnp.sum(jnp.where(lane == ROUTE_ROW_LANE + 1, route, 0.0), axis=-1, keepdims=True)
    r1, r2 = r1.astype(jnp.int32), r2.astype(jnp.int32)
    col = lax.broadcasted_iota(jnp.int32, (MOE_TD, MOE_L), 1)
    pick1 = jnp.where(col == r1, 1.0, 0.0).astype(BF16)
    pick2 = jnp.where(col == r2, 1.0, 0.0).astype(BF16)
    w1 = jnp.sum(jnp.where(lane == ROUTE_W_LANE, route, 0.0), axis=-1, keepdims=True)
    w2 = jnp.sum(jnp.where(lane == ROUTE_W_LANE + 1, route, 0.0), axis=-1, keepdims=True)

    def wait(j, carry):
        copy(slot, 0, 0).wait()
        return carry
    lax.fori_loop(0, nch_ref[slot], wait, 0)
    ys = ys_scr[slot]
    y = w1 * _dot(pick1, ys) + w2 * _dot(pick2, ys)
    o_ref[...] = x1_ref[...] + mod_ref[0, 5:6, :] * y


def _combine(counts_flat, base, x1, route, mod, ys, tiles_per_batch):
    t, d = x1.shape
    tok = lambda w: pl.BlockSpec((MOE_TD, w), lambda i, c, b: (i, 0))
    grid_spec = pltpu.PrefetchScalarGridSpec(
        num_scalar_prefetch=2,
        grid=(t // MOE_TD,),
        in_specs=[tok(d), tok(LANES),
                  pl.BlockSpec((1, 6, d), lambda i, c, b: (i // tiles_per_batch, 0, 0)),
                  pl.BlockSpec(memory_space=pl.ANY)],
        out_specs=tok(d),
        scratch_shapes=[pltpu.VMEM((2, MOE_L, d), BF16), pltpu.SemaphoreType.DMA((2,)),
                        pltpu.SMEM((2,), jnp.int32)])
    return pl.pallas_call(
        _combine_kernel,
        grid_spec=grid_spec,
        out_shape=jax.ShapeDtypeStruct((t, d), F32),
        compiler_params=pltpu.CompilerParams(dimension_semantics=("arbitrary",),
                                             vmem_limit_bytes=VMEM_LIMIT),
        name="moe_combine",
    )(counts_flat, base, x1, route, mod, ys)


def kernel(x, c, w_ada, b_ada, norm1_g, w_in, b_fox, q_norm_g, k_norm_g, attn_out_g, hgrn_lb,
           hgrn_out_g, w_out, norm2_g, w_router_group, b_router_group, w_router_expert,
           b_router_expert, w_gate, w_up, w_down):
    b, s, d = x.shape
    l = 0
    aw = ATTN_WIDTH
    mod = _ada(c, w_ada[l], b_ada[l]).reshape(b, 6, d)

    w = w_in[l]
    wq = w[:, 0:aw].astype(BF16)
    wk = w[:, aw:2 * aw].astype(BF16)
    wv = w[:, 2 * aw:3 * aw].astype(BF16)
    f0 = 3 * aw
    pad_f = LANES - FOX_PIECES * ATTN_HEADS
    wf = jnp.pad(jnp.tile(w[:, f0:f0 + ATTN_HEADS], (1, FOX_PIECES)), ((0, 0), (0, pad_f))).astype(BF16)
    wh = w[:, f0 + ATTN_HEADS:].astype(BF16)
    bfox = jnp.pad(jnp.tile(b_fox[l], FOX_PIECES), (0, pad_f)).reshape(1, LANES)
    gq = jnp.tile(q_norm_g[l], ATTN_HEADS).reshape(1, aw)
    gk = jnp.tile(k_norm_g[l], ATTN_HEADS).reshape(1, aw)

    tm = min(512, s)
    q, k, v, ck, hq, hf, hi, hg = _inproj(x, mod, norm1_g[l].reshape(1, d), wq, wk, wv, wf, wh,
                                           gq, gk, bfox, tm)
    ao = _fox(q, k, v, ck, attn_out_g[l].reshape(1, aw), min(256, s), 128)
    ho = _hgrn(hq, hf, hi, hg, hgrn_lb[0:2], hgrn_out_g[l].reshape(1, HGRN_WIDTH))

    wr = jnp.pad(jnp.concatenate([w_router_expert[l], w_router_group[l]], axis=1),
                 ((0, 0), (0, LANES - N_GROUPS - N_EXPERTS)))
    br = jnp.pad(jnp.concatenate([b_router_expert[l], b_router_group[l]]),
                 (0, LANES - N_GROUPS - N_EXPERTS)).reshape(1, LANES)
    wr_hi = wr.astype(BF16)
    wr2 = jnp.stack([wr_hi, (wr - wr_hi.astype(F32)).astype(BF16)])
    t = b * s
    x1, h2, route2, counts, rows = _outproj(x.reshape(t, d), ao.reshape(t, ATTN_WIDTH),
                                            ho.reshape(t, HGRN_WIDTH), mod, w_out[l].astype(BF16),
                                            norm2_g[l].reshape(1, d), wr2, br, tm, s // tm)
    ntiles = t // MOE_TD
    counts_flat = counts.reshape(ntiles, LANES)[:, :2 * N_EXPERTS].reshape(-1)
    n_rows = _ceil_to(2 * t + ntiles * N_EXPERTS * (MOE_ALIGN - 1), MOE_TM) + N_EXPERTS * MOE_TM
    base, texp, meta, tail = _plan(counts_flat, ntiles, n_rows // MOE_TM)
    xs = _dispatch(counts_flat, base, tail, meta, h2,
                   rows.reshape(ntiles, SUBLANES, MOE_TD), n_rows)
    ys = _experts(texp, meta, xs, w_gate[l].astype(BF16), w_up[l].astype(BF16),
                  w_down[l].astype(BF16))
    out = _combine(counts_flat, base, x1, route2, mod, ys, s // MOE_TD)
    return out.reshape(b, s, d)
```

```python
import functools

import numpy as np
import jax
import jax.numpy as jnp
from jax import lax
from jax.experimental import pallas as pl
from jax.experimental.pallas import tpu as pltpu

F32 = jnp.float32
BF16 = jnp.bfloat16

D_MODEL = 1024
ATTN_HEAD_DIM = 64
ATTN_WIDTH = 512
ATTN_HEADS = 8
HGRN_WIDTH = 512
HGRN_HEADS = 4
HGRN_DK = 128
N_GROUPS = 4
EXPERTS_PER_GROUP = 4
N_EXPERTS = 16
D_EXPERT = 512
NORM_EPS = 1e-6
LANES = 128
SUBLANES = 8
VMEM_LIMIT = 56 * 1024 * 1024

HGRN_CHUNK = 128
HGRN_STEP = 512
HGRN_LEVELS = (1, 2, 4, 8, 16, 32, 64)
HGRN_MXU_LEVELS = (1, 2)
ROUTE_W_LANE = 2 * N_EXPERTS
ROUTE_ROW_LANE = ROUTE_W_LANE + 2
MOE_TD = 256
MOE_ALIGN = 16
MOE_TM = 512
MOE_L = 2 * MOE_TD + MOE_ALIGN * N_EXPERTS
LOG2E = 1.4426950408889634
FOX_GROUP = 4
FOX_PIECES = 3
FOX_CK_WIDTH = (ATTN_HEADS // FOX_GROUP) * LANES


def _sigmoid(x):
    return 1.0 / (1.0 + jnp.exp(-x))


def _silu(x):
    return x * (0.5 * jnp.tanh(0.5 * x) + 0.5)


def _split3(x):
    p1 = x.astype(BF16)
    r1 = x - p1.astype(F32)
    p2 = r1.astype(BF16)
    p3 = (r1 - p2.astype(F32)).astype(BF16)
    return p1, p2, p3


def _dot(a, b):
    return jnp.dot(a, b, preferred_element_type=F32)


def _dot_nt(a, b):
    return lax.dot_general(a, b, (((1,), (1,)), ((), ())), preferred_element_type=F32)


def _ada_kernel(c_ref, w_ref, b_ref, o_ref):
    c = c_ref[...]
    o_ref[...] = jnp.dot(_silu(c), w_ref[...], preferred_element_type=F32,
                         precision=lax.Precision.HIGHEST) + b_ref[...]


def _ada(c, w, b):
    bsz, d = c.shape
    n = w.shape[1]
    tn = 1024
    return pl.pallas_call(
        _ada_kernel,
        grid=(n // tn,),
        in_specs=[pl.BlockSpec((bsz, d), lambda j: (0, 0)),
                  pl.BlockSpec((d, tn), lambda j: (0, j)),
                  pl.BlockSpec((1, tn), lambda j: (0, j))],
        out_specs=pl.BlockSpec((bsz, tn), lambda j: (0, j)),
        out_shape=jax.ShapeDtypeStruct((bsz, n), F32),
        compiler_params=pltpu.CompilerParams(dimension_semantics=("arbitrary",),
                                             vmem_limit_bytes=VMEM_LIMIT),
        name="ada",
    )(c, w, b.reshape(1, n))


def _inproj_kernel(x_ref, mod_ref, g1_ref, wq_ref, wk_ref, wv_ref, wf_ref, wh_ref,
                   gq_ref, gk_ref, bf_ref, gsum_ref, tri_ref, place_ref,
                   q_out, k_out, v_out, ck_out, hq_out, hf_out, hi_out, hg_out,
                   carry_ref):
    si = pl.program_id(1)

    @pl.when(si == 0)
    def _():
        carry_ref[...] = jnp.zeros_like(carry_ref)

    x = x_ref[0]
    shift = mod_ref[0, 0:1, :]
    scale = mod_ref[0, 1:2, :]
    ms = jnp.mean(x * x, axis=-1, keepdims=True)
    h = (x * lax.rsqrt(ms + NORM_EPS) * g1_ref[...]) * (1.0 + scale) + shift
    hb = h.astype(BF16)

    def qk_norm(w_ref, g_ref, mult):
        t = _dot(hb, w_ref[...])
        ssq = _dot((t * t).astype(BF16), gsum_ref[...])
        return t * lax.rsqrt(ssq * (1.0 / ATTN_HEAD_DIM) + NORM_EPS) * (g_ref[...] * mult)

    q_out[0] = qk_norm(wq_ref, gq_ref, ATTN_HEAD_DIM ** -0.5 * LOG2E).T.astype(BF16)
    k_out[0] = qk_norm(wk_ref, gk_ref, 1.0).astype(BF16)
    v_out[0] = _dot(hb, wv_ref[...]).T.astype(BF16)

    af = _dot(hb, wf_ref[...]) + bf_ref[...]
    lf = jnp.minimum(af, 0.0) - jnp.log(1.0 + jnp.exp(-jnp.abs(af)))
    tri = tri_ref[...]
    p1, p2, p3 = _split3(lf)
    cum = (_dot(tri, p1) + _dot(tri, p2)) + _dot(tri, p3) + carry_ref[...]
    tm = cum.shape[0]
    carry_ref[...] = cum[tm - 1:tm, :]
    c1, c2, c3 = _split3(cum * (-LOG2E))
    lane = lax.broadcasted_iota(jnp.int32, cum.shape, 1)
    zero = jnp.zeros_like(c1)
    pieces = jnp.where(lane < ATTN_HEADS, c1,
                       jnp.where(lane < 2 * ATTN_HEADS, c2,
                                 jnp.where(lane < 3 * ATTN_HEADS, c3, zero)))
    ck_out[0] = _dot(pieces, place_ref[...]).astype(BF16)

    hq_out[0] = _dot(hb, wh_ref[:, 0 * HGRN_WIDTH:1 * HGRN_WIDTH]).astype(BF16)
    hf_out[0] = _dot(hb, wh_ref[:, 1 * HGRN_WIDTH:2 * HGRN_WIDTH]).astype(BF16)
    hi_out[0] = _dot(hb, wh_ref[:, 2 * HGRN_WIDTH:3 * HGRN_WIDTH]).astype(BF16)
    hg_out[0] = _dot(hb, wh_ref[:, 3 * HGRN_WIDTH:4 * HGRN_WIDTH]).astype(BF16)


def _inproj(x, mod, g1, wq, wk, wv, wf, wh, gq, gk, bfox, tm):
    b, s, d = x.shape
    gsum = jnp.asarray(np.kron(np.eye(ATTN_HEADS), np.ones((ATTN_HEAD_DIM, ATTN_HEAD_DIM))), BF16)
    tri = jnp.asarray(np.tril(np.ones((tm, tm))), BF16)
    place_np = np.zeros((LANES, FOX_CK_WIDTH), np.float32)
    for piece in range(FOX_PIECES):
        for hd in range(ATTN_HEADS):
            col = (hd // FOX_GROUP) * LANES + (hd % FOX_GROUP) * FOX_PIECES + piece
            place_np[piece * ATTN_HEADS + hd, col] = 1.0
    place = jnp.asarray(place_np, BF16)
    const = lambda shape: pl.BlockSpec(shape, lambda bi, si: (0,) * len(shape))
    tok = lambda w: pl.BlockSpec((1, tm, w), lambda bi, si: (bi, si, 0))
    tok_t = lambda w: pl.BlockSpec((1, w, tm), lambda bi, si: (bi, 0, si))
    act = lambda w: jax.ShapeDtypeStruct((b, s, w), BF16)
    act_t = lambda w: jax.ShapeDtypeStruct((b, w, s), BF16)
    return pl.pallas_call(
        _inproj_kernel,
        grid=(b, s // tm),
        in_specs=[tok(d),
                  pl.BlockSpec((1, 6, d), lambda bi, si: (bi, 0, 0)),
                  const((1, d)),
                  const((d, ATTN_WIDTH)), const((d, ATTN_WIDTH)), const((d, ATTN_WIDTH)),
                  const((d, LANES)), const((d, 4 * HGRN_WIDTH)),
                  const((1, ATTN_WIDTH)), const((1, ATTN_WIDTH)), const((1, LANES)),
                  const((ATTN_WIDTH, ATTN_WIDTH)), const((tm, tm)),
                  const((LANES, FOX_CK_WIDTH))],
        out_specs=[tok_t(ATTN_WIDTH), tok(ATTN_WIDTH), tok_t(ATTN_WIDTH),
                   tok(FOX_CK_WIDTH),
                   tok(HGRN_WIDTH), tok(HGRN_WIDTH), tok(HGRN_WIDTH), tok(HGRN_WIDTH)],
        out_shape=[act_t(ATTN_WIDTH), act(ATTN_WIDTH), act_t(ATTN_WIDTH),
                   act(FOX_CK_WIDTH),
                   act(HGRN_WIDTH), act(HGRN_WIDTH), act(HGRN_WIDTH), act(HGRN_WIDTH)],
        scratch_shapes=[pltpu.VMEM((1, LANES), F32)],
        compiler_params=pltpu.CompilerParams(dimension_semantics=("arbitrary", "arbitrary"),
                                             vmem_limit_bytes=VMEM_LIMIT),
        name="inproj",
    )(x, mod, g1, wq, wk, wv, wf, wh, gq, gk, bfox, gsum, tri, place)


def _fox_kernel(qt_ref, k_ref, vt_ref, ck_ref, g_ref, o_ref, st_scr, pt_scr, *, tq, tk):
    qi = pl.program_id(2)
    qt = qt_ref[0].astype(F32)
    row = lax.broadcasted_iota(jnp.int32, (LANES, tq), 0)
    first = row < ATTN_HEAD_DIM
    rhs = []
    for hd in range(FOX_GROUP):
        qp = qt[(hd // 2) * LANES:(hd // 2 + 1) * LANES, :]
        qh = jnp.where(first, qp, 0.0) if hd % 2 == 0 else jnp.where(first, 0.0, qp)
        ones = jnp.where((row >= hd * FOX_PIECES) & (row < (hd + 1) * FOX_PIECES), 1.0, 0.0)
        rhs.append(jnp.concatenate([qh, ones], axis=0).astype(BF16))

    def stage_qk(j, slot):
        k0 = pl.multiple_of(j * tk, tk)
        kb = k_ref[0, pl.ds(k0, tk), :]
        ckb = ck_ref[0, pl.ds(k0, tk), :]
        for hd in range(FOX_GROUP):
            pr = slice((hd // 2) * LANES, (hd // 2 + 1) * LANES)
            lhs = jnp.concatenate([kb[:, pr], ckb], axis=1)
            st_scr[slot, hd] = _dot(lhs, rhs[hd])

    def stage_softmax(j, slot, stats, masked):
        out = []
        for hd in range(FOX_GROUP):
            m, l = stats[hd]
            st = st_scr[slot, hd]
            if masked:
                key = j * tk + lax.broadcasted_iota(jnp.int32, (tk, tq), 0)
                qry = qi * tq + lax.broadcasted_iota(jnp.int32, (tk, tq), 1)
                st = jnp.where(key <= qry, st, -jnp.inf)
            m_new = jnp.maximum(m, jnp.max(st, axis=0, keepdims=True))
            alpha = jnp.exp2(m - m_new)
            pt = jnp.exp2(st - m_new)
            pt_scr[slot, hd] = pt.astype(BF16)
            out.append(((m_new, alpha * l + jnp.sum(pt, axis=0, keepdims=True)), alpha))
        return tuple(o[0] for o in out), tuple(o[1] for o in out)

    def stage_pv(j, slot, accs, alphas):
        k0 = pl.multiple_of(jnp.maximum(j, 0) * tk, tk)
        vtb = vt_ref[0, :, pl.ds(k0, tk)]
        out = []
        for hd in range(FOX_GROUP):
            vth = vtb[hd * ATTN_HEAD_DIM:(hd + 1) * ATTN_HEAD_DIM, :]
            out.append(alphas[hd] * accs[hd] + _dot(vth, pt_scr[slot, hd]))
        return tuple(out)

    per_q = tq // tk
    stats = tuple((jnp.full((1, tq), -1e30, F32), jnp.zeros((1, tq), F32))
                  for _ in range(FOX_GROUP))
    accs = tuple(jnp.zeros((ATTN_HEAD_DIM, tq), F32) for _ in range(FOX_GROUP))
    alphas = tuple(jnp.ones((1, tq), F32) for _ in range(FOX_GROUP))
    pt_scr[1] = jnp.zeros_like(pt_scr[1])
    stage_qk(0, 0)

    def full_blocks(i, carry):
        stats, accs, alphas = carry
        for u in range(per_q):
            j = i * per_q + u
            accs = stage_pv(j - 1, 1 - u, accs, alphas)
            stats, alphas = stage_softmax(j, u, stats, False)
            stage_qk(j + 1, 1 - u)
        return stats, accs, alphas

    stats, accs, alphas = lax.fori_loop(0, qi, full_blocks, (stats, accs, alphas))
    j = qi * per_q
    accs = stage_pv(j - 1, 1, accs, alphas)
    stats, alphas = stage_softmax(j, 0, stats, True)
    stage_qk(j + 1, 1)
    accs = stage_pv(j, 0, accs, alphas)
    stats, alphas = stage_softmax(j + 1, 1, stats, True)
    accs = stage_pv(j + 1, 1, accs, alphas)
    carry = tuple((stats[hd][0], stats[hd][1], accs[hd]) for hd in range(FOX_GROUP))

    head0 = lax.broadcasted_iota(jnp.int32, (tq, LANES), 1) < ATTN_HEAD_DIM
    for pr in range(FOX_GROUP // 2):
        (_, l0, a0), (_, l1, a1) = carry[2 * pr], carry[2 * pr + 1]
        ot = jnp.concatenate([a0 * (1.0 / l0), a1 * (1.0 / l1)], axis=0)
        o = ot.T
        osq = o * o
        ss0 = jnp.sum(jnp.where(head0, osq, 0.0), axis=-1, keepdims=True)
        ss1 = jnp.sum(jnp.where(head0, 0.0, osq), axis=-1, keepdims=True)
        ms = jnp.where(head0, ss0, ss1) * (1.0 / ATTN_HEAD_DIM)
        sl = slice(pr * LANES, (pr + 1) * LANES)
        o_ref[0, :, sl] = (o * lax.rsqrt(ms + NORM_EPS) * g_ref[:, sl]).astype(BF16)


def _fox(qt, k, vt, ck, g_out, tq, tk):
    b, s, _ = k.shape
    groups = ATTN_HEADS // FOX_GROUP
    gw = FOX_GROUP * ATTN_HEAD_DIM
    return pl.pallas_call(
        functools.partial(_fox_kernel, tq=tq, tk=tk),
        grid=(b, groups, s // tq),
        in_specs=[pl.BlockSpec((1, gw, tq), lambda bi, g, qi: (bi, g, qi)),
                  pl.BlockSpec((1, s, gw), lambda bi, g, qi: (bi, 0, g)),
                  pl.BlockSpec((1, gw, s), lambda bi, g, qi: (bi, g, 0)),
                  pl.BlockSpec((1, s, LANES), lambda bi, g, qi: (bi, 0, g)),
                  pl.BlockSpec((1, gw), lambda bi, g, qi: (0, g))],
        out_specs=pl.BlockSpec((1, tq, gw), lambda bi, g, qi: (bi, qi, g)),
        out_shape=jax.ShapeDtypeStruct((b, s, ATTN_WIDTH), BF16),
        scratch_shapes=[pltpu.VMEM((2, FOX_GROUP, tk, tq), F32),
                        pltpu.VMEM((2, FOX_GROUP, tk, tq), BF16)],
        compiler_params=pltpu.CompilerParams(
            dimension_semantics=("arbitrary", "arbitrary", "arbitrary"),
            vmem_limit_bytes=VMEM_LIMIT),
        name="fox",
    )(qt, k, vt, ck, g_out)


def _hgrn_decay_matrix(c):
    t = np.arange(c)[:, None]
    j = np.arange(c)[None, :]
    blocks = [(j <= t)]
    for m in HGRN_MXU_LEVELS:
        mid = (t // (2 * m)) * (2 * m) + m
        right = (t % (2 * m)) >= m
        blocks.append(np.where(right, (j >= mid) & (j <= t), (j > t) & (j < mid)))
    return np.concatenate(blocks, axis=0).astype(np.float32)


def _hgrn_kernel(hq_ref, hf_ref, hi_ref, hg_ref, lb_ref, g_ref, w_ref, lvl_ref, o_ref, st_ref):
    ci = pl.program_id(1)
    c = HGRN_CHUNK

    @pl.when(ci == 0)
    def _():
        st_ref[...] = jnp.zeros_like(st_ref)

    r0 = lb_ref[0:1, :]
    r1 = lb_ref[1:2, :]
    rmax = jnp.maximum(r0, r1)
    e0 = jnp.exp(r0 - rmax)
    lb = e0 / (e0 + jnp.exp(r1 - rmax))

    lvl = lvl_ref[...]
    at_level = [lvl == float(li) for li in range(len(HGRN_LEVELS))]
    for sub in range(hq_ref.shape[1] // c):
        rows = slice(sub * c, (sub + 1) * c)
        f = lb + (1.0 - lb) * _sigmoid(hf_ref[0, rows, :].astype(F32))
        g = jnp.log2(f)
        g1 = g.astype(BF16)
        g2 = (g - g1.astype(F32)).astype(BF16)
        w = w_ref[...]
        xw = _dot(w, g1) + _dot(w, g2)
        cum = xw[0:c]
        cum3 = cum.reshape(c // SUBLANES, SUBLANES, cum.shape[1])

        def level_sums(m):
            if m in HGRN_MXU_LEVELS:
                at = 1 + HGRN_MXU_LEVELS.index(m)
                return xw[at * c:(at + 1) * c]
            if 2 * m == SUBLANES:
                ref = cum3[:, m - 1:m, :]
            else:
                last = cum3[:, SUBLANES - 1:SUBLANES, :]
                per = 2 * m // SUBLANES
                pick = [(r // per) * per + per // 2 - 1 for r in range(c // SUBLANES)]
                ref = jnp.concatenate([last[p:p + 1] for p in pick], axis=0)
            return -jnp.abs(cum - jnp.broadcast_to(ref, cum3.shape).reshape(cum.shape))

        e_levels = [jnp.exp2(level_sums(m)) for m in HGRN_LEVELS]
        e_pre_all = jnp.exp2(cum)
        e_suf_all = jnp.exp2(cum[c - 1:c, :] - cum)
        q_all = _silu(hq_ref[0, rows, :].astype(F32))
        k_all = 1.0 - f
        v_all = hi_ref[0, rows, :].astype(F32)


        for hd in range(HGRN_HEADS):
            sl = slice(hd * HGRN_DK, (hd + 1) * HGRN_DK)
            q, k, v = q_all[:, sl], k_all[:, sl], v_all[:, sl]
            e_pre = e_pre_all[:, sl]
            e_suf = e_suf_all[:, sl]

            a = jnp.zeros((c, c), F32)
            for li in range(len(HGRN_LEVELS)):
                e = e_levels[li][:, sl]
                a = jnp.where(at_level[li], _dot_nt((q * e).astype(BF16), (k * e).astype(BF16)), a)
            vb = v.astype(BF16)
            out = _dot(a.astype(BF16), vb)

            out = out + jnp.sum(q * k, axis=-1, keepdims=True) * v

            st = st_ref[hd]
            out = out + _dot_nt((q * e_pre).astype(BF16), st.astype(BF16))
            kdec = (k * e_suf).astype(BF16)
            upd = lax.dot_general(vb, kdec, (((0,), (0,)), ((), ())), preferred_element_type=F32)
            st_ref[hd] = st * e_pre[c - 1:c, :] + upd

            ms = jnp.mean(out * out, axis=-1, keepdims=True)
            gate = _silu(hg_ref[0, rows, sl].astype(F32))
            o_ref[0, rows, sl] = (out * lax.rsqrt(ms + NORM_EPS) * g_ref[:, sl] * gate).astype(BF16)


def _hgrn(hq, hf, hi, hg, lb_rows, g_out):
    b, s, wd = hq.shape
    c = HGRN_CHUNK
    wmat = jnp.asarray(_hgrn_decay_matrix(c), BF16)
    tt = np.arange(c)[:, None]
    ss = np.arange(c)[None, :]
    lvl_np = np.full((c, c), -1.0, np.float32)
    for li, m in reversed(list(enumerate(HGRN_LEVELS))):
        lvl_np[(ss < tt) & (tt // (2 * m) == ss // (2 * m)) & (tt // m != ss // m)] = li
    lvl = jnp.asarray(lvl_np)
    tok = pl.BlockSpec((1, HGRN_STEP, wd), lambda bi, ci: (bi, ci, 0))
    return pl.pallas_call(
        _hgrn_kernel,
        grid=(b, s // HGRN_STEP),
        in_specs=[tok, tok, tok, tok,
                  pl.BlockSpec((2, wd), lambda bi, ci: (0, 0)),
                  pl.BlockSpec((1, wd), lambda bi, ci: (0, 0)),
                  pl.BlockSpec(wmat.shape, lambda bi, ci: (0, 0)),
                  pl.BlockSpec((c, c), lambda bi, ci: (0, 0))],
        out_specs=tok,
        out_shape=jax.ShapeDtypeStruct((b, s, wd), BF16),
        scratch_shapes=[pltpu.VMEM((HGRN_HEADS, HGRN_DK, HGRN_DK), F32)],
        compiler_params=pltpu.CompilerParams(dimension_semantics=("arbitrary", "arbitrary"),
                                             vmem_limit_bytes=VMEM_LIMIT),
        name="hgrn",
    )(hq, hf, hi, hg, lb_rows, g_out, wmat, lvl)


def _outproj_kernel(x_ref, ao_ref, ho_ref, mod_ref, wo_ref, g2_ref, wr_ref, br_ref,
                    lstrict_ref, fold_ref, upper_ref, selrows_ref,
                    x1_out, h2_out, route_out, cnt_out, rows_out, lg_scr):
    @pl.when(pl.program_id(0) == 0)
    def _():
        lg_scr[...] = jnp.zeros_like(lg_scr)

    prev_logits = lg_scr[...]
    gate1 = mod_ref[0, 2:3, :]
    shift2 = mod_ref[0, 3:4, :]
    scale2 = mod_ref[0, 4:5, :]
    mix = _dot(ao_ref[...], wo_ref[0:ATTN_WIDTH, :]) + _dot(ho_ref[...], wo_ref[ATTN_WIDTH:D_MODEL, :])
    x1 = x_ref[...] + gate1 * mix
    x1_out[...] = x1
    ms = jnp.mean(x1 * x1, axis=-1, keepdims=True)
    h2 = (x1 * lax.rsqrt(ms + NORM_EPS) * g2_ref[...]) * (1.0 + scale2) + shift2
    h2_out[...] = h2.astype(BF16)

    h2_hi = h2.astype(BF16)
    h2_lo = (h2 - h2_hi.astype(F32)).astype(BF16)
    lg_scr[...] = (_dot(h2_hi, wr_ref[0]) + (_dot(h2_hi, wr_ref[1]) + _dot(h2_lo, wr_ref[0]))) + br_ref[...]
    _route_tile(prev_logits, lstrict_ref, fold_ref, upper_ref, selrows_ref, route_out, cnt_out, rows_out)


def _route_tile(logits, lstrict_ref, fold_ref, upper_ref, selrows_ref, route_out, cnt_out, rows_out):
    tm = logits.shape[0]
    lane = lax.broadcasted_iota(jnp.int32, (tm, LANES), 1)
    neg = -jnp.inf
    is_group = (lane >= N_EXPERTS) & (lane < N_EXPERTS + N_GROUPS)
    gl = jnp.where(is_group, logits, neg)
    gmax = jnp.max(gl, axis=-1, keepdims=True)
    gsum = jnp.sum(jnp.exp(gl - gmax), axis=-1, keepdims=True)
    group_p = 1.0 / gsum
    gidx = jnp.min(jnp.where(is_group & (gl == gmax), lane, LANES), axis=-1, keepdims=True) - N_EXPERTS
    in_group = (lane < N_EXPERTS) & ((lane // EXPERTS_PER_GROUP) == gidx)
    el = jnp.where(in_group, logits, neg)
    top1 = jnp.max(el, axis=-1, keepdims=True)
    idx1 = jnp.min(jnp.where(in_group & (el == top1), lane, LANES), axis=-1, keepdims=True)
    el2 = jnp.where(lane == idx1, neg, el)
    top2 = jnp.max(el2, axis=-1, keepdims=True)
    idx2 = jnp.min(jnp.where(in_group & (lane != idx1) & (el2 == top2), lane, LANES),
                   axis=-1, keepdims=True)
    e2 = jnp.exp(top2 - top1)
    w1 = group_p / (1.0 + e2)
    w2 = group_p * e2 / (1.0 + e2)
    route = jnp.where(lane == idx1, 1.0,
                      jnp.where(lane == idx2 + N_EXPERTS, 1.0,
                                jnp.where(lane == ROUTE_W_LANE, w1,
                                          jnp.where(lane == ROUTE_W_LANE + 1, w2, 0.0))))
    sel = selrows_ref[...]
    ln = lax.broadcasted_iota(jnp.int32, (MOE_TD, LANES), 1)
    for sub in range(tm // MOE_TD):
        tile = slice(sub * MOE_TD, (sub + 1) * MOE_TD)
        rt = route[tile]
        cnt_out[0, sub:sub + 1, :] = jnp.sum(rt, axis=0, keepdims=True).astype(jnp.int32)
        v = _local_slots(rt, lstrict_ref[...], fold_ref[...], upper_ref[...])
        hi = jnp.floor(v * (1.0 / MOE_ALIGN))
        lo = v - hi * MOE_ALIGN
        rows_out[0, sub * SUBLANES:(sub + 1) * SUBLANES, :] = (
            _dot_nt(sel, hi.astype(BF16)) * MOE_ALIGN + _dot_nt(sel, lo.astype(BF16)))
        r1 = jnp.sum(jnp.where(ln < N_EXPERTS, v, 0.0), axis=-1, keepdims=True)
        r2 = jnp.sum(jnp.where(ln < N_EXPERTS, 0.0, v), axis=-1, keepdims=True)
        route_out[tile, :] = jnp.where(ln == ROUTE_ROW_LANE, r1,
                                       jnp.where(ln == ROUTE_ROW_LANE + 1, r2, rt))


def _outproj(x, ao, ho, mod, wo, g2, wr, br, tm, tiles_per_batch):
    t, d = x.shape
    n = t // tm
    sub = tm // MOE_TD
    const = lambda shape: pl.BlockSpec(shape, lambda i: (0,) * len(shape))
    cur = lambda w: pl.BlockSpec((tm, w), lambda i: (jnp.minimum(i, n - 1), 0))
    prev = lambda shape: pl.BlockSpec(shape, lambda i: (jnp.maximum(i - 1, 0),) + (0,) * (len(shape) - 1))
    return pl.pallas_call(
        _outproj_kernel,
        grid=(n + 1,),
        in_specs=[cur(d), cur(ATTN_WIDTH), cur(HGRN_WIDTH),
                  pl.BlockSpec((1, 6, d), lambda i: (jnp.minimum(i, n - 1) // tiles_per_batch, 0, 0)),
                  const((d, d)), const((1, d)), const((2, d, LANES)), const((1, LANES)),
                  const((MOE_TD, MOE_TD)), const((LANES, LANES)), const((LANES, LANES)),
                  const((SUBLANES, LANES))],
        out_specs=[cur(d), cur(d), prev((tm, LANES)), prev((1, sub, LANES)),
                   prev((1, sub * SUBLANES, MOE_TD))],
        out_shape=[jax.ShapeDtypeStruct((t, d), F32),
                   jax.ShapeDtypeStruct((t, d), BF16),
                   jax.ShapeDtypeStruct((t, LANES), F32),
                   jax.ShapeDtypeStruct((n, sub, LANES), jnp.int32),
                   jax.ShapeDtypeStruct((n, sub * SUBLANES, MOE_TD), F32)],
        scratch_shapes=[pltpu.VMEM((tm, LANES), F32)],
        compiler_params=pltpu.CompilerParams(dimension_semantics=("arbitrary",),
                                             vmem_limit_bytes=VMEM_LIMIT),
        name="outproj",
    )(x, ao, ho, mod, wo, g2, wr, br, *_moe_constants())


def _ceil_to(v, m):
    return ((v + (m - 1)) // m) * m


def _cdiv_pow2(v, m):
    return lax.shift_right_logical(v + (m - 1), m.bit_length() - 1)


def _moe_constants():
    a = np.arange(LANES)
    ne = N_EXPERTS
    td = MOE_TD
    lstrict = np.tril(np.ones((td, td)), -1)
    fold = ((a[:, None] < 2 * ne) & (a[None, :] < 2 * ne) & (a[:, None] % ne == a[None, :] % ne))
    upper = ((a[:, None] < ne) & (a[None, :] < 2 * ne) & (a[:, None] < a[None, :] % ne))
    selrows = np.zeros((SUBLANES, LANES))
    selrows[0, :ne] = 1.0
    selrows[1, ne:2 * ne] = 1.0
    return tuple(jnp.asarray(m, BF16) for m in (lstrict, fold, upper, selrows))


def _local_slots(route, lstrict, fold, upper):
    lane = lax.broadcasted_iota(jnp.int32, route.shape, 1)
    member = jnp.where(lane < 2 * N_EXPERTS, route, 0.0)
    rank = _dot(_dot(lstrict, member.astype(BF16)).astype(BF16), fold)
    cnt = jnp.broadcast_to(jnp.sum(member, axis=0, keepdims=True), (SUBLANES, LANES))
    cnt = _dot(cnt.astype(BF16), fold)
    run = jnp.floor((cnt + (MOE_ALIGN - 1)) * (1.0 / MOE_ALIGN)) * MOE_ALIGN
    start = _dot(run.astype(BF16), upper)[0:1, :]
    return member * (start + rank)


def _plan_kernel(c_ref, base_ref, texp_ref, meta_ref, tail_ref, *, ntiles, n_row_tiles):
    ne = N_EXPERTS
    off = jnp.int32(0)
    for e in range(ne):
        def body(i, run, e=e, off=off):
            c = c_ref[i * 2 * ne + e] + c_ref[i * 2 * ne + ne + e]
            base_ref[i * ne + e] = off + run
            return run + _cdiv_pow2(c, MOE_ALIGN) * MOE_ALIGN
        total = lax.fori_loop(0, ntiles, body, jnp.int32(0), unroll=8)
        nt = (total + (MOE_TM - 1)) // MOE_TM
        first = off // MOE_TM

        def fill(j, carry, e=e, first=first):
            texp_ref[first + j] = e
            return carry
        lax.fori_loop(0, nt, fill, 0)
        tail_ref[e] = off + total
        tail_ref[ne + e] = (nt * MOE_TM - total) // MOE_ALIGN
        off = off + nt * MOE_TM
    nvalid = off // MOE_TM

    def fill_rest(j, carry):
        texp_ref[j] = ne - 1
        return carry
    lax.fori_loop(nvalid, n_row_tiles, fill_rest, 0)
    meta_ref[0] = nvalid


def _plan(counts_flat, ntiles, n_row_tiles):
    smem = pl.BlockSpec(memory_space=pltpu.SMEM)
    return pl.pallas_call(
        functools.partial(_plan_kernel, ntiles=ntiles, n_row_tiles=n_row_tiles),
        in_specs=[smem],
        out_specs=[smem, smem, smem, smem],
        out_shape=[jax.ShapeDtypeStruct((ntiles * N_EXPERTS,), jnp.int32),
                   jax.ShapeDtypeStruct((n_row_tiles,), jnp.int32),
                   jax.ShapeDtypeStruct((1,), jnp.int32),
                   jax.ShapeDtypeStruct((2 * N_EXPERTS,), jnp.int32)],
        name="moe_plan",
    )(counts_flat)


def _run_chunks(c_ref, base_ref, tile, fn):
    ne = N_EXPERTS
    local = jnp.int32(0)
    for e in range(ne):
        c = c_ref[tile * 2 * ne + e] + c_ref[tile * 2 * ne + ne + e]
        nchunk = _cdiv_pow2(c, MOE_ALIGN)
        hbm = base_ref[tile * ne + e]

        def body(j, carry, hbm=hbm, local=local):
            fn(pl.multiple_of(hbm + j * MOE_ALIGN, MOE_ALIGN),
               pl.multiple_of(local + j * MOE_ALIGN, MOE_ALIGN))
            return carry
        lax.fori_loop(0, nchunk, body, 0)
        local = local + nchunk * MOE_ALIGN
    return _cdiv_pow2(local, MOE_ALIGN)


def _dispatch_kernel(c_ref, base_ref, tail_ref, meta_ref, h2_ref, rows_ref, xs_ref,
                     xs_scr, zero_scr, sem, nch_ref, *, n_row_tiles):
    i = pl.program_id(0)
    n = pl.num_programs(0)
    slot = i % 2

    def copy(slot_, hbm, local):
        return pltpu.make_async_copy(xs_scr.at[slot_, pl.ds(local, MOE_ALIGN), :],
                                     xs_ref.at[pl.ds(hbm, MOE_ALIGN), :], sem.at[slot_])

    def wait_all(slot_):
        def body(j, carry):
            copy(slot_, 0, 0).wait()
            return carry
        lax.fori_loop(0, nch_ref[slot_], body, 0)

    @pl.when(i >= 2)
    def _():
        wait_all(slot)

    rows = rows_ref[0]
    slot_id = lax.broadcasted_iota(jnp.int32, (MOE_L, MOE_TD), 0)
    p1 = jnp.where(slot_id == rows[0:1, :].astype(jnp.int32), 1.0, 0.0)
    p2 = jnp.where(slot_id == rows[1:2, :].astype(jnp.int32), 1.0, 0.0)
    xs_scr[slot] = _dot((p1 + p2).astype(BF16), h2_ref[...]).astype(BF16)
    nch_ref[slot] = _run_chunks(c_ref, base_ref, i, lambda hbm, local: copy(slot, hbm, local).start())

    @pl.when(i == n - 1)
    def _():
        zero_scr[...] = jnp.zeros_like(zero_scr)

        def chunk(hbm):
            return pltpu.make_async_copy(zero_scr.at[pl.ds(0, MOE_ALIGN), :],
                                         xs_ref.at[pl.ds(hbm, MOE_ALIGN), :], sem.at[2])

        def tile(j):
            return pltpu.make_async_copy(zero_scr, xs_ref.at[pl.ds(j * MOE_TM, MOE_TM), :], sem.at[3])

        nchunks = jnp.int32(0)
        for e in range(N_EXPERTS):
            start = tail_ref[e]
            cnt = tail_ref[N_EXPERTS + e]

            def fill(j, carry, start=start):
                chunk(pl.multiple_of(start + j * MOE_ALIGN, MOE_ALIGN)).start()
                return carry
            lax.fori_loop(0, cnt, fill, 0)
            nchunks = nchunks + cnt

        def fill_tile(j, carry):
            tile(j).start()
            return carry
        lax.fori_loop(meta_ref[0], n_row_tiles, fill_tile, 0)

        wait_all(slot)

        @pl.when(n >= 2)
        def _():
            wait_all(1 - slot)

        def wait_chunk(j, carry):
            chunk(0).wait()
            return carry
        lax.fori_loop(0, nchunks, wait_chunk, 0)

        def wait_tile(j, carry):
            tile(0).wait()
            return carry
        lax.fori_loop(meta_ref[0], n_row_tiles, wait_tile, 0)


def _dispatch(counts_flat, base, tail, meta, h2, rows, n_rows):
    t, d = h2.shape
    ntiles = t // MOE_TD
    grid_spec = pltpu.PrefetchScalarGridSpec(
        num_scalar_prefetch=4,
        grid=(ntiles,),
        in_specs=[pl.BlockSpec((MOE_TD, d), lambda i, *_: (i, 0)),
                  pl.BlockSpec((1, SUBLANES, MOE_TD), lambda i, *_: (i, 0, 0))],
        out_specs=pl.BlockSpec(memory_space=pl.ANY),
        scratch_shapes=[pltpu.VMEM((2, MOE_L, d), BF16), pltpu.VMEM((MOE_TM, d), BF16),
                        pltpu.SemaphoreType.DMA((4,)), pltpu.SMEM((2,), jnp.int32)])
    return pl.pallas_call(
        functools.partial(_dispatch_kernel, n_row_tiles=n_rows // MOE_TM),
        grid_spec=grid_spec,
        out_shape=jax.ShapeDtypeStruct((n_rows, d), BF16),
        compiler_params=pltpu.CompilerParams(dimension_semantics=("arbitrary",),
                                             vmem_limit_bytes=VMEM_LIMIT),
        name="moe_dispatch",
    )(counts_flat, base, tail, meta, h2, rows)


def _experts_kernel(texp_ref, meta_ref, xs_ref, wg_ref, wu_ref, wd_ref, ys_ref):
    del texp_ref

    @pl.when(pl.program_id(0) >= meta_ref[0])
    def _():
        ys_ref[...] = jnp.zeros_like(ys_ref)

    @pl.when(pl.program_id(0) < meta_ref[0])
    def _():
        x = xs_ref[...]
        act = _silu(_dot(x, wg_ref[0])) * _dot(x, wu_ref[0])
        ys_ref[...] = _dot(act.astype(BF16), wd_ref[0]).astype(BF16)


def _experts(texp, meta, xs, wg, wu, wd):
    n_rows, d = xs.shape
    row = lambda w: pl.BlockSpec((MOE_TM, w), lambda i, te, mt: (jnp.minimum(i, mt[0] - 1), 0))
    wspec = lambda shape: pl.BlockSpec((1,) + shape, lambda i, te, mt: (te[i], 0, 0))
    grid_spec = pltpu.PrefetchScalarGridSpec(
        num_scalar_prefetch=2,
        grid=(n_rows // MOE_TM,),
        in_specs=[row(d), wspec((d, D_EXPERT)), wspec((d, D_EXPERT)), wspec((D_EXPERT, d))],
        out_specs=pl.BlockSpec((MOE_TM, d), lambda i, te, mt: (i, 0)))
    return pl.pallas_call(
        _experts_kernel,
        grid_spec=grid_spec,
        out_shape=jax.ShapeDtypeStruct((n_rows, d), BF16),
        compiler_params=pltpu.CompilerParams(dimension_semantics=("arbitrary",),
                                             vmem_limit_bytes=VMEM_LIMIT),
        name="moe_experts",
    )(texp, meta, xs, wg, wu, wd)


def _combine_kernel(c_ref, base_ref, x1_ref, route_ref, mod_ref, ys_ref, o_ref, ys_scr, sem,
                    nch_ref):
    i = pl.program_id(0)
    n = pl.num_programs(0)
    slot = i % 2

    def copy(slot_, hbm, local):
        return pltpu.make_async_copy(ys_ref.at[pl.ds(hbm, MOE_ALIGN), :],
                                     ys_scr.at[slot_, pl.ds(local, MOE_ALIGN), :], sem.at[slot_])

    def fetch(tile, slot_):
        nch_ref[slot_] = _run_chunks(c_ref, base_ref, tile,
                                     lambda hbm, local: copy(slot_, hbm, local).start())

    @pl.when(i == 0)
    def _():
        ys_scr[...] = jnp.zeros_like(ys_scr)
        fetch(0, 0)

    @pl.when(i + 1 < n)
    def _():
        fetch(i + 1, 1 - slot)

    route = route_ref[...]
    lane = lax.broadcasted_iota(jnp.int32, route.shape, 1)
    r1 = jnp.sum(jnp.where(lane == ROUTE_ROW_LANE, route, 0.0), axis=-1, keepdims=True)
    r2 = jnp.sum(jnp.where(lane == ROUTE_ROW_LANE + 1, route, 0.0), axis=-1, keepdims=True)
    r1, r2 = r1.astype(jnp.int32), r2.astype(jnp.int32)
    col = lax.broadcasted_iota(jnp.int32, (MOE_TD, MOE_L), 1)
    pick1 = jnp.where(col == r1, 1.0, 0.0).astype(BF16)
    pick2 = jnp.where(col == r2, 1.0, 0.0).astype(BF16)
    w1 = jnp.sum(jnp.where(lane == ROUTE_W_LANE, route, 0.0), axis=-1, keepdims=True)
    w2 = jnp.sum(jnp.where(lane == ROUTE_W_LANE + 1, route, 0.0), axis=-1, keepdims=True)

    def wait(j, carry):
        copy(slot, 0, 0).wait()
        return carry
    lax.fori_loop(0, nch_ref[slot], wait, 0)
    ys = ys_scr[slot]
    y = w1 * _dot(pick1, ys) + w2 * _dot(pick2, ys)
    o_ref[...] = x1_ref[...] + mod_ref[0, 5:6, :] * y


def _combine(counts_flat, base, x1, route, mod, ys, tiles_per_batch):
    t, d = x1.shape
    tok = lambda w: pl.BlockSpec((MOE_TD, w), lambda i, c, b: (i, 0))
    grid_spec = pltpu.PrefetchScalarGridSpec(
        num_scalar_prefetch=2,
        grid=(t // MOE_TD,),
        in_specs=[tok(d), tok(LANES),
                  pl.BlockSpec((1, 6, d), lambda i, c, b: (i // tiles_per_batch, 0, 0)),
                  pl.BlockSpec(memory_space=pl.ANY)],
        out_specs=tok(d),
        scratch_shapes=[pltpu.VMEM((2, MOE_L, d), BF16), pltpu.SemaphoreType.DMA((2,)),
                        pltpu.SMEM((2,), jnp.int32)])
    return pl.pallas_call(
        _combine_kernel,
        grid_spec=grid_spec,
        out_shape=jax.ShapeDtypeStruct((t, d), F32),
        compiler_params=pltpu.CompilerParams(dimension_semantics=("arbitrary",),
                                             vmem_limit_bytes=VMEM_LIMIT),
        name="moe_combine",
    )(counts_flat, base, x1, route, mod, ys)


def kernel(x, c, w_ada, b_ada, norm1_g, w_in, b_fox, q_norm_g, k_norm_g, attn_out_g, hgrn_lb,
           hgrn_out_g, w_out, norm2_g, w_router_group, b_router_group, w_router_expert,
           b_router_expert, w_gate, w_up, w_down):
    b, s, d = x.shape
    l = 0
    aw = ATTN_WIDTH
    mod = _ada(c, w_ada[l], b_ada[l]).reshape(b, 6, d)

    w = w_in[l]
    wq = w[:, 0:aw].astype(BF16)
    wk = w[:, aw:2 * aw].astype(BF16)
    wv = w[:, 2 * aw:3 * aw].astype(BF16)
    f0 = 3 * aw
    pad_f = LANES - FOX_PIECES * ATTN_HEADS
    wf = jnp.pad(jnp.tile(w[:, f0:f0 + ATTN_HEADS], (1, FOX_PIECES)), ((0, 0), (0, pad_f))).astype(BF16)
    wh = w[:, f0 + ATTN_HEADS:].astype(BF16)
    bfox = jnp.pad(jnp.tile(b_fox[l], FOX_PIECES), (0, pad_f)).reshape(1, LANES)
    gq = jnp.tile(q_norm_g[l], ATTN_HEADS).reshape(1, aw)
    gk = jnp.tile(k_norm_g[l], ATTN_HEADS).reshape(1, aw)

    tm = min(512, s)
    q, k, v, ck, hq, hf, hi, hg = _inproj(x, mod, norm1_g[l].reshape(1, d), wq, wk, wv, wf, wh,
                                           gq, gk, bfox, tm)
    ao = _fox(q, k, v, ck, attn_out_g[l].reshape(1, aw), min(256, s), 128)
    ho = _hgrn(hq, hf, hi, hg, hgrn_lb[0:2], hgrn_out_g[l].reshape(1, HGRN_WIDTH))

    wr = jnp.pad(jnp.concatenate([w_router_expert[l], w_router_group[l]], axis=1),
                 ((0, 0), (0, LANES - N_GROUPS - N_EXPERTS)))
    br = jnp.pad(jnp.concatenate([b_router_expert[l], b_router_group[l]]),
                 (0, LANES - N_GROUPS - N_EXPERTS)).reshape(1, LANES)
    wr_hi = wr.astype(BF16)
    wr2 = jnp.stack([wr_hi, (wr - wr_hi.astype(F32)).astype(BF16)])
    t = b * s
    x1, h2, route2, counts, rows = _outproj(x.reshape(t, d), ao.reshape(t, ATTN_WIDTH),
                                            ho.reshape(t, HGRN_WIDTH), mod, w_out[l].astype(BF16),
                                            norm2_g[l].reshape(1, d), wr2, br, tm, s // tm)
    ntiles = t // MOE_TD
    counts_flat = counts.reshape(ntiles, LANES)[:, :2 * N_EXPERTS].reshape(-1)
    n_rows = _ceil_to(2 * t + ntiles * N_EXPERTS * (MOE_ALIGN - 1), MOE_TM) + N_EXPERTS * MOE_TM
    base, texp, meta, tail = _plan(counts_flat, ntiles, n_rows // MOE_TM)
    xs = _dispatch(counts_flat, base, tail, meta, h2,
                   rows.reshape(ntiles, SUBLANES, MOE_TD), n_rows)
    ys = _experts(texp, meta, xs, w_gate[l].astype(BF16), w_up[l].astype(BF16),
                  w_down[l].astype(BF16))
    out = _combine(counts_flat, base, x1, route2, mod, ys, s // MOE_TD)
    return out.reshape(b, s, d)
```

```python
import functools

import numpy as np
import jax
import jax.numpy as jnp
from jax import lax
from jax.experimental import pallas as pl
from jax.experimental.pallas import tpu as pltpu

F32 = jnp.float32
BF16 = jnp.bfloat16

D_MODEL = 1024
ATTN_HEAD_DIM = 64
ATTN_WIDTH = 512
ATTN_HEADS = 8
HGRN_WIDTH = 512
HGRN_HEADS = 4
HGRN_DK = 128
N_GROUPS = 4
EXPERTS_PER_GROUP = 4
N_EXPERTS = 16
D_EXPERT = 512
NORM_EPS = 1e-6
LANES = 128
SUBLANES = 8
VMEM_LIMIT = 56 * 1024 * 1024

HGRN_CHUNK = 128
HGRN_STEP = 512
HGRN_LEVELS = (1, 2, 4, 8, 16, 32, 64)
HGRN_MXU_LEVELS = (1, 2)
ROUTE_W_LANE = 2 * N_EXPERTS
ROUTE_ROW_LANE = ROUTE_W_LANE + 2
MOE_TD = 256
MOE_ALIGN = 16
MOE_TM = 1024
MOE_L = 2 * MOE_TD + MOE_ALIGN * N_EXPERTS
LOG2E = 1.4426950408889634
FOX_GROUP = 4
FOX_PIECES = 3
FOX_CK_WIDTH = (ATTN_HEADS // FOX_GROUP) * LANES


def _sigmoid(x):
    return 1.0 / (1.0 + jnp.exp(-x))


def _silu(x):
    return x * (0.5 * jnp.tanh(0.5 * x) + 0.5)


def _split3(x):
    p1 = x.astype(BF16)
    r1 = x - p1.astype(F32)
    p2 = r1.astype(BF16)
    p3 = (r1 - p2.astype(F32)).astype(BF16)
    return p1, p2, p3


def _dot(a, b):
    return jnp.dot(a, b, preferred_element_type=F32)


def _dot_nt(a, b):
    return lax.dot_general(a, b, (((1,), (1,)), ((), ())), preferred_element_type=F32)


def _ada_kernel(c_ref, w_ref, b_ref, o_ref):
    c = c_ref[...]
    o_ref[...] = jnp.dot(_silu(c), w_ref[...], preferred_element_type=F32,
                         precision=lax.Precision.HIGHEST) + b_ref[...]


def _ada(c, w, b):
    bsz, d = c.shape
    n = w.shape[1]
    tn = 1024
    return pl.pallas_call(
        _ada_kernel,
        grid=(n // tn,),
        in_specs=[pl.BlockSpec((bsz, d), lambda j: (0, 0)),
                  pl.BlockSpec((d, tn), lambda j: (0, j)),
                  pl.BlockSpec((1, tn), lambda j: (0, j))],
        out_specs=pl.BlockSpec((bsz, tn), lambda j: (0, j)),
        out_shape=jax.ShapeDtypeStruct((bsz, n), F32),
        compiler_params=pltpu.CompilerParams(dimension_semantics=("arbitrary",),
                                             vmem_limit_bytes=VMEM_LIMIT),
        name="ada",
    )(c, w, b.reshape(1, n))


def _inproj_kernel(x_ref, mod_ref, g1_ref, wq_ref, wk_ref, wv_ref, wf_ref, wh_ref,
                   gq_ref, gk_ref, bf_ref, gsum_ref, tri_ref, place_ref,
                   q_out, k_out, v_out, ck_out, hq_out, hf_out, hi_out, hg_out,
                   carry_ref):
    si = pl.program_id(1)

    @pl.when(si == 0)
    def _():
        carry_ref[...] = jnp.zeros_like(carry_ref)

    x = x_ref[0]
    shift = mod_ref[0, 0:1, :]
    scale = mod_ref[0, 1:2, :]
    ms = jnp.mean(x * x, axis=-1, keepdims=True)
    h = (x * lax.rsqrt(ms + NORM_EPS) * g1_ref[...]) * (1.0 + scale) + shift
    hb = h.astype(BF16)

    def qk_norm(w_ref, g_ref, mult):
        t = _dot(hb, w_ref[...])
        ssq = _dot((t * t).astype(BF16), gsum_ref[...])
        return t * lax.rsqrt(ssq * (1.0 / ATTN_HEAD_DIM) + NORM_EPS) * (g_ref[...] * mult)

    q_out[0] = qk_norm(wq_ref, gq_ref, ATTN_HEAD_DIM ** -0.5 * LOG2E).T.astype(BF16)
    k_out[0] = qk_norm(wk_ref, gk_ref, 1.0).astype(BF16)
    v_out[0] = _dot(hb, wv_ref[...]).T.astype(BF16)

    af = _dot(hb, wf_ref[...]) + bf_ref[...]
    lf = jnp.minimum(af, 0.0) - jnp.log(1.0 + jnp.exp(-jnp.abs(af)))
    tri = tri_ref[...]
    p1, p2, p3 = _split3(lf)
    cum = (_dot(tri, p1) + _dot(tri, p2)) + _dot(tri, p3) + carry_ref[...]
    tm = cum.shape[0]
    carry_ref[...] = cum[tm - 1:tm, :]
    c1, c2, c3 = _split3(cum * (-LOG2E))
    lane = lax.broadcasted_iota(jnp.int32, cum.shape, 1)
    zero = jnp.zeros_like(c1)
    pieces = jnp.where(lane < ATTN_HEADS, c1,
                       jnp.where(lane < 2 * ATTN_HEADS, c2,
                                 jnp.where(lane < 3 * ATTN_HEADS, c3, zero)))
    ck_out[0] = _dot(pieces, place_ref[...]).astype(BF16)

    hq_out[0] = _dot(hb, wh_ref[:, 0 * HGRN_WIDTH:1 * HGRN_WIDTH]).astype(BF16)
    hf_out[0] = _dot(hb, wh_ref[:, 1 * HGRN_WIDTH:2 * HGRN_WIDTH]).astype(BF16)
    hi_out[0] = _dot(hb, wh_ref[:, 2 * HGRN_WIDTH:3 * HGRN_WIDTH]).astype(BF16)
    hg_out[0] = _dot(hb, wh_ref[:, 3 * HGRN_WIDTH:4 * HGRN_WIDTH]).astype(BF16)


def _inproj(x, mod, g1, wq, wk, wv, wf, wh, gq, gk, bfox, tm):
    b, s, d = x.shape
    gsum = jnp.asarray(np.kron(np.eye(ATTN_HEADS), np.ones((ATTN_HEAD_DIM, ATTN_HEAD_DIM))), BF16)
    tri = jnp.asarray(np.tril(np.ones((tm, tm))), BF16)
    place_np = np.zeros((LANES, FOX_CK_WIDTH), np.float32)
    for piece in range(FOX_PIECES):
        for hd in range(ATTN_HEADS):
            col = (hd // FOX_GROUP) * LANES + (hd % FOX_GROUP) * FOX_PIECES + piece
            place_np[piece * ATTN_HEADS + hd, col] = 1.0
    place = jnp.asarray(place_np, BF16)
    const = lambda shape: pl.BlockSpec(shape, lambda bi, si: (0,) * len(shape))
    tok = lambda w: pl.BlockSpec((1, tm, w), lambda bi, si: (bi, si, 0))
    tok_t = lambda w: pl.BlockSpec((1, w, tm), lambda bi, si: (bi, 0, si))
    act = lambda w: jax.ShapeDtypeStruct((b, s, w), BF16)
    act_t = lambda w: jax.ShapeDtypeStruct((b, w, s), BF16)
    return pl.pallas_call(
        _inproj_kernel,
        grid=(b, s // tm),
        in_specs=[tok(d),
                  pl.BlockSpec((1, 6, d), lambda bi, si: (bi, 0, 0)),
                  const((1, d)),
                  const((d, ATTN_WIDTH)), const((d, ATTN_WIDTH)), const((d, ATTN_WIDTH)),
                  const((d, LANES)), const((d, 4 * HGRN_WIDTH)),
                  const((1, ATTN_WIDTH)), const((1, ATTN_WIDTH)), const((1, LANES)),
                  const((ATTN_WIDTH, ATTN_WIDTH)), const((tm, tm)),
                  const((LANES, FOX_CK_WIDTH))],
        out_specs=[tok_t(ATTN_WIDTH), tok(ATTN_WIDTH), tok_t(ATTN_WIDTH),
                   tok(FOX_CK_WIDTH),
                   tok(HGRN_WIDTH), tok(HGRN_WIDTH), tok(HGRN_WIDTH), tok(HGRN_WIDTH)],
        out_shape=[act_t(ATTN_WIDTH), act(ATTN_WIDTH), act_t(ATTN_WIDTH),
                   act(FOX_CK_WIDTH),
                   act(HGRN_WIDTH), act(HGRN_WIDTH), act(HGRN_WIDTH), act(HGRN_WIDTH)],
        scratch_shapes=[pltpu.VMEM((1, LANES), F32)],
        compiler_params=pltpu.CompilerParams(dimension_semantics=("arbitrary", "arbitrary"),
                                             vmem_limit_bytes=VMEM_LIMIT),
        name="inproj",
    )(x, mod, g1, wq, wk, wv, wf, wh, gq, gk, bfox, gsum, tri, place)


def _fox_kernel(qt_ref, k_ref, vt_ref, ck_ref, g_ref, o_ref, st_scr, pt_scr, *, tq, tk):
    qi = pl.program_id(2)
    qt = qt_ref[0].astype(F32)
    row = lax.broadcasted_iota(jnp.int32, (LANES, tq), 0)
    first = row < ATTN_HEAD_DIM
    rhs = []
    for hd in range(FOX_GROUP):
        qp = qt[(hd // 2) * LANES:(hd // 2 + 1) * LANES, :]
        qh = jnp.where(first, qp, 0.0) if hd % 2 == 0 else jnp.where(first, 0.0, qp)
        ones = jnp.where((row >= hd * FOX_PIECES) & (row < (hd + 1) * FOX_PIECES), 1.0, 0.0)
        rhs.append(jnp.concatenate([qh, ones], axis=0).astype(BF16))

    def stage_qk(j, slot):
        k0 = pl.multiple_of(j * tk, tk)
        kb = k_ref[0, pl.ds(k0, tk), :]
        ckb = ck_ref[0, pl.ds(k0, tk), :]
        for hd in range(FOX_GROUP):
            pr = slice((hd // 2) * LANES, (hd // 2 + 1) * LANES)
            lhs = jnp.concatenate([kb[:, pr], ckb], axis=1)
            st_scr[slot, hd] = _dot(lhs, rhs[hd])

    def stage_softmax(j, slot, stats, masked):
        out = []
        for hd in range(FOX_GROUP):
            m, l = stats[hd]
            st = st_scr[slot, hd]
            if masked:
                key = j * tk + lax.broadcasted_iota(jnp.int32, (tk, tq), 0)
                qry = qi * tq + lax.broadcasted_iota(jnp.int32, (tk, tq), 1)
                st = jnp.where(key <= qry, st, -jnp.inf)
            m_new = jnp.maximum(m, jnp.max(st, axis=0, keepdims=True))
            alpha = jnp.exp2(m - m_new)
            pt = jnp.exp2(st - m_new)
            pt_scr[slot, hd] = pt.astype(BF16)
            out.append(((m_new, alpha * l + jnp.sum(pt, axis=0, keepdims=True)), alpha))
        return tuple(o[0] for o in out), tuple(o[1] for o in out)

    def stage_pv(j, slot, accs, alphas):
        k0 = pl.multiple_of(jnp.maximum(j, 0) * tk, tk)
        vtb = vt_ref[0, :, pl.ds(k0, tk)]
        out = []
        for hd in range(FOX_GROUP):
            vth = vtb[hd * ATTN_HEAD_DIM:(hd + 1) * ATTN_HEAD_DIM, :]
            out.append(alphas[hd] * accs[hd] + _dot(vth, pt_scr[slot, hd]))
        return tuple(out)

    per_q = tq // tk
    stats = tuple((jnp.full((1, tq), -1e30, F32), jnp.zeros((1, tq), F32))
                  for _ in range(FOX_GROUP))
    accs = tuple(jnp.zeros((ATTN_HEAD_DIM, tq), F32) for _ in range(FOX_GROUP))
    alphas = tuple(jnp.ones((1, tq), F32) for _ in range(FOX_GROUP))
    pt_scr[1] = jnp.zeros_like(pt_scr[1])
    stage_qk(0, 0)

    def full_blocks(i, carry):
        stats, accs, alphas = carry
        for u in range(per_q):
            j = i * per_q + u
            accs = stage_pv(j - 1, 1 - u, accs, alphas)
            stats, alphas = stage_softmax(j, u, stats, False)
            stage_qk(j + 1, 1 - u)
        return stats, accs, alphas

    stats, accs, alphas = lax.fori_loop(0, qi, full_blocks, (stats, accs, alphas))
    j = qi * per_q
    accs = stage_pv(j - 1, 1, accs, alphas)
    stats, alphas = stage_softmax(j, 0, stats, True)
    stage_qk(j + 1, 1)
    accs = stage_pv(j, 0, accs, alphas)
    stats, alphas = stage_softmax(j + 1, 1, stats, True)
    accs = stage_pv(j + 1, 1, accs, alphas)
    carry = tuple((stats[hd][0], stats[hd][1], accs[hd]) for hd in range(FOX_GROUP))

    head0 = lax.broadcasted_iota(jnp.int32, (tq, LANES), 1) < ATTN_HEAD_DIM
    for pr in range(FOX_GROUP // 2):
        (_, l0, a0), (_, l1, a1) = carry[2 * pr], carry[2 * pr + 1]
        ot = jnp.concatenate([a0 * (1.0 / l0), a1 * (1.0 / l1)], axis=0)
        o = ot.T
        osq = o * o
        ss0 = jnp.sum(jnp.where(head0, osq, 0.0), axis=-1, keepdims=True)
        ss1 = jnp.sum(jnp.where(head0, 0.0, osq), axis=-1, keepdims=True)
        ms = jnp.where(head0, ss0, ss1) * (1.0 / ATTN_HEAD_DIM)
        sl = slice(pr * LANES, (pr + 1) * LANES)
        o_ref[0, :, sl] = (o * lax.rsqrt(ms + NORM_EPS) * g_ref[:, sl]).astype(BF16)


def _fox(qt, k, vt, ck, g_out, tq, tk):
    b, s, _ = k.shape
    groups = ATTN_HEADS // FOX_GROUP
    gw = FOX_GROUP * ATTN_HEAD_DIM
    return pl.pallas_call(
        functools.partial(_fox_kernel, tq=tq, tk=tk),
        grid=(b, groups, s // tq),
        in_specs=[pl.BlockSpec((1, gw, tq), lambda bi, g, qi: (bi, g, qi)),
                  pl.BlockSpec((1, s, gw), lambda bi, g, qi: (bi, 0, g)),
                  pl.BlockSpec((1, gw, s), lambda bi, g, qi: (bi, g, 0)),
                  pl.BlockSpec((1, s, LANES), lambda bi, g, qi: (bi, 0, g)),
                  pl.BlockSpec((1, gw), lambda bi, g, qi: (0, g))],
        out_specs=pl.BlockSpec((1, tq, gw), lambda bi, g, qi: (bi, qi, g)),
        out_shape=jax.ShapeDtypeStruct((b, s, ATTN_WIDTH), BF16),
        scratch_shapes=[pltpu.VMEM((2, FOX_GROUP, tk, tq), F32),
                        pltpu.VMEM((2, FOX_GROUP, tk, tq), BF16)],
        compiler_params=pltpu.CompilerParams(
            dimension_semantics=("arbitrary", "arbitrary", "arbitrary"),
            vmem_limit_bytes=VMEM_LIMIT),
        name="fox",
    )(qt, k, vt, ck, g_out)


def _hgrn_decay_matrix(c):
    t = np.arange(c)[:, None]
    j = np.arange(c)[None, :]
    blocks = [(j <= t)]
    for m in HGRN_MXU_LEVELS:
        mid = (t // (2 * m)) * (2 * m) + m
        right = (t % (2 * m)) >= m
        blocks.append(np.where(right, (j >= mid) & (j <= t), (j > t) & (j < mid)))
    return np.concatenate(blocks, axis=0).astype(np.float32)


def _hgrn_kernel(hq_ref, hf_ref, hi_ref, hg_ref, lb_ref, g_ref, w_ref, lvl_ref, o_ref, st_ref):
    ci = pl.program_id(1)
    c = HGRN_CHUNK

    @pl.when(ci == 0)
    def _():
        st_ref[...] = jnp.zeros_like(st_ref)

    r0 = lb_ref[0:1, :]
    r1 = lb_ref[1:2, :]
    rmax = jnp.maximum(r0, r1)
    e0 = jnp.exp(r0 - rmax)
    lb = e0 / (e0 + jnp.exp(r1 - rmax))

    lvl = lvl_ref[...]
    at_level = [lvl == float(li) for li in range(len(HGRN_LEVELS))]
    for sub in range(hq_ref.shape[1] // c):
        rows = slice(sub * c, (sub + 1) * c)
        f = lb + (1.0 - lb) * _sigmoid(hf_ref[0, rows, :].astype(F32))
        g = jnp.log2(f)
        g1 = g.astype(BF16)
        g2 = (g - g1.astype(F32)).astype(BF16)
        w = w_ref[...]
        xw = _dot(w, g1) + _dot(w, g2)
        cum = xw[0:c]
        cum3 = cum.reshape(c // SUBLANES, SUBLANES, cum.shape[1])

        def level_sums(m):
            if m in HGRN_MXU_LEVELS:
                at = 1 + HGRN_MXU_LEVELS.index(m)
                return xw[at * c:(at + 1) * c]
            if 2 * m == SUBLANES:
                ref = cum3[:, m - 1:m, :]
            else:
                last = cum3[:, SUBLANES - 1:SUBLANES, :]
                per = 2 * m // SUBLANES
                pick = [(r // per) * per + per // 2 - 1 for r in range(c // SUBLANES)]
                ref = jnp.concatenate([last[p:p + 1] for p in pick], axis=0)
            return -jnp.abs(cum - jnp.broadcast_to(ref, cum3.shape).reshape(cum.shape))

        e_levels = [jnp.exp2(level_sums(m)) for m in HGRN_LEVELS]
        e_pre_all = jnp.exp2(cum)
        e_suf_all = jnp.exp2(cum[c - 1:c, :] - cum)
        q_all = _silu(hq_ref[0, rows, :].astype(F32))
        k_all = 1.0 - f
        v_all = hi_ref[0, rows, :].astype(F32)


        for hd in range(HGRN_HEADS):
            sl = slice(hd * HGRN_DK, (hd + 1) * HGRN_DK)
            q, k, v = q_all[:, sl], k_all[:, sl], v_all[:, sl]
            e_pre = e_pre_all[:, sl]
            e_suf = e_suf_all[:, sl]

            a = jnp.zeros((c, c), F32)
            for li in range(len(HGRN_LEVELS)):
                e = e_levels[li][:, sl]
                a = jnp.where(at_level[li], _dot_nt((q * e).astype(BF16), (k * e).astype(BF16)), a)
            vb = v.astype(BF16)
            out = _dot(a.astype(BF16), vb)

            out = out + jnp.sum(q * k, axis=-1, keepdims=True) * v

            st = st_ref[hd]
            out = out + _dot_nt((q * e_pre).astype(BF16), st.astype(BF16))
            kdec = (k * e_suf).astype(BF16)
            upd = lax.dot_general(vb, kdec, (((0,), (0,)), ((), ())), preferred_element_type=F32)
            st_ref[hd] = st * e_pre[c - 1:c, :] + upd

            ms = jnp.mean(out * out, axis=-1, keepdims=True)
            gate = _silu(hg_ref[0, rows, sl].astype(F32))
            o_ref[0, rows, sl] = (out * lax.rsqrt(ms + NORM_EPS) * g_ref[:, sl] * gate).astype(BF16)


def _hgrn(hq, hf, hi, hg, lb_rows, g_out):
    b, s, wd = hq.shape
    c = HGRN_CHUNK
    wmat = jnp.asarray(_hgrn_decay_matrix(c), BF16)
    tt = np.arange(c)[:, None]
    ss = np.arange(c)[None, :]
    lvl_np = np.full((c, c), -1.0, np.float32)
    for li, m in reversed(list(enumerate(HGRN_LEVELS))):
        lvl_np[(ss < tt) & (tt // (2 * m) == ss // (2 * m)) & (tt // m != ss // m)] = li
    lvl = jnp.asarray(lvl_np)
    tok = pl.BlockSpec((1, HGRN_STEP, wd), lambda bi, ci: (bi, ci, 0))
    return pl.pallas_call(
        _hgrn_kernel,
        grid=(b, s // HGRN_STEP),
        in_specs=[tok, tok, tok, tok,
                  pl.BlockSpec((2, wd), lambda bi, ci: (0, 0)),
                  pl.BlockSpec((1, wd), lambda bi, ci: (0, 0)),
                  pl.BlockSpec(wmat.shape, lambda bi, ci: (0, 0)),
                  pl.BlockSpec((c, c), lambda bi, ci: (0, 0))],
        out_specs=tok,
        out_shape=jax.ShapeDtypeStruct((b, s, wd), BF16),
        scratch_shapes=[pltpu.VMEM((HGRN_HEADS, HGRN_DK, HGRN_DK), F32)],
        compiler_params=pltpu.CompilerParams(dimension_semantics=("arbitrary", "arbitrary"),
                                             vmem_limit_bytes=VMEM_LIMIT),
        name="hgrn",
    )(hq, hf, hi, hg, lb_rows, g_out, wmat, lvl)


def _outproj_kernel(x_ref, ao_ref, ho_ref, mod_ref, wo_ref, g2_ref, wr_ref, br_ref,
                    lstrict_ref, fold_ref, upper_ref, selrows_ref,
                    x1_out, h2_out, route_out, cnt_out, rows_out, lg_scr):
    @pl.when(pl.program_id(0) == 0)
    def _():
        lg_scr[...] = jnp.zeros_like(lg_scr)

    prev_logits = lg_scr[...]
    gate1 = mod_ref[0, 2:3, :]
    shift2 = mod_ref[0, 3:4, :]
    scale2 = mod_ref[0, 4:5, :]
    mix = _dot(ao_ref[...], wo_ref[0:ATTN_WIDTH, :]) + _dot(ho_ref[...], wo_ref[ATTN_WIDTH:D_MODEL, :])
    x1 = x_ref[...] + gate1 * mix
    x1_out[...] = x1
    ms = jnp.mean(x1 * x1, axis=-1, keepdims=True)
    h2 = (x1 * lax.rsqrt(ms + NORM_EPS) * g2_ref[...]) * (1.0 + scale2) + shift2
    h2_out[...] = h2.astype(BF16)

    h2_hi = h2.astype(BF16)
    h2_lo = (h2 - h2_hi.astype(F32)).astype(BF16)
    lg_scr[...] = (_dot(h2_hi, wr_ref[0]) + (_dot(h2_hi, wr_ref[1]) + _dot(h2_lo, wr_ref[0]))) + br_ref[...]
    _route_tile(prev_logits, lstrict_ref, fold_ref, upper_ref, selrows_ref, route_out, cnt_out, rows_out)


def _route_tile(logits, lstrict_ref, fold_ref, upper_ref, selrows_ref, route_out, cnt_out, rows_out):
    tm = logits.shape[0]
    lane = lax.broadcasted_iota(jnp.int32, (tm, LANES), 1)
    neg = -jnp.inf
    is_group = (lane >= N_EXPERTS) & (lane < N_EXPERTS + N_GROUPS)
    gl = jnp.where(is_group, logits, neg)
    gmax = jnp.max(gl, axis=-1, keepdims=True)
    gsum = jnp.sum(jnp.exp(gl - gmax), axis=-1, keepdims=True)
    group_p = 1.0 / gsum
    gidx = jnp.min(jnp.where(is_group & (gl == gmax), lane, LANES), axis=-1, keepdims=True) - N_EXPERTS
    in_group = (lane < N_EXPERTS) & ((lane // EXPERTS_PER_GROUP) == gidx)
    el = jnp.where(in_group, logits, neg)
    top1 = jnp.max(el, axis=-1, keepdims=True)
    idx1 = jnp.min(jnp.where(in_group & (el == top1), lane, LANES), axis=-1, keepdims=True)
    el2 = jnp.where(lane == idx1, neg, el)
    top2 = jnp.max(el2, axis=-1, keepdims=True)
    idx2 = jnp.min(jnp.where(in_group & (lane != idx1) & (el2 == top2), lane, LANES),
                   axis=-1, keepdims=True)
    e2 = jnp.exp(top2 - top1)
    w1 = group_p / (1.0 + e2)
    w2 = group_p * e2 / (1.0 + e2)
    route = jnp.where(lane == idx1, 1.0,
                      jnp.where(lane == idx2 + N_EXPERTS, 1.0,
                                jnp.where(lane == ROUTE_W_LANE, w1,
                                          jnp.where(lane == ROUTE_W_LANE + 1, w2, 0.0))))
    sel = selrows_ref[...]
    ln = lax.broadcasted_iota(jnp.int32, (MOE_TD, LANES), 1)
    for sub in range(tm // MOE_TD):
        tile = slice(sub * MOE_TD, (sub + 1) * MOE_TD)
        rt = route[tile]
        cnt_out[0, sub:sub + 1, :] = jnp.sum(rt, axis=0, keepdims=True).astype(jnp.int32)
        v = _local_slots(rt, lstrict_ref[...], fold_ref[...], upper_ref[...])
        hi = jnp.floor(v * (1.0 / MOE_ALIGN))
        lo = v - hi * MOE_ALIGN
        rows_out[0, sub * SUBLANES:(sub + 1) * SUBLANES, :] = (
            _dot_nt(sel, hi.astype(BF16)) * MOE_ALIGN + _dot_nt(sel, lo.astype(BF16)))
        r1 = jnp.sum(jnp.where(ln < N_EXPERTS, v, 0.0), axis=-1, keepdims=True)
        r2 = jnp.sum(jnp.where(ln < N_EXPERTS, 0.0, v), axis=-1, keepdims=True)
        route_out[tile, :] = jnp.where(ln == ROUTE_ROW_LANE, r1,
                                       jnp.where(ln == ROUTE_ROW_LANE + 1, r2, rt))


def _outproj(x, ao, ho, mod, wo, g2, wr, br, tm, tiles_per_batch):
    t, d = x.shape
    n = t // tm
    sub = tm // MOE_TD
    const = lambda shape: pl.BlockSpec(shape, lambda i: (0,) * len(shape))
    cur = lambda w: pl.BlockSpec((tm, w), lambda i: (jnp.minimum(i, n - 1), 0))
    prev = lambda shape: pl.BlockSpec(shape, lambda i: (jnp.maximum(i - 1, 0),) + (0,) * (len(shape) - 1))
    return pl.pallas_call(
        _outproj_kernel,
        grid=(n + 1,),
        in_specs=[cur(d), cur(ATTN_WIDTH), cur(HGRN_WIDTH),
                  pl.BlockSpec((1, 6, d), lambda i: (jnp.minimum(i, n - 1) // tiles_per_batch, 0, 0)),
                  const((d, d)), const((1, d)), const((2, d, LANES)), const((1, LANES)),
                  const((MOE_TD, MOE_TD)), const((LANES, LANES)), const((LANES, LANES)),
                  const((SUBLANES, LANES))],
        out_specs=[cur(d), cur(d), prev((tm, LANES)), prev((1, sub, LANES)),
                   prev((1, sub * SUBLANES, MOE_TD))],
        out_shape=[jax.ShapeDtypeStruct((t, d), F32),
                   jax.ShapeDtypeStruct((t, d), BF16),
                   jax.ShapeDtypeStruct((t, LANES), F32),
                   jax.ShapeDtypeStruct((n, sub, LANES), jnp.int32),
                   jax.ShapeDtypeStruct((n, sub * SUBLANES, MOE_TD), F32)],
        scratch_shapes=[pltpu.VMEM((tm, LANES), F32)],
        compiler_params=pltpu.CompilerParams(dimension_semantics=("arbitrary",),
                                             vmem_limit_bytes=VMEM_LIMIT),
        name="outproj",
    )(x, ao, ho, mod, wo, g2, wr, br, *_moe_constants())


def _ceil_to(v, m):
    return ((v + (m - 1)) // m) * m


def _cdiv_pow2(v, m):
    return lax.shift_right_logical(v + (m - 1), m.bit_length() - 1)


def _moe_constants():
    a = np.arange(LANES)
    ne = N_EXPERTS
    td = MOE_TD
    lstrict = np.tril(np.ones((td, td)), -1)
    fold = ((a[:, None] < 2 * ne) & (a[None, :] < 2 * ne) & (a[:, None] % ne == a[None, :] % ne))
    upper = ((a[:, None] < ne) & (a[None, :] < 2 * ne) & (a[:, None] < a[None, :] % ne))
    selrows = np.zeros((SUBLANES, LANES))
    selrows[0, :ne] = 1.0
    selrows[1, ne:2 * ne] = 1.0
    return tuple(jnp.asarray(m, BF16) for m in (lstrict, fold, upper, selrows))


def _local_slots(route, lstrict, fold, upper):
    lane = lax.broadcasted_iota(jnp.int32, route.shape, 1)
    member = jnp.where(lane < 2 * N_EXPERTS, route, 0.0)
    rank = _dot(_dot(lstrict, member.astype(BF16)).astype(BF16), fold)
    cnt = jnp.broadcast_to(jnp.sum(member, axis=0, keepdims=True), (SUBLANES, LANES))
    cnt = _dot(cnt.astype(BF16), fold)
    run = jnp.floor((cnt + (MOE_ALIGN - 1)) * (1.0 / MOE_ALIGN)) * MOE_ALIGN
    start = _dot(run.astype(BF16), upper)[0:1, :]
    return member * (start + rank)


def _plan_kernel(c_ref, base_ref, texp_ref, meta_ref, tail_ref, *, ntiles, n_row_tiles):
    ne = N_EXPERTS
    off = jnp.int32(0)
    for e in range(ne):
        def body(i, run, e=e, off=off):
            c = c_ref[i * 2 * ne + e] + c_ref[i * 2 * ne + ne + e]
            base_ref[i * ne + e] = off + run
            return run + _cdiv_pow2(c, MOE_ALIGN) * MOE_ALIGN
        total = lax.fori_loop(0, ntiles, body, jnp.int32(0), unroll=8)
        nt = (total + (MOE_TM - 1)) // MOE_TM
        first = off // MOE_TM

        def fill(j, carry, e=e, first=first):
            texp_ref[first + j] = e
            return carry
        lax.fori_loop(0, nt, fill, 0)
        tail_ref[e] = off + total
        tail_ref[ne + e] = (nt * MOE_TM - total) // MOE_ALIGN
        off = off + nt * MOE_TM
    nvalid = off // MOE_TM

    def fill_rest(j, carry):
        texp_ref[j] = ne - 1
        return carry
    lax.fori_loop(nvalid, n_row_tiles, fill_rest, 0)
    meta_ref[0] = nvalid


def _plan(counts_flat, ntiles, n_row_tiles):
    smem = pl.BlockSpec(memory_space=pltpu.SMEM)
    return pl.pallas_call(
        functools.partial(_plan_kernel, ntiles=ntiles, n_row_tiles=n_row_tiles),
        in_specs=[smem],
        out_specs=[smem, smem, smem, smem],
        out_shape=[jax.ShapeDtypeStruct((ntiles * N_EXPERTS,), jnp.int32),
                   jax.ShapeDtypeStruct((n_row_tiles,), jnp.int32),
                   jax.ShapeDtypeStruct((1,), jnp.int32),
                   jax.ShapeDtypeStruct((2 * N_EXPERTS,), jnp.int32)],
        name="moe_plan",
    )(counts_flat)


def _run_chunks(c_ref, base_ref, tile, fn):
    ne = N_EXPERTS
    local = jnp.int32(0)
    for e in range(ne):
        c = c_ref[tile * 2 * ne + e] + c_ref[tile * 2 * ne + ne + e]
        nchunk = _cdiv_pow2(c, MOE_ALIGN)
        hbm = base_ref[tile * ne + e]

        def body(j, carry, hbm=hbm, local=local):
            fn(pl.multiple_of(hbm + j * MOE_ALIGN, MOE_ALIGN),
               pl.multiple_of(local + j * MOE_ALIGN, MOE_ALIGN))
            return carry
        lax.fori_loop(0, nchunk, body, 0)
        local = local + nchunk * MOE_ALIGN
    return _cdiv_pow2(local, MOE_ALIGN)


def _dispatch_kernel(c_ref, base_ref, tail_ref, meta_ref, h2_ref, rows_ref, xs_ref,
                     xs_scr, zero_scr, sem, nch_ref, *, n_row_tiles):
    i = pl.program_id(0)
    n = pl.num_programs(0)
    slot = i % 2

    def copy(slot_, hbm, local):
        return pltpu.make_async_copy(xs_scr.at[slot_, pl.ds(local, MOE_ALIGN), :],
                                     xs_ref.at[pl.ds(hbm, MOE_ALIGN), :], sem.at[slot_])

    def wait_all(slot_):
        def body(j, carry):
            copy(slot_, 0, 0).wait()
            return carry
        lax.fori_loop(0, nch_ref[slot_], body, 0)

    @pl.when(i >= 2)
    def _():
        wait_all(slot)

    rows = rows_ref[0]
    slot_id = lax.broadcasted_iota(jnp.int32, (MOE_L, MOE_TD), 0)
    p1 = jnp.where(slot_id == rows[0:1, :].astype(jnp.int32), 1.0, 0.0)
    p2 = jnp.where(slot_id == rows[1:2, :].astype(jnp.int32), 1.0, 0.0)
    xs_scr[slot] = _dot((p1 + p2).astype(BF16), h2_ref[...]).astype(BF16)
    nch_ref[slot] = _run_chunks(c_ref, base_ref, i, lambda hbm, local: copy(slot, hbm, local).start())

    @pl.when(i == n - 1)
    def _():
        zero_scr[...] = jnp.zeros_like(zero_scr)

        def chunk(hbm):
            return pltpu.make_async_copy(zero_scr.at[pl.ds(0, MOE_ALIGN), :],
                                         xs_ref.at[pl.ds(hbm, MOE_ALIGN), :], sem.at[2])

        def tile(j):
            return pltpu.make_async_copy(zero_scr, xs_ref.at[pl.ds(j * MOE_TM, MOE_TM), :], sem.at[3])

        nchunks = jnp.int32(0)
        for e in range(N_EXPERTS):
            start = tail_ref[e]
            cnt = tail_ref[N_EXPERTS + e]

            def fill(j, carry, start=start):
                chunk(pl.multiple_of(start + j * MOE_ALIGN, MOE_ALIGN)).start()
                return carry
            lax.fori_loop(0, cnt, fill, 0)
            nchunks = nchunks + cnt

        def fill_tile(j, carry):
            tile(j).start()
            return carry
        lax.fori_loop(meta_ref[0], n_row_tiles, fill_tile, 0)

        wait_all(slot)

        @pl.when(n >= 2)
        def _():
            wait_all(1 - slot)

        def wait_chunk(j, carry):
            chunk(0).wait()
            return carry
        lax.fori_loop(0, nchunks, wait_chunk, 0)

        def wait_tile(j, carry):
            tile(0).wait()
            return carry
        lax.fori_loop(meta_ref[0], n_row_tiles, wait_tile, 0)


def _dispatch(counts_flat, base, tail, meta, h2, rows, n_rows):
    t, d = h2.shape
    ntiles = t // MOE_TD
    grid_spec = pltpu.PrefetchScalarGridSpec(
        num_scalar_prefetch=4,
        grid=(ntiles,),
        in_specs=[pl.BlockSpec((MOE_TD, d), lambda i, *_: (i, 0)),
                  pl.BlockSpec((1, SUBLANES, MOE_TD), lambda i, *_: (i, 0, 0))],
        out_specs=pl.BlockSpec(memory_space=pl.ANY),
        scratch_shapes=[pltpu.VMEM((2, MOE_L, d), BF16), pltpu.VMEM((MOE_TM, d), BF16),
                        pltpu.SemaphoreType.DMA((4,)), pltpu.SMEM((2,), jnp.int32)])
    return pl.pallas_call(
        functools.partial(_dispatch_kernel, n_row_tiles=n_rows // MOE_TM),
        grid_spec=grid_spec,
        out_shape=jax.ShapeDtypeStruct((n_rows, d), BF16),
        compiler_params=pltpu.CompilerParams(dimension_semantics=("arbitrary",),
                                             vmem_limit_bytes=VMEM_LIMIT),
        name="moe_dispatch",
    )(counts_flat, base, tail, meta, h2, rows)


def _experts_kernel(texp_ref, meta_ref, xs_ref, wg_ref, wu_ref, wd_ref, ys_ref):
    del texp_ref

    @pl.when(pl.program_id(0) >= meta_ref[0])
    def _():
        ys_ref[...] = jnp.zeros_like(ys_ref)

    @pl.when(pl.program_id(0) < meta_ref[0])
    def _():
        x = xs_ref[...]
        act = _silu(_dot(x, wg_ref[0])) * _dot(x, wu_ref[0])
        ys_ref[...] = _dot(act.astype(BF16), wd_ref[0]).astype(BF16)


def _experts(texp, meta, xs, wg, wu, wd):
    n_rows, d = xs.shape
    row = lambda w: pl.BlockSpec((MOE_TM, w), lambda i, te, mt: (jnp.minimum(i, mt[0] - 1), 0))
    wspec = lambda shape: pl.BlockSpec((1,) + shape, lambda i, te, mt: (te[i], 0, 0))
    grid_spec = pltpu.PrefetchScalarGridSpec(
        num_scalar_prefetch=2,
        grid=(n_rows // MOE_TM,),
        in_specs=[row(d), wspec((d, D_EXPERT)), wspec((d, D_EXPERT)), wspec((D_EXPERT, d))],
        out_specs=pl.BlockSpec((MOE_TM, d), lambda i, te, mt: (i, 0)))
    return pl.pallas_call(
        _experts_kernel,
        grid_spec=grid_spec,
        out_shape=jax.ShapeDtypeStruct((n_rows, d), BF16),
        compiler_params=pltpu.CompilerParams(dimension_semantics=("arbitrary",),
                                             vmem_limit_bytes=VMEM_LIMIT),
        name="moe_experts",
    )(texp, meta, xs, wg, wu, wd)


def _combine_kernel(c_ref, base_ref, x1_ref, route_ref, mod_ref, ys_ref, o_ref, ys_scr, sem,
                    nch_ref):
    i = pl.program_id(0)
    n = pl.num_programs(0)
    slot = i % 2

    def copy(slot_, hbm, local):
        return pltpu.make_async_copy(ys_ref.at[pl.ds(hbm, MOE_ALIGN), :],
                                     ys_scr.at[slot_, pl.ds(local, MOE_ALIGN), :], sem.at[slot_])

    def fetch(tile, slot_):
        nch_ref[slot_] = _run_chunks(c_ref, base_ref, tile,
                                     lambda hbm, local: copy(slot_, hbm, local).start())

    @pl.when(i == 0)
    def _():
        ys_scr[...] = jnp.zeros_like(ys_scr)
        fetch(0, 0)

    @pl.when(i + 1 < n)
    def _():
        fetch(i + 1, 1 - slot)

    route = route_ref[...]
    lane = lax.broadcasted_iota(jnp.int32, route.shape, 1)
    r1 = jnp.sum(jnp.where(lane == ROUTE_ROW_LANE, route, 0.0), axis=-1, keepdims=True)
    r2 = jnp.sum(jnp.where(lane == ROUTE_ROW_LANE + 1, route, 0.0), axis=-1, keepdims=True)
    r1, r2 = r1.astype(jnp.int32), r2.astype(jnp.int32)
    col = lax.broadcasted_iota(jnp.int32, (MOE_TD, MOE_L), 1)
    pick1 = jnp.where(col == r1, 1.0, 0.0).astype(BF16)
    pick2 = jnp.where(col == r2, 1.0, 0.0).astype(BF16)
    w1 = jnp.sum(jnp.where(lane == ROUTE_W_LANE, route, 0.0), axis=-1, keepdims=True)
    w2 = jnp.sum(jnp.where(lane == ROUTE_W_LANE + 1, route, 0.0), axis=-1, keepdims=True)

    def wait(j, carry):
        copy(slot, 0, 0).wait()
        return carry
    lax.fori_loop(0, nch_ref[slot], wait, 0)
    ys = ys_scr[slot]
    y = w1 * _dot(pick1, ys) + w2 * _dot(pick2, ys)
    o_ref[...] = x1_ref[...] + mod_ref[0, 5:6, :] * y


def _combine(counts_flat, base, x1, route, mod, ys, tiles_per_batch):
    t, d = x1.shape
    tok = lambda w: pl.BlockSpec((MOE_TD, w), lambda i, c, b: (i, 0))
    grid_spec = pltpu.PrefetchScalarGridSpec(
        num_scalar_prefetch=2,
        grid=(t // MOE_TD,),
        in_specs=[tok(d), tok(LANES),
                  pl.BlockSpec((1, 6, d), lambda i, c, b: (i // tiles_per_batch, 0, 0)),
                  pl.BlockSpec(memory_space=pl.ANY)],
        out_specs=tok(d),
        scratch_shapes=[pltpu.VMEM((2, MOE_L, d), BF16), pltpu.SemaphoreType.DMA((2,)),
                        pltpu.SMEM((2,), jnp.int32)])
    return pl.pallas_call(
        _combine_kernel,
        grid_spec=grid_spec,
        out_shape=jax.ShapeDtypeStruct((t, d), F32),
        compiler_params=pltpu.CompilerParams(dimension_semantics=("arbitrary",),
                                             vmem_limit_bytes=VMEM_LIMIT),
        name="moe_combine",
    )(counts_flat, base, x1, route, mod, ys)


def kernel(x, c, w_ada, b_ada, norm1_g, w_in, b_fox, q_norm_g, k_norm_g, attn_out_g, hgrn_lb,
           hgrn_out_g, w_out, norm2_g, w_router_group, b_router_group, w_router_expert,
           b_router_expert, w_gate, w_up, w_down):
    b, s, d = x.shape
    l = 0
    aw = ATTN_WIDTH
    mod = _ada(c, w_ada[l], b_ada[l]).reshape(b, 6, d)

    w = w_in[l]
    wq = w[:, 0:aw].astype(BF16)
    wk = w[:, aw:2 * aw].astype(BF16)
    wv = w[:, 2 * aw:3 * aw].astype(BF16)
    f0 = 3 * aw
    pad_f = LANES - FOX_PIECES * ATTN_HEADS
    wf = jnp.pad(jnp.tile(w[:, f0:f0 + ATTN_HEADS], (1, FOX_PIECES)), ((0, 0), (0, pad_f))).astype(BF16)
    wh = w[:, f0 + ATTN_HEADS:].astype(BF16)
    bfox = jnp.pad(jnp.tile(b_fox[l], FOX_PIECES), (0, pad_f)).reshape(1, LANES)
    gq = jnp.tile(q_norm_g[l], ATTN_HEADS).reshape(1, aw)
    gk = jnp.tile(k_norm_g[l], ATTN_HEADS).reshape(1, aw)

    tm = min(512, s)
    q, k, v, ck, hq, hf, hi, hg = _inproj(x, mod, norm1_g[l].reshape(1, d), wq, wk, wv, wf, wh,
                                           gq, gk, bfox, tm)
    ao = _fox(q, k, v, ck, attn_out_g[l].reshape(1, aw), min(256, s), 128)
    ho = _hgrn(hq, hf, hi, hg, hgrn_lb[0:2], hgrn_out_g[l].reshape(1, HGRN_WIDTH))

    wr = jnp.pad(jnp.concatenate([w_router_expert[l], w_router_group[l]], axis=1),
                 ((0, 0), (0, LANES - N_GROUPS - N_EXPERTS)))
    br = jnp.pad(jnp.concatenate([b_router_expert[l], b_router_group[l]]),
                 (0, LANES - N_GROUPS - N_EXPERTS)).reshape(1, LANES)
    wr_hi = wr.astype(BF16)
    wr2 = jnp.stack([wr_hi, (wr - wr_hi.astype(F32)).astype(BF16)])
    t = b * s
    x1, h2, route2, counts, rows = _outproj(x.reshape(t, d), ao.reshape(t, ATTN_WIDTH),
                                            ho.reshape(t, HGRN_WIDTH), mod, w_out[l].astype(BF16),
                                            norm2_g[l].reshape(1, d), wr2, br, tm, s // tm)
    ntiles = t // MOE_TD
    counts_flat = counts.reshape(ntiles, LANES)[:, :2 * N_EXPERTS].reshape(-1)
    n_rows = _ceil_to(2 * t + ntiles * N_EXPERTS * (MOE_ALIGN - 1), MOE_TM) + N_EXPERTS * MOE_TM
    base, texp, meta, tail = _plan(counts_flat, ntiles, n_rows // MOE_TM)
    xs = _dispatch(counts_flat, base, tail, meta, h2,
                   rows.reshape(ntiles, SUBLANES, MOE_TD), n_rows)
    ys = _experts(texp, meta, xs, w_gate[l].astype(BF16), w_up[l].astype(BF16),
                  w_down[l].astype(BF16))
    out = _combine(counts_flat, base, x1, route2, mod, ys, s // MOE_TD)
    return out.reshape(b, s, d)
```

```python
import functools

import numpy as np
import jax
import jax.numpy as jnp
from jax import lax
from jax.experimental import pallas as pl
from jax.experimental.pallas import tpu as pltpu

F32 = jnp.float32
BF16 = jnp.bfloat16

D_MODEL = 1024
ATTN_HEAD_DIM = 64
ATTN_WIDTH = 512
ATTN_HEADS = 8
HGRN_WIDTH = 512
HGRN_HEADS = 4
HGRN_DK = 128
N_GROUPS = 4
EXPERTS_PER_GROUP = 4
N_EXPERTS = 16
D_EXPERT = 512
NORM_EPS = 1e-6
LANES = 128
SUBLANES = 8
VMEM_LIMIT = 56 * 1024 * 1024

HGRN_CHUNK = 128
HGRN_STEP = 512
HGRN_LEVELS = (1, 2, 4, 8, 16, 32, 64)
HGRN_MXU_LEVELS = (1, 2)
ROUTE_W_LANE = 2 * N_EXPERTS
ROUTE_ROW_LANE = ROUTE_W_LANE + 2
MOE_TD = 256
MOE_ALIGN = 16
MOE_TM = 1024
MOE_L = 2 * MOE_TD + MOE_ALIGN * N_EXPERTS
LOG2E = 1.4426950408889634
FOX_GROUP = 4
FOX_PIECES = 3
FOX_CK_WIDTH = (ATTN_HEADS // FOX_GROUP) * LANES


def _sigmoid(x):
    return 1.0 / (1.0 + jnp.exp(-x))


def _silu(x):
    return x * (0.5 * jnp.tanh(0.5 * x) + 0.5)


def _split3(x):
    p1 = x.astype(BF16)
    r1 = x - p1.astype(F32)
    p2 = r1.astype(BF16)
    p3 = (r1 - p2.astype(F32)).astype(BF16)
    return p1, p2, p3


def _dot(a, b):
    return jnp.dot(a, b, preferred_element_type=F32)


def _dot_nt(a, b):
    return lax.dot_general(a, b, (((1,), (1,)), ((), ())), preferred_element_type=F32)


def _ada_kernel(c_ref, w_ref, b_ref, o_ref):
    c = c_ref[...]
    o_ref[...] = jnp.dot(_silu(c), w_ref[...], preferred_element_type=F32,
                         precision=lax.Precision.HIGHEST) + b_ref[...]


def _ada(c, w, b):
    bsz, d = c.shape
    n = w.shape[1]
    tn = 1024
    return pl.pallas_call(
        _ada_kernel,
        grid=(n // tn,),
        in_specs=[pl.BlockSpec((bsz, d), lambda j: (0, 0)),
                  pl.BlockSpec((d, tn), lambda j: (0, j)),
                  pl.BlockSpec((1, tn), lambda j: (0, j))],
        out_specs=pl.BlockSpec((bsz, tn), lambda j: (0, j)),
        out_shape=jax.ShapeDtypeStruct((bsz, n), F32),
        compiler_params=pltpu.CompilerParams(dimension_semantics=("arbitrary",),
                                             vmem_limit_bytes=VMEM_LIMIT),
        name="ada",
    )(c, w, b.reshape(1, n))


def _inproj_kernel(x_ref, mod_ref, g1_ref, wq_ref, wk_ref, wv_ref, wf_ref, wh_ref,
                   gq_ref, gk_ref, bf_ref, gsum_ref, tri_ref, place_ref,
                   q_out, k_out, v_out, ck_out, hq_out, hf_out, hi_out, hg_out,
                   carry_ref):
    si = pl.program_id(1)

    @pl.when(si == 0)
    def _():
        carry_ref[...] = jnp.zeros_like(carry_ref)

    x = x_ref[0]
    shift = mod_ref[0, 0:1, :]
    scale = mod_ref[0, 1:2, :]
    ms = jnp.mean(x * x, axis=-1, keepdims=True)
    h = (x * lax.rsqrt(ms + NORM_EPS) * g1_ref[...]) * (1.0 + scale) + shift
    hb = h.astype(BF16)

    def qk_norm(w_ref, g_ref, mult):
        t = _dot(hb, w_ref[...])
        ssq = _dot((t * t).astype(BF16), gsum_ref[...])
        return t * lax.rsqrt(ssq * (1.0 / ATTN_HEAD_DIM) + NORM_EPS) * (g_ref[...] * mult)

    q_out[0] = qk_norm(wq_ref, gq_ref, ATTN_HEAD_DIM ** -0.5 * LOG2E).T.astype(BF16)
    k_out[0] = qk_norm(wk_ref, gk_ref, 1.0).astype(BF16)
    v_out[0] = _dot(hb, wv_ref[...]).T.astype(BF16)

    af = _dot(hb, wf_ref[...]) + bf_ref[...]
    lf = jnp.minimum(af, 0.0) - jnp.log(1.0 + jnp.exp(-jnp.abs(af)))
    tri = tri_ref[...]
    p1, p2, p3 = _split3(lf)
    cum = (_dot(tri, p1) + _dot(tri, p2)) + _dot(tri, p3) + carry_ref[...]
    tm = cum.shape[0]
    carry_ref[...] = cum[tm - 1:tm, :]
    c1, c2, c3 = _split3(cum * (-LOG2E))
    lane = lax.broadcasted_iota(jnp.int32, cum.shape, 1)
    zero = jnp.zeros_like(c1)
    pieces = jnp.where(lane < ATTN_HEADS, c1,
                       jnp.where(lane < 2 * ATTN_HEADS, c2,
                                 jnp.where(lane < 3 * ATTN_HEADS, c3, zero)))
    ck_out[0] = _dot(pieces, place_ref[...]).astype(BF16)

    hq_out[0] = _dot(hb, wh_ref[:, 0 * HGRN_WIDTH:1 * HGRN_WIDTH]).astype(BF16)
    hf_out[0] = _dot(hb, wh_ref[:, 1 * HGRN_WIDTH:2 * HGRN_WIDTH]).astype(BF16)
    hi_out[0] = _dot(hb, wh_ref[:, 2 * HGRN_WIDTH:3 * HGRN_WIDTH]).astype(BF16)
    hg_out[0] = _dot(hb, wh_ref[:, 3 * HGRN_WIDTH:4 * HGRN_WIDTH]).astype(BF16)


def _inproj(x, mod, g1, wq, wk, wv, wf, wh, gq, gk, bfox, tm):
    b, s, d = x.shape
    gsum = jnp.asarray(np.kron(np.eye(ATTN_HEADS), np.ones((ATTN_HEAD_DIM, ATTN_HEAD_DIM))), BF16)
    tri = jnp.asarray(np.tril(np.ones((tm, tm))), BF16)
    place_np = np.zeros((LANES, FOX_CK_WIDTH), np.float32)
    for piece in range(FOX_PIECES):
        for hd in range(ATTN_HEADS):
            col = (hd // FOX_GROUP) * LANES + (hd % FOX_GROUP) * FOX_PIECES + piece
            place_np[piece * ATTN_HEADS + hd, col] = 1.0
    place = jnp.asarray(place_np, BF16)
    const = lambda shape: pl.BlockSpec(shape, lambda bi, si: (0,) * len(shape))
    tok = lambda w: pl.BlockSpec((1, tm, w), lambda bi, si: (bi, si, 0))
    tok_t = lambda w: pl.BlockSpec((1, w, tm), lambda bi, si: (bi, 0, si))
    act = lambda w: jax.ShapeDtypeStruct((b, s, w), BF16)
    act_t = lambda w: jax.ShapeDtypeStruct((b, w, s), BF16)
    return pl.pallas_call(
        _inproj_kernel,
        grid=(b, s // tm),
        in_specs=[tok(d),
                  pl.BlockSpec((1, 6, d), lambda bi, si: (bi, 0, 0)),
                  const((1, d)),
                  const((d, ATTN_WIDTH)), const((d, ATTN_WIDTH)), const((d, ATTN_WIDTH)),
                  const((d, LANES)), const((d, 4 * HGRN_WIDTH)),
                  const((1, ATTN_WIDTH)), const((1, ATTN_WIDTH)), const((1, LANES)),
                  const((ATTN_WIDTH, ATTN_WIDTH)), const((tm, tm)),
                  const((LANES, FOX_CK_WIDTH))],
        out_specs=[tok_t(ATTN_WIDTH), tok(ATTN_WIDTH), tok_t(ATTN_WIDTH),
                   tok(FOX_CK_WIDTH),
                   tok(HGRN_WIDTH), tok(HGRN_WIDTH), tok(HGRN_WIDTH), tok(HGRN_WIDTH)],
        out_shape=[act_t(ATTN_WIDTH), act(ATTN_WIDTH), act_t(ATTN_WIDTH),
                   act(FOX_CK_WIDTH),
                   act(HGRN_WIDTH), act(HGRN_WIDTH), act(HGRN_WIDTH), act(HGRN_WIDTH)],
        scratch_shapes=[pltpu.VMEM((1, LANES), F32)],
        compiler_params=pltpu.CompilerParams(dimension_semantics=("arbitrary", "arbitrary"),
                                             vmem_limit_bytes=VMEM_LIMIT),
        name="inproj",
    )(x, mod, g1, wq, wk, wv, wf, wh, gq, gk, bfox, gsum, tri, place)


def _fox_kernel(qt_ref, k_ref, vt_ref, ck_ref, g_ref, o_ref, st_scr, pt_scr, *, tq, tk):
    qi = pl.program_id(2)
    qt = qt_ref[0].astype(F32)
    row = lax.broadcasted_iota(jnp.int32, (LANES, tq), 0)
    first = row < ATTN_HEAD_DIM
    rhs = []
    for hd in range(FOX_GROUP):
        qp = qt[(hd // 2) * LANES:(hd // 2 + 1) * LANES, :]
        qh = jnp.where(first, qp, 0.0) if hd % 2 == 0 else jnp.where(first, 0.0, qp)
        ones = jnp.where((row >= hd * FOX_PIECES) & (row < (hd + 1) * FOX_PIECES), 1.0, 0.0)
        rhs.append(jnp.concatenate([qh, ones], axis=0).astype(BF16))

    def stage_qk(j, slot):
        k0 = pl.multiple_of(j * tk, tk)
        kb = k_ref[0, pl.ds(k0, tk), :]
        ckb = ck_ref[0, pl.ds(k0, tk), :]
        for hd in range(FOX_GROUP):
            pr = slice((hd // 2) * LANES, (hd // 2 + 1) * LANES)
            lhs = jnp.concatenate([kb[:, pr], ckb], axis=1)
            st_scr[slot, hd] = _dot(lhs, rhs[hd])

    def stage_softmax(j, slot, stats, masked):
        out = []
        for hd in range(FOX_GROUP):
            m, l = stats[hd]
            st = st_scr[slot, hd]
            if masked:
                key = j * tk + lax.broadcasted_iota(jnp.int32, (tk, tq), 0)
                qry = qi * tq + lax.broadcasted_iota(jnp.int32, (tk, tq), 1)
                st = jnp.where(key <= qry, st, -jnp.inf)
            m_new = jnp.maximum(m, jnp.max(st, axis=0, keepdims=True))
            alpha = jnp.exp2(m - m_new)
            pt = jnp.exp2(st - m_new)
            pt_scr[slot, hd] = pt.astype(BF16)
            out.append(((m_new, alpha * l + jnp.sum(pt, axis=0, keepdims=True)), alpha))
        return tuple(o[0] for o in out), tuple(o[1] for o in out)

    def stage_pv(j, slot, accs, alphas):
        k0 = pl.multiple_of(jnp.maximum(j, 0) * tk, tk)
        vtb = vt_ref[0, :, pl.ds(k0, tk)]
        out = []
        for hd in range(FOX_GROUP):
            vth = vtb[hd * ATTN_HEAD_DIM:(hd + 1) * ATTN_HEAD_DIM, :]
            out.append(alphas[hd] * accs[hd] + _dot(vth, pt_scr[slot, hd]))
        return tuple(out)

    per_q = tq // tk
    stats = tuple((jnp.full((1, tq), -1e30, F32), jnp.zeros((1, tq), F32))
                  for _ in range(FOX_GROUP))
    accs = tuple(jnp.zeros((ATTN_HEAD_DIM, tq), F32) for _ in range(FOX_GROUP))
    alphas = tuple(jnp.ones((1, tq), F32) for _ in range(FOX_GROUP))
    pt_scr[1] = jnp.zeros_like(pt_scr[1])
    stage_qk(0, 0)

    def full_blocks(i, carry):
        stats, accs, alphas = carry
        for u in range(per_q):
            j = i * per_q + u
            accs = stage_pv(j - 1, 1 - u, accs, alphas)
            stats, alphas = stage_softmax(j, u, stats, False)
            stage_qk(j + 1, 1 - u)
        return stats, accs, alphas

    stats, accs, alphas = lax.fori_loop(0, qi, full_blocks, (stats, accs, alphas))
    j = qi * per_q
    accs = stage_pv(j - 1, 1, accs, alphas)
    stats, alphas = stage_softmax(j, 0, stats, True)
    stage_qk(j + 1, 1)
    accs = stage_pv(j, 0, accs, alphas)
    stats, alphas = stage_softmax(j + 1, 1, stats, True)
    accs = stage_pv(j + 1, 1, accs, alphas)
    carry = tuple((stats[hd][0], stats[hd][1], accs[hd]) for hd in range(FOX_GROUP))

    head0 = lax.broadcasted_iota(jnp.int32, (tq, LANES), 1) < ATTN_HEAD_DIM
    for pr in range(FOX_GROUP // 2):
        (_, l0, a0), (_, l1, a1) = carry[2 * pr], carry[2 * pr + 1]
        ot = jnp.concatenate([a0 * (1.0 / l0), a1 * (1.0 / l1)], axis=0)
        o = ot.T
        osq = o * o
        ss0 = jnp.sum(jnp.where(head0, osq, 0.0), axis=-1, keepdims=True)
        ss1 = jnp.sum(jnp.where(head0, 0.0, osq), axis=-1, keepdims=True)
        ms = jnp.where(head0, ss0, ss1) * (1.0 / ATTN_HEAD_DIM)
        sl = slice(pr * LANES, (pr + 1) * LANES)
        o_ref[0, :, sl] = (o * lax.rsqrt(ms + NORM_EPS) * g_ref[:, sl]).astype(BF16)


def _fox(qt, k, vt, ck, g_out, tq, tk):
    b, s, _ = k.shape
    groups = ATTN_HEADS // FOX_GROUP
    gw = FOX_GROUP * ATTN_HEAD_DIM
    return pl.pallas_call(
        functools.partial(_fox_kernel, tq=tq, tk=tk),
        grid=(b, groups, s // tq),
        in_specs=[pl.BlockSpec((1, gw, tq), lambda bi, g, qi: (bi, g, qi)),
                  pl.BlockSpec((1, s, gw), lambda bi, g, qi: (bi, 0, g)),
                  pl.BlockSpec((1, gw, s), lambda bi, g, qi: (bi, g, 0)),
                  pl.BlockSpec((1, s, LANES), lambda bi, g, qi: (bi, 0, g)),
                  pl.BlockSpec((1, gw), lambda bi, g, qi: (0, g))],
        out_specs=pl.BlockSpec((1, tq, gw), lambda bi, g, qi: (bi, qi, g)),
        out_shape=jax.ShapeDtypeStruct((b, s, ATTN_WIDTH), BF16),
        scratch_shapes=[pltpu.VMEM((2, FOX_GROUP, tk, tq), F32),
                        pltpu.VMEM((2, FOX_GROUP, tk, tq), BF16)],
        compiler_params=pltpu.CompilerParams(
            dimension_semantics=("arbitrary", "arbitrary", "arbitrary"),
            vmem_limit_bytes=VMEM_LIMIT),
        name="fox",
    )(qt, k, vt, ck, g_out)


def _hgrn_decay_matrix(c):
    t = np.arange(c)[:, None]
    j = np.arange(c)[None, :]
    blocks = [(j <= t)]
    for m in HGRN_MXU_LEVELS:
        mid = (t // (2 * m)) * (2 * m) + m
        right = (t % (2 * m)) >= m
        blocks.append(np.where(right, (j >= mid) & (j <= t), (j > t) & (j < mid)))
    return np.concatenate(blocks, axis=0).astype(np.float32)


def _hgrn_kernel(hq_ref, hf_ref, hi_ref, hg_ref, lb_ref, g_ref, w_ref, lvl_ref, o_ref, st_ref):
    ci = pl.program_id(1)
    c = HGRN_CHUNK

    @pl.when(ci == 0)
    def _():
        st_ref[...] = jnp.zeros_like(st_ref)

    r0 = lb_ref[0:1, :]
    r1 = lb_ref[1:2, :]
    rmax = jnp.maximum(r0, r1)
    e0 = jnp.exp(r0 - rmax)
    lb = e0 / (e0 + jnp.exp(r1 - rmax))

    lvl = lvl_ref[...]
    at_level = [lvl == float(li) for li in range(len(HGRN_LEVELS))]
    for sub in range(hq_ref.shape[1] // c):
        rows = slice(sub * c, (sub + 1) * c)
        f = lb + (1.0 - lb) * _sigmoid(hf_ref[0, rows, :].astype(F32))
        g = jnp.log2(f)
        g1 = g.astype(BF16)
        g2 = (g - g1.astype(F32)).astype(BF16)
        w = w_ref[...]
        xw = _dot(w, g1) + _dot(w, g2)
        cum = xw[0:c]
        cum3 = cum.reshape(c // SUBLANES, SUBLANES, cum.shape[1])

        def level_sums(m):
            if m in HGRN_MXU_LEVELS:
                at = 1 + HGRN_MXU_LEVELS.index(m)
                return xw[at * c:(at + 1) * c]
            if 2 * m == SUBLANES:
                ref = cum3[:, m - 1:m, :]
            else:
                last = cum3[:, SUBLANES - 1:SUBLANES, :]
                per = 2 * m // SUBLANES
                pick = [(r // per) * per + per // 2 - 1 for r in range(c // SUBLANES)]
                ref = jnp.concatenate([last[p:p + 1] for p in pick], axis=0)
            return -jnp.abs(cum - jnp.broadcast_to(ref, cum3.shape).reshape(cum.shape))

        e_levels = [jnp.exp2(level_sums(m)) for m in HGRN_LEVELS]
        e_pre_all = jnp.exp2(cum)
        e_suf_all = jnp.exp2(cum[c - 1:c, :] - cum)
        q_all = _silu(hq_ref[0, rows, :].astype(F32))
        k_all = 1.0 - f
        v_all = hi_ref[0, rows, :].astype(F32)


        for hd in range(HGRN_HEADS):
            sl = slice(hd * HGRN_DK, (hd + 1) * HGRN_DK)
            q, k, v = q_all[:, sl], k_all[:, sl], v_all[:, sl]
            e_pre = e_pre_all[:, sl]
            e_suf = e_suf_all[:, sl]

            a = jnp.zeros((c, c), F32)
            for li in range(len(HGRN_LEVELS)):
                e = e_levels[li][:, sl]
                a = jnp.where(at_level[li], _dot_nt((q * e).astype(BF16), (k * e).astype(BF16)), a)
            vb = v.astype(BF16)
            out = _dot(a.astype(BF16), vb)

            out = out + jnp.sum(q * k, axis=-1, keepdims=True) * v

            st = st_ref[hd]
            out = out + _dot_nt((q * e_pre).astype(BF16), st.astype(BF16))
            kdec = (k * e_suf).astype(BF16)
            upd = lax.dot_general(vb, kdec, (((0,), (0,)), ((), ())), preferred_element_type=F32)
            st_ref[hd] = st * e_pre[c - 1:c, :] + upd

            ms = jnp.mean(out * out, axis=-1, keepdims=True)
            gate = _silu(hg_ref[0, rows, sl].astype(F32))
            o_ref[0, rows, sl] = (out * lax.rsqrt(ms + NORM_EPS) * g_ref[:, sl] * gate).astype(BF16)


def _hgrn(hq, hf, hi, hg, lb_rows, g_out):
    b, s, wd = hq.shape
    c = HGRN_CHUNK
    wmat = jnp.asarray(_hgrn_decay_matrix(c), BF16)
    tt = np.arange(c)[:, None]
    ss = np.arange(c)[None, :]
    lvl_np = np.full((c, c), -1.0, np.float32)
    for li, m in reversed(list(enumerate(HGRN_LEVELS))):
        lvl_np[(ss < tt) & (tt // (2 * m) == ss // (2 * m)) & (tt // m != ss // m)] = li
    lvl = jnp.asarray(lvl_np)
    tok = pl.BlockSpec((1, HGRN_STEP, wd), lambda bi, ci: (bi, ci, 0))
    return pl.pallas_call(
        _hgrn_kernel,
        grid=(b, s // HGRN_STEP),
        in_specs=[tok, tok, tok, tok,
                  pl.BlockSpec((2, wd), lambda bi, ci: (0, 0)),
                  pl.BlockSpec((1, wd), lambda bi, ci: (0, 0)),
                  pl.BlockSpec(wmat.shape, lambda bi, ci: (0, 0)),
                  pl.BlockSpec((c, c), lambda bi, ci: (0, 0))],
        out_specs=tok,
        out_shape=jax.ShapeDtypeStruct((b, s, wd), BF16),
        scratch_shapes=[pltpu.VMEM((HGRN_HEADS, HGRN_DK, HGRN_DK), F32)],
        compiler_params=pltpu.CompilerParams(dimension_semantics=("arbitrary", "arbitrary"),
                                             vmem_limit_bytes=VMEM_LIMIT),
        name="hgrn",
    )(hq, hf, hi, hg, lb_rows, g_out, wmat, lvl)


def _outproj_kernel(x_ref, ao_ref, ho_ref, mod_ref, wo_ref, g2_ref, wr_ref, br_ref,
                    lstrict_ref, fold_ref, upper_ref, selrows_ref,
                    x1_out, h2_out, route_out, cnt_out, rows_out, lg_scr):
    @pl.when(pl.program_id(0) == 0)
    def _():
        lg_scr[...] = jnp.zeros_like(lg_scr)

    prev_logits = lg_scr[...]
    gate1 = mod_ref[0, 2:3, :]
    shift2 = mod_ref[0, 3:4, :]
    scale2 = mod_ref[0, 4:5, :]
    mix = _dot(ao_ref[...], wo_ref[0:ATTN_WIDTH, :]) + _dot(ho_ref[...], wo_ref[ATTN_WIDTH:D_MODEL, :])
    x1 = x_ref[...] + gate1 * mix
    x1_out[...] = x1
    ms = jnp.mean(x1 * x1, axis=-1, keepdims=True)
    h2 = (x1 * lax.rsqrt(ms + NORM_EPS) * g2_ref[...]) * (1.0 + scale2) + shift2
    h2_out[...] = h2.astype(BF16)

    h2_hi = h2.astype(BF16)
    h2_lo = (h2 - h2_hi.astype(F32)).astype(BF16)
    lg_scr[...] = (_dot(h2_hi, wr_ref[0]) + (_dot(h2_hi, wr_ref[1]) + _dot(h2_lo, wr_ref[0]))) + br_ref[...]
    _route_tile(prev_logits, lstrict_ref, fold_ref, upper_ref, selrows_ref, route_out, cnt_out, rows_out)


def _route_tile(logits, lstrict_ref, fold_ref, upper_ref, selrows_ref, route_out, cnt_out, rows_out):
    tm = logits.shape[0]
    lane = lax.broadcasted_iota(jnp.int32, (tm, LANES), 1)
    neg = -jnp.inf
    is_group = (lane >= N_EXPERTS) & (lane < N_EXPERTS + N_GROUPS)
    gl = jnp.where(is_group, logits, neg)
    gmax = jnp.max(gl, axis=-1, keepdims=True)
    gsum = jnp.sum(jnp.exp(gl - gmax), axis=-1, keepdims=True)
    group_p = 1.0 / gsum
    gidx = jnp.min(jnp.where(is_group & (gl == gmax), lane, LANES), axis=-1, keepdims=True) - N_EXPERTS
    in_group = (lane < N_EXPERTS) & ((lane // EXPERTS_PER_GROUP) == gidx)
    el = jnp.where(in_group, logits, neg)
    top1 = jnp.max(el, axis=-1, keepdims=True)
    idx1 = jnp.min(jnp.where(in_group & (el == top1), lane, LANES), axis=-1, keepdims=True)
    el2 = jnp.where(lane == idx1, neg, el)
    top2 = jnp.max(el2, axis=-1, keepdims=True)
    idx2 = jnp.min(jnp.where(in_group & (lane != idx1) & (el2 == top2), lane, LANES),
                   axis=-1, keepdims=True)
    e2 = jnp.exp(top2 - top1)
    w1 = group_p / (1.0 + e2)
    w2 = group_p * e2 / (1.0 + e2)
    route = jnp.where(lane == idx1, 1.0,
                      jnp.where(lane == idx2 + N_EXPERTS, 1.0,
                                jnp.where(lane == ROUTE_W_LANE, w1,
                                          jnp.where(lane == ROUTE_W_LANE + 1, w2, 0.0))))
    sel = selrows_ref[...]
    ln = lax.broadcasted_iota(jnp.int32, (MOE_TD, LANES), 1)
    for sub in range(tm // MOE_TD):
        tile = slice(sub * MOE_TD, (sub + 1) * MOE_TD)
        rt = route[tile]
        cnt_out[0, sub:sub + 1, :] = jnp.sum(rt, axis=0, keepdims=True).astype(jnp.int32)
        v = _local_slots(rt, lstrict_ref[...], fold_ref[...], upper_ref[...])
        hi = jnp.floor(v * (1.0 / MOE_ALIGN))
        lo = v - hi * MOE_ALIGN
        rows_out[0, sub * SUBLANES:(sub + 1) * SUBLANES, :] = (
            _dot_nt(sel, hi.astype(BF16)) * MOE_ALIGN + _dot_nt(sel, lo.astype(BF16)))
        r1 = jnp.sum(jnp.where(ln < N_EXPERTS, v, 0.0), axis=-1, keepdims=True)
        r2 = jnp.sum(jnp.where(ln < N_EXPERTS, 0.0, v), axis=-1, keepdims=True)
        route_out[tile, :] = jnp.where(ln == ROUTE_ROW_LANE, r1,
                                       jnp.where(ln == ROUTE_ROW_LANE + 1, r2, rt))


def _outproj(x, ao, ho, mod, wo, g2, wr, br, tm, tiles_per_batch):
    t, d = x.shape
    n = t // tm
    sub = tm // MOE_TD
    const = lambda shape: pl.BlockSpec(shape, lambda i: (0,) * len(shape))
    cur = lambda w: pl.BlockSpec((tm, w), lambda i: (jnp.minimum(i, n - 1), 0))
    prev = lambda shape: pl.BlockSpec(shape, lambda i: (jnp.maximum(i - 1, 0),) + (0,) * (len(shape) - 1))
    return pl.pallas_call(
        _outproj_kernel,
        grid=(n + 1,),
        in_specs=[cur(d), cur(ATTN_WIDTH), cur(HGRN_WIDTH),
                  pl.BlockSpec((1, 6, d), lambda i: (jnp.minimum(i, n - 1) // tiles_per_batch, 0, 0)),
                  const((d, d)), const((1, d)), const((2, d, LANES)), const((1, LANES)),
                  const((MOE_TD, MOE_TD)), const((LANES, LANES)), const((LANES, LANES)),
                  const((SUBLANES, LANES))],
        out_specs=[cur(d), cur(d), prev((tm, LANES)), prev((1, sub, LANES)),
                   prev((1, sub * SUBLANES, MOE_TD))],
        out_shape=[jax.ShapeDtypeStruct((t, d), F32),
                   jax.ShapeDtypeStruct((t, d), BF16),
                   jax.ShapeDtypeStruct((t, LANES), F32),
                   jax.ShapeDtypeStruct((n, sub, LANES), jnp.int32),
                   jax.ShapeDtypeStruct((n, sub * SUBLANES, MOE_TD), F32)],
        scratch_shapes=[pltpu.VMEM((tm, LANES), F32)],
        compiler_params=pltpu.CompilerParams(dimension_semantics=("arbitrary",),
                                             vmem_limit_bytes=VMEM_LIMIT),
        name="outproj",
    )(x, ao, ho, mod, wo, g2, wr, br, *_moe_constants())


def _ceil_to(v, m):
    return ((v + (m - 1)) // m) * m


def _cdiv_pow2(v, m):
    return lax.shift_right_logical(v + (m - 1), m.bit_length() - 1)


def _moe_constants():
    a = np.arange(LANES)
    ne = N_EXPERTS
    td = MOE_TD
    lstrict = np.tril(np.ones((td, td)), -1)
    fold = ((a[:, None] < 2 * ne) & (a[None, :] < 2 * ne) & (a[:, None] % ne == a[None, :] % ne))
    upper = ((a[:, None] < ne) & (a[None, :] < 2 * ne) & (a[:, None] < a[None, :] % ne))
    selrows = np.zeros((SUBLANES, LANES))
    selrows[0, :ne] = 1.0
    selrows[1, ne:2 * ne] = 1.0
    return tuple(jnp.asarray(m, BF16) for m in (lstrict, fold, upper, selrows))


def _local_slots(route, lstrict, fold, upper):
    lane = lax.broadcasted_iota(jnp.int32, route.shape, 1)
    member = jnp.where(lane < 2 * N_EXPERTS, route, 0.0)
    rank = _dot(_dot(lstrict, member.astype(BF16)).astype(BF16), fold)
    cnt = jnp.broadcast_to(jnp.sum(member, axis=0, keepdims=True), (SUBLANES, LANES))
    cnt = _dot(cnt.astype(BF16), fold)
    run = jnp.floor((cnt + (MOE_ALIGN - 1)) * (1.0 / MOE_ALIGN)) * MOE_ALIGN
    start = _dot(run.astype(BF16), upper)[0:1, :]
    return member * (start + rank)


def _plan_kernel(c_ref, base_ref, texp_ref, meta_ref, tail_ref, *, ntiles, n_row_tiles):
    ne = N_EXPERTS
    off = jnp.int32(0)
    for e in range(ne):
        def body(i, run, e=e, off=off):
            c = c_ref[i * 2 * ne + e] + c_ref[i * 2 * ne + ne + e]
            base_ref[i * ne + e] = off + run
            return run + _cdiv_pow2(c, MOE_ALIGN) * MOE_ALIGN
        total = lax.fori_loop(0, ntiles, body, jnp.int32(0), unroll=8)
        nt = (total + (MOE_TM - 1)) // MOE_TM
        first = off // MOE_TM

        def fill(j, carry, e=e, first=first):
            texp_ref[first + j] = e
            return carry
        lax.fori_loop(0, nt, fill, 0)
        tail_ref[e] = off + total
        tail_ref[ne + e] = (nt * MOE_TM - total) // MOE_ALIGN
        off = off + nt * MOE_TM
    nvalid = off // MOE_TM

    def fill_rest(j, carry):
        texp_ref[j] = ne - 1
        return carry
    lax.fori_loop(nvalid, n_row_tiles, fill_rest, 0)
    meta_ref[0] = nvalid


def _plan(counts_flat, ntiles, n_row_tiles):
    smem = pl.BlockSpec(memory_space=pltpu.SMEM)
    return pl.pallas_call(
        functools.partial(_plan_kernel, ntiles=ntiles, n_row_tiles=n_row_tiles),
        in_specs=[smem],
        out_specs=[smem, smem, smem, smem],
        out_shape=[jax.ShapeDtypeStruct((ntiles * N_EXPERTS,), jnp.int32),
                   jax.ShapeDtypeStruct((n_row_tiles,), jnp.int32),
                   jax.ShapeDtypeStruct((1,), jnp.int32),
                   jax.ShapeDtypeStruct((2 * N_EXPERTS,), jnp.int32)],
        name="moe_plan",
    )(counts_flat)


def _run_chunks(c_ref, base_ref, tile, fn):
    ne = N_EXPERTS
    local = jnp.int32(0)
    for e in range(ne):
        c = c_ref[tile * 2 * ne + e] + c_ref[tile * 2 * ne + ne + e]
        nchunk = _cdiv_pow2(c, MOE_ALIGN)
        hbm = base_ref[tile * ne + e]

        def body(j, carry, hbm=hbm, local=local, e=e):
            fn(pl.multiple_of(hbm + j * MOE_ALIGN, MOE_ALIGN),
               pl.multiple_of(local + j * MOE_ALIGN, MOE_ALIGN), e % 2)
            return carry
        lax.fori_loop(0, nchunk, body, 0)
        local = local + nchunk * MOE_ALIGN
    return _cdiv_pow2(local, MOE_ALIGN)


def _dispatch_kernel(c_ref, base_ref, tail_ref, meta_ref, h2_ref, rows_ref, xs_ref,
                     xs_scr, zero_scr, sem, nch_ref, *, n_row_tiles):
    i = pl.program_id(0)
    n = pl.num_programs(0)
    slot = i % 2

    def copy(slot_, hbm, local):
        return pltpu.make_async_copy(xs_scr.at[slot_, pl.ds(local, MOE_ALIGN), :],
                                     xs_ref.at[pl.ds(hbm, MOE_ALIGN), :], sem.at[slot_])

    def wait_all(slot_):
        def body(j, carry):
            copy(slot_, 0, 0).wait()
            return carry
        lax.fori_loop(0, nch_ref[slot_], body, 0)

    @pl.when(i >= 2)
    def _():
        wait_all(slot)

    rows = rows_ref[0]
    slot_id = lax.broadcasted_iota(jnp.int32, (MOE_L, MOE_TD), 0)
    p1 = jnp.where(slot_id == rows[0:1, :].astype(jnp.int32), 1.0, 0.0)
    p2 = jnp.where(slot_id == rows[1:2, :].astype(jnp.int32), 1.0, 0.0)
    xs_scr[slot] = _dot((p1 + p2).astype(BF16), h2_ref[...]).astype(BF16)
    nch_ref[slot] = _run_chunks(c_ref, base_ref, i,
                                lambda hbm, local, prio: copy(slot, hbm, local).start(priority=prio))

    @pl.when(i == n - 1)
    def _():
        zero_scr[...] = jnp.zeros_like(zero_scr)

        def chunk(hbm):
            return pltpu.make_async_copy(zero_scr.at[pl.ds(0, MOE_ALIGN), :],
                                         xs_ref.at[pl.ds(hbm, MOE_ALIGN), :], sem.at[2])

        def tile(j):
            return pltpu.make_async_copy(zero_scr, xs_ref.at[pl.ds(j * MOE_TM, MOE_TM), :], sem.at[3])

        nchunks = jnp.int32(0)
        for e in range(N_EXPERTS):
            start = tail_ref[e]
            cnt = tail_ref[N_EXPERTS + e]

            def fill(j, carry, start=start):
                chunk(pl.multiple_of(start + j * MOE_ALIGN, MOE_ALIGN)).start()
                return carry
            lax.fori_loop(0, cnt, fill, 0)
            nchunks = nchunks + cnt

        def fill_tile(j, carry):
            tile(j).start()
            return carry
        lax.fori_loop(meta_ref[0], n_row_tiles, fill_tile, 0)

        wait_all(slot)

        @pl.when(n >= 2)
        def _():
            wait_all(1 - slot)

        def wait_chunk(j, carry):
            chunk(0).wait()
            return carry
        lax.fori_loop(0, nchunks, wait_chunk, 0)

        def wait_tile(j, carry):
            tile(0).wait()
            return carry
        lax.fori_loop(meta_ref[0], n_row_tiles, wait_tile, 0)


def _dispatch(counts_flat, base, tail, meta, h2, rows, n_rows):
    t, d = h2.shape
    ntiles = t // MOE_TD
    grid_spec = pltpu.PrefetchScalarGridSpec(
        num_scalar_prefetch=4,
        grid=(ntiles,),
        in_specs=[pl.BlockSpec((MOE_TD, d), lambda i, *_: (i, 0)),
                  pl.BlockSpec((1, SUBLANES, MOE_TD), lambda i, *_: (i, 0, 0))],
        out_specs=pl.BlockSpec(memory_space=pl.ANY),
        scratch_shapes=[pltpu.VMEM((2, MOE_L, d), BF16), pltpu.VMEM((MOE_TM, d), BF16),
                        pltpu.SemaphoreType.DMA((4,)), pltpu.SMEM((2,), jnp.int32)])
    return pl.pallas_call(
        functools.partial(_dispatch_kernel, n_row_tiles=n_rows // MOE_TM),
        grid_spec=grid_spec,
        out_shape=jax.ShapeDtypeStruct((n_rows, d), BF16),
        compiler_params=pltpu.CompilerParams(dimension_semantics=("arbitrary",),
                                             vmem_limit_bytes=VMEM_LIMIT),
        name="moe_dispatch",
    )(counts_flat, base, tail, meta, h2, rows)


def _experts_kernel(texp_ref, meta_ref, xs_ref, wg_ref, wu_ref, wd_ref, ys_ref):
    del texp_ref

    @pl.when(pl.program_id(0) >= meta_ref[0])
    def _():
        ys_ref[...] = jnp.zeros_like(ys_ref)

    @pl.when(pl.program_id(0) < meta_ref[0])
    def _():
        x = xs_ref[...]
        act = _silu(_dot(x, wg_ref[0])) * _dot(x, wu_ref[0])
        ys_ref[...] = _dot(act.astype(BF16), wd_ref[0]).astype(BF16)


def _experts(texp, meta, xs, wg, wu, wd):
    n_rows, d = xs.shape
    row = lambda w: pl.BlockSpec((MOE_TM, w), lambda i, te, mt: (jnp.minimum(i, mt[0] - 1), 0))
    wspec = lambda shape: pl.BlockSpec((1,) + shape, lambda i, te, mt: (te[i], 0, 0))
    grid_spec = pltpu.PrefetchScalarGridSpec(
        num_scalar_prefetch=2,
        grid=(n_rows // MOE_TM,),
        in_specs=[row(d), wspec((d, D_EXPERT)), wspec((d, D_EXPERT)), wspec((D_EXPERT, d))],
        out_specs=pl.BlockSpec((MOE_TM, d), lambda i, te, mt: (i, 0)))
    return pl.pallas_call(
        _experts_kernel,
        grid_spec=grid_spec,
        out_shape=jax.ShapeDtypeStruct((n_rows, d), BF16),
        compiler_params=pltpu.CompilerParams(dimension_semantics=("arbitrary",),
                                             vmem_limit_bytes=VMEM_LIMIT),
        name="moe_experts",
    )(texp, meta, xs, wg, wu, wd)


def _combine_kernel(c_ref, base_ref, x1_ref, route_ref, mod_ref, ys_ref, o_ref, ys_scr, sem,
                    nch_ref):
    i = pl.program_id(0)
    n = pl.num_programs(0)
    slot = i % 2

    def copy(slot_, hbm, local):
        return pltpu.make_async_copy(ys_ref.at[pl.ds(hbm, MOE_ALIGN), :],
                                     ys_scr.at[slot_, pl.ds(local, MOE_ALIGN), :], sem.at[slot_])

    def fetch(tile, slot_):
        nch_ref[slot_] = _run_chunks(c_ref, base_ref, tile,
                                     lambda hbm, local, prio: copy(slot_, hbm, local).start(priority=prio))

    @pl.when(i == 0)
    def _():
        ys_scr[...] = jnp.zeros_like(ys_scr)
        fetch(0, 0)

    @pl.when(i + 1 < n)
    def _():
        fetch(i + 1, 1 - slot)

    route = route_ref[...]
    lane = lax.broadcasted_iota(jnp.int32, route.shape, 1)
    r1 = jnp.sum(jnp.where(lane == ROUTE_ROW_LANE, route, 0.0), axis=-1, keepdims=True)
    r2 = jnp.sum(jnp.where(lane == ROUTE_ROW_LANE + 1, route, 0.0), axis=-1, keepdims=True)
    r1, r2 = r1.astype(jnp.int32), r2.astype(jnp.int32)
    col = lax.broadcasted_iota(jnp.int32, (MOE_TD, MOE_L), 1)
    pick1 = jnp.where(col == r1, 1.0, 0.0).astype(BF16)
    pick2 = jnp.where(col == r2, 1.0, 0.0).astype(BF16)
    w1 = jnp.sum(jnp.where(lane == ROUTE_W_LANE, route, 0.0), axis=-1, keepdims=True)
    w2 = jnp.sum(jnp.where(lane == ROUTE_W_LANE + 1, route, 0.0), axis=-1, keepdims=True)

    def wait(j, carry):
        copy(slot, 0, 0).wait()
        return carry
    lax.fori_loop(0, nch_ref[slot], wait, 0)
    ys = ys_scr[slot]
    y = w1 * _dot(pick1, ys) + w2 * _dot(pick2, ys)
    o_ref[...] = x1_ref[...] + mod_ref[0, 5:6, :] * y


def _combine(counts_flat, base, x1, route, mod, ys, tiles_per_batch):
    t, d = x1.shape
    tok = lambda w: pl.BlockSpec((MOE_TD, w), lambda i, c, b: (i, 0))
    grid_spec = pltpu.PrefetchScalarGridSpec(
        num_scalar_prefetch=2,
        grid=(t // MOE_TD,),
        in_specs=[tok(d), tok(LANES),
                  pl.BlockSpec((1, 6, d), lambda i, c, b: (i // tiles_per_batch, 0, 0)),
                  pl.BlockSpec(memory_space=pl.ANY)],
        out_specs=tok(d),
        scratch_shapes=[pltpu.VMEM((2, MOE_L, d), BF16), pltpu.SemaphoreType.DMA((2,)),
                        pltpu.SMEM((2,), jnp.int32)])
    return pl.pallas_call(
        _combine_kernel,
        grid_spec=grid_spec,
        out_shape=jax.ShapeDtypeStruct((t, d), F32),
        compiler_params=pltpu.CompilerParams(dimension_semantics=("arbitrary",),
                                             vmem_limit_bytes=VMEM_LIMIT),
        name="moe_combine",
    )(counts_flat, base, x1, route, mod, ys)


def kernel(x, c, w_ada, b_ada, norm1_g, w_in, b_fox, q_norm_g, k_norm_g, attn_out_g, hgrn_lb,
           hgrn_out_g, w_out, norm2_g, w_router_group, b_router_group, w_router_expert,
           b_router_expert, w_gate, w_up, w_down):
    b, s, d = x.shape
    l = 0
    aw = ATTN_WIDTH
    mod = _ada(c, w_ada[l], b_ada[l]).reshape(b, 6, d)

    w = w_in[l]
    wq = w[:, 0:aw].astype(BF16)
    wk = w[:, aw:2 * aw].astype(BF16)
    wv = w[:, 2 * aw:3 * aw].astype(BF16)
    f0 = 3 * aw
    pad_f = LANES - FOX_PIECES * ATTN_HEADS
    wf = jnp.pad(jnp.tile(w[:, f0:f0 + ATTN_HEADS], (1, FOX_PIECES)), ((0, 0), (0, pad_f))).astype(BF16)
    wh = w[:, f0 + ATTN_HEADS:].astype(BF16)
    bfox = jnp.pad(jnp.tile(b_fox[l], FOX_PIECES), (0, pad_f)).reshape(1, LANES)
    gq = jnp.tile(q_norm_g[l], ATTN_HEADS).reshape(1, aw)
    gk = jnp.tile(k_norm_g[l], ATTN_HEADS).reshape(1, aw)

    tm = min(512, s)
    q, k, v, ck, hq, hf, hi, hg = _inproj(x, mod, norm1_g[l].reshape(1, d), wq, wk, wv, wf, wh,
                                           gq, gk, bfox, tm)
    ao = _fox(q, k, v, ck, attn_out_g[l].reshape(1, aw), min(256, s), 128)
    ho = _hgrn(hq, hf, hi, hg, hgrn_lb[0:2], hgrn_out_g[l].reshape(1, HGRN_WIDTH))

    wr = jnp.pad(jnp.concatenate([w_router_expert[l], w_router_group[l]], axis=1),
                 ((0, 0), (0, LANES - N_GROUPS - N_EXPERTS)))
    br = jnp.pad(jnp.concatenate([b_router_expert[l], b_router_group[l]]),
                 (0, LANES - N_GROUPS - N_EXPERTS)).reshape(1, LANES)
    wr_hi = wr.astype(BF16)
    wr2 = jnp.stack([wr_hi, (wr - wr_hi.astype(F32)).astype(BF16)])
    t = b * s
    x1, h2, route2, counts, rows = _outproj(x.reshape(t, d), ao.reshape(t, ATTN_WIDTH),
                                            ho.reshape(t, HGRN_WIDTH), mod, w_out[l].astype(BF16),
                                            norm2_g[l].reshape(1, d), wr2, br, tm, s // tm)
    ntiles = t // MOE_TD
    counts_flat = counts.reshape(ntiles, LANES)[:, :2 * N_EXPERTS].reshape(-1)
    n_rows = _ceil_to(2 * t + ntiles * N_EXPERTS * (MOE_ALIGN - 1), MOE_TM) + N_EXPERTS * MOE_TM
    base, texp, meta, tail = _plan(counts_flat, ntiles, n_rows // MOE_TM)
    xs = _dispatch(counts_flat, base, tail, meta, h2,
                   rows.reshape(ntiles, SUBLANES, MOE_TD), n_rows)
    ys = _experts(texp, meta, xs, w_gate[l].astype(BF16), w_up[l].astype(BF16),
                  w_down[l].astype(BF16))
    out = _combine(counts_flat, base, x1, route2, mod, ys, s // MOE_TD)
    return out.reshape(b, s, d)
```
